```python
import math
import functools
import jax
import jax.numpy as jnp
from jax import lax
import numpy as np

D_MODEL = 4096
BATCH = 2
SEQ = 8192
DEPTH = 2

GRID_W = 64
CTX_LEN = 256

N_HEADS = 32
Q_LORA = 1024
KV_LORA = 512
QK_NOPE = 128
QK_ROPE = 64
V_HEAD = 128
ROPE_THETA = 10000.0
Q_BLOCK = 128
ATTN_SCALE = (QK_NOPE + QK_ROPE) ** -0.5

S5_WIDTH = 2048
S5_GROUP = 16
S5_GROUPS = S5_WIDTH // S5_GROUP
S5_STATE = 64
S5_CHUNK = 16
DT_MIN = 1e-3
DT_MAX = 1e-1

N_BRANCH = 2
O_CQ = 0
O_CKV = O_CQ + Q_LORA
O_KR = O_CKV + KV_LORA
O_U = O_KR + QK_ROPE
O_G = O_U + S5_WIDTH
IN_COLS = O_G + N_BRANCH * D_MODEL

D_FF = 11008
N_EXPERTS = 8
TOP_K = 2
D_EXPERT = 2048
N_DENSE = (DEPTH + 1) // 2
N_MOE = DEPTH // 2

ALPHA = (2 * DEPTH) ** 0.25
BETA = (8 * DEPTH) ** -0.25
LN_EPS = 1e-6
RMS_EPS = 1e-6

kernel_name = 'hybrid_mla_s5_moe_diffusion_block'


def ln_plain(x):
    xf = x.astype(jnp.float32)
    mu = jnp.mean(xf, axis=-1, keepdims=True)
    var = jnp.mean(jnp.square(xf - mu), axis=-1, keepdims=True)
    return ((xf - mu) * lax.rsqrt(var + LN_EPS)).astype(x.dtype)


def ln_affine(x, g, b):
    return ln_plain(x) * g + b


def rms_norm(x, g):
    xf = x.astype(jnp.float32)
    y = xf * lax.rsqrt(jnp.mean(jnp.square(xf), axis=-1, keepdims=True) + RMS_EPS)
    return y.astype(x.dtype) * g


def modulate(x, shift, scale):
    return ln_plain(x) * (1 + scale) + shift


def axial_rope_tables(n_tokens):
    rows = n_tokens // GRID_W
    row = jnp.repeat(jnp.arange(rows, dtype=jnp.float32), GRID_W)
    col = jnp.tile(jnp.arange(GRID_W, dtype=jnp.float32), rows)
    n_freq = QK_ROPE // 4
    inv_freq = ROPE_THETA ** (-jnp.arange(n_freq, dtype=jnp.float32) / n_freq)
    ang_r = row[:, None] * inv_freq
    ang_c = col[:, None] * inv_freq
    return (jnp.cos(ang_r), jnp.sin(ang_r), jnp.cos(ang_c), jnp.sin(ang_c))


def rotate_pairs(x, cos, sin):
    x1, x2 = jnp.split(x, 2, axis=-1)
    return jnp.concatenate([x1 * cos - x2 * sin, x2 * cos + x1 * sin], axis=-1)


def apply_axial_rope(x, tabs):
    shape = (1, x.shape[1]) + (1,) * (x.ndim - 3) + (tabs[0].shape[-1],)
    cr, sr, cc, sc = (t.reshape(shape).astype(x.dtype) for t in tabs)
    xr, xc = jnp.split(x, 2, axis=-1)
    return jnp.concatenate([rotate_pairs(xr, cr, sr), rotate_pairs(xc, cc, sc)], axis=-1)


def mla_q(cq, q_norm, w_uq):
    b, n, _ = cq.shape
    q = (rms_norm(cq, q_norm) @ w_uq).reshape(b, n, N_HEADS, QK_NOPE + QK_ROPE)
    return q[..., :QK_NOPE], q[..., QK_NOPE:]


def mla_kv(ckv, kv_norm, w_ukv):
    b, n, _ = ckv.shape
    kv = (rms_norm(ckv, kv_norm) @ w_ukv).reshape(b, n, N_HEADS, QK_NOPE + V_HEAD)
    return kv[..., :QK_NOPE], kv[..., QK_NOPE:]


def mla_attend(qn, qr, kn, kr, v):
    s = (jnp.einsum('bqhd,bkhd->bhqk', qn, kn, preferred_element_type=jnp.float32)
         + jnp.einsum('bqhr,bkr->bhqk', qr, kr, preferred_element_type=jnp.float32)) * ATTN_SCALE
    p = jax.nn.softmax(s, axis=-1).astype(v.dtype)
    return jnp.einsum('bhqk,bkhd->bqhd', p, v)


def mla_attend_blocked(qn, qr, kn, kr, v):
    b, n = qn.shape[:2]
    nb = n // Q_BLOCK

    def to_blocks(t):
        return jnp.moveaxis(t.reshape((b, nb, Q_BLOCK) + t.shape[2:]), 1, 0)

    out = lax.map(lambda qb: mla_attend(qb[0], qb[1], kn, kr, v), (to_blocks(qn), to_blocks(qr)))
    return jnp.moveaxis(out, 0, 1).reshape(b, n, N_HEADS * V_HEAD)


def s5_discretize(a_re, a_im, log_dt, b_re, b_im):
    dt = jnp.exp(log_dt.astype(jnp.float32))[..., None]
    ar = a_re.astype(jnp.float32)
    ai = a_im.astype(jnp.float32)
    mag = jnp.exp(ar * dt)
    lb_re = mag * jnp.cos(ai * dt)
    lb_im = mag * jnp.sin(ai * dt)
    nr = lb_re - 1.0
    den = ar * ar + ai * ai
    f_re = ((nr * ar + lb_im * ai) / den)[..., None]
    f_im = ((lb_im * ar - nr * ai) / den)[..., None]
    br = b_re.astype(jnp.float32)
    bi = b_im.astype(jnp.float32)
    return lb_re, lb_im, f_re * br - f_im * bi, f_re * bi + f_im * br


def linear_scan(a_re, a_im, b_re, b_im, s0, reverse):
    if s0 is not None:
        first = b_re.shape[1] - 1 if reverse else 0
        s_re, s_im = s0
        b_re = b_re.at[:, first].add(a_re * s_re - a_im * s_im)
        b_im = b_im.at[:, first].add(a_re * s_im + a_im * s_re)
    a_re_t = jnp.broadcast_to(a_re, b_re.shape)
    a_im_t = jnp.broadcast_to(a_im, b_re.shape)

    def combine(e1, e2):
        a1r, a1i, b1r, b1i = e1
        a2r, a2i, b2r, b2i = e2
        return (a2r * a1r - a2i * a1i, a2r * a1i + a2i * a1r,
                a2r * b1r - a2i * b1i + b2r, a2r * b1i + a2i * b1r + b2i)

    _, _, s_re, s_im = lax.associative_scan(combine, (a_re_t, a_im_t, b_re, b_im), reverse=reverse, axis=1)
    return s_re, s_im


def s5_direction(ux, uc, lr, li, br, bi, cr, ci, reverse, need_ctx):
    def drive(u):
        return jnp.einsum('bngh,gph->bngp', u, br), jnp.einsum('bngh,gph->bngp', u, bi)

    def readout(s_re, s_im):
        return jnp.einsum('ghp,bngp->bngh', cr, s_re) - jnp.einsum('ghp,bngp->bngh', ci, s_im)

    sc_re, sc_im = linear_scan(lr, li, *drive(uc), None, reverse)
    end = 0 if reverse else -1
    sx_re, sx_im = linear_scan(lr, li, *drive(ux), (sc_re[:, end], sc_im[:, end]), reverse)
    y_c = readout(sc_re, sc_im) if need_ctx else None
    return readout(sx_re, sx_im), y_c


def s5_mixer(u_x, u_c, need_ctx, a_re, a_im, log_dt, b_re, b_im, c_re, c_im, d_skip, w_glu, b_glu):
    dtype = u_x.dtype
    lb_re, lb_im, bb_re, bb_im = s5_discretize(a_re, a_im, log_dt, b_re, b_im)
    nc = S5_GROUPS // S5_CHUNK

    def chunk_param(t):
        return jnp.moveaxis(t.reshape((2, nc, S5_CHUNK) + t.shape[2:]), 1, 0).astype(dtype)

    def chunk_u(u):
        b, n, _ = u.shape
        return jnp.moveaxis(u.reshape(b, n, nc, S5_CHUNK, S5_GROUP), 2, 0)

    def unchunk(y):
        y = jnp.moveaxis(y, 0, 2)
        return y.reshape(y.shape[0], y.shape[1], S5_WIDTH)

    def run_chunk(args):
        ux, uc, lr, li, br, bi, cr, ci = args
        yx_f, yc_f = s5_direction(ux, uc, lr[0], li[0], br[0], bi[0], cr[0], ci[0], False, need_ctx)
        yx_b, yc_b = s5_direction(ux, uc, lr[1], li[1], br[1], bi[1], cr[1], ci[1], True, need_ctx)
        yc = yc_f + yc_b if need_ctx else None
        return yx_f + yx_b, yc

    params = tuple(chunk_param(t) for t in (lb_re, lb_im, bb_re, bb_im, c_re, c_im))
    yx, yc = lax.map(run_chunk, (chunk_u(u_x), chunk_u(u_c)) + params)

    def glu_out(y, u):
        g = jax.nn.gelu(y + d_skip * u)
        return g * jax.nn.sigmoid(g @ w_glu + b_glu)

    out_c = glu_out(unchunk(yc), u_c) if need_ctx else None
    return glu_out(unchunk(yx), u_x), out_c


def token_mixers(hx, hc, rope, need_ctx, w_in, b_gate, q_norm, w_uq, kv_norm, w_ukv, w_branch_mla,
                 s5_params, s5_d, w_glu, b_glu, w_branch_s5, w_out):
    b, n, _ = hx.shape
    zx = hx @ w_in
    cq_x, ckv_x, kr_x = zx[..., O_CQ:O_CKV], zx[..., O_CKV:O_KR], zx[..., O_KR:O_U]
    u_x, gm_x, gs_x = zx[..., O_U:O_G], zx[..., O_G:O_G + D_MODEL], zx[..., O_G + D_MODEL:]
    lo, hi = (0, IN_COLS) if need_ctx else (O_CKV, O_G)
    zc = hc @ w_in[:, lo:hi]

    def ccol(a, e):
        return zc[..., a - lo:e - lo]

    ckv_c, kr_c, u_c = ccol(O_CKV, O_KR), ccol(O_KR, O_U), ccol(O_U, O_G)

    kn_x, v_x = mla_kv(ckv_x, kv_norm, w_ukv)
    kn_c, v_c = mla_kv(ckv_c, kv_norm, w_ukv)
    kn_all = jnp.concatenate([kn_x, kn_c], axis=1)
    kr_all = jnp.concatenate([apply_axial_rope(kr_x, rope), kr_c], axis=1)
    v_all = jnp.concatenate([v_x, v_c], axis=1)
    qn_x, qr_x = mla_q(cq_x, q_norm, w_uq)
    o_x = mla_attend_blocked(qn_x, apply_axial_rope(qr_x, rope), kn_all, kr_all, v_all)

    y_x, y_c = s5_mixer(u_x, u_c, need_ctx, *s5_params, s5_d, w_glu, b_glu)

    def merge(o_mla, y_s5, gm, gs):
        g_mla = jax.nn.sigmoid(gm + b_gate[:D_MODEL])
        g_s5 = jax.nn.sigmoid(gs + b_gate[D_MODEL:])
        return (g_mla * (o_mla @ w_branch_mla) + g_s5 * (y_s5 @ w_branch_s5)) @ w_out

    out_x = merge(o_x, y_x, gm_x, gs_x)
    out_c = None
    if need_ctx:
        qn_c, qr_c = mla_q(ccol(O_CQ, O_CKV), q_norm, w_uq)
        o_c = mla_attend(qn_c, qr_c, kn_c, kr_c, v_c).reshape(b, hc.shape[1], N_HEADS * V_HEAD)
        out_c = merge(o_c, y_c, ccol(O_G, O_G + D_MODEL), ccol(O_G + D_MODEL, IN_COLS))
    return out_x, out_c


def swiglu(h, w1, w3, w2):
    return (jax.nn.silu(h @ w1) * (h @ w3)) @ w2


def moe_swiglu(h, w_router, b_router, w1, w3, w2):
    logits = jnp.einsum('bnd,de->bne', h, w_router, preferred_element_type=jnp.float32) + b_router
    top_v, top_i = lax.top_k(logits, TOP_K)
    top_w = jax.nn.softmax(top_v, axis=-1)
    combine = jnp.einsum('bnk,bnke->bne', top_w,
                         jax.nn.one_hot(top_i, N_EXPERTS, dtype=jnp.float32)).astype(h.dtype)
    y = combine[..., 0:1] * swiglu(h, w1[0], w3[0], w2[0])
    for e in range(1, N_EXPERTS):
        y = y + combine[..., e:e + 1] * swiglu(h, w1[e], w3[e], w2[e])
    return y


def setup_inputs(seed: int = 0) -> dict:
    key = jax.random.key(seed)
    ks = iter(jax.random.split(key, 64))
    f32 = jnp.float32
    D = D_MODEL
    G, P, H = S5_GROUPS, S5_STATE, S5_GROUP

    def nrm(shape, scale):
        return jax.random.normal(next(ks), shape, f32) * scale

    def gain(shape):
        return 1.0 + 0.02 * jax.random.normal(next(ks), shape, f32)

    n_idx = jnp.arange(P, dtype=f32)
    inp = {}
    inp['x'] = nrm((BATCH, SEQ, D), 1.0)
    inp['c'] = nrm((BATCH, D), 1.0)
    inp['ctx'] = nrm((BATCH, CTX_LEN, D), 1.0)
    inp['c_ctx'] = nrm((D,), 1.0)
    inp['w_mod'] = nrm((DEPTH, D, 6 * D), D ** -0.5)
    inp['b_mod'] = nrm((DEPTH, 6 * D), 0.02)
    inp['w_in'] = nrm((DEPTH, D, IN_COLS), D ** -0.5)
    inp['b_gate'] = nrm((DEPTH, N_BRANCH * D), 0.02)
    inp['q_norm'] = gain((DEPTH, Q_LORA))
    inp['w_uq'] = nrm((DEPTH, Q_LORA, N_HEADS * (QK_NOPE + QK_ROPE)), Q_LORA ** -0.5)
    inp['kv_norm'] = gain((DEPTH, KV_LORA))
    inp['w_ukv'] = nrm((DEPTH, KV_LORA, N_HEADS * (QK_NOPE + V_HEAD)), KV_LORA ** -0.5)
    inp['w_branch_mla'] = nrm((DEPTH, N_HEADS * V_HEAD, D), (N_HEADS * V_HEAD) ** -0.5)
    inp['s5_a_re'] = -0.5 * jnp.exp(nrm((DEPTH, 2, G, P), 0.05))
    inp['s5_a_im'] = math.pi * n_idx + nrm((DEPTH, 2, G, P), 0.02)
    inp['s5_log_dt'] = jax.random.uniform(next(ks), (DEPTH, 2, G), f32, math.log(DT_MIN), math.log(DT_MAX))
    inp['s5_b_re'] = nrm((DEPTH, 2, G, P, H), (2 * H) ** -0.5)
    inp['s5_b_im'] = nrm((DEPTH, 2, G, P, H), (2 * H) ** -0.5)
    inp['s5_c_re'] = nrm((DEPTH, 2, G, H, P), P ** -0.5)
    inp['s5_c_im'] = nrm((DEPTH, 2, G, H, P), P ** -0.5)
    inp['s5_d'] = nrm((DEPTH, S5_WIDTH), 1.0)
    inp['w_glu'] = nrm((DEPTH, S5_WIDTH, S5_WIDTH), S5_WIDTH ** -0.5)
    inp['b_glu'] = nrm((DEPTH, S5_WIDTH), 0.02)
    inp['w_branch_s5'] = nrm((DEPTH, S5_WIDTH, D), S5_WIDTH ** -0.5)
    inp['w_out'] = nrm((DEPTH, D, D), BETA * D ** -0.5)
    inp['ln_mix_g'] = gain((DEPTH, D))
    inp['ln_mix_b'] = nrm((DEPTH, D), 0.02)
    inp['ln_ffn_g'] = gain((DEPTH, D))
    inp['ln_ffn_b'] = nrm((DEPTH, D), 0.02)
    inp['ffn_w1'] = nrm((N_DENSE, D, D_FF), D ** -0.5)
    inp['ffn_w3'] = nrm((N_DENSE, D, D_FF), D ** -0.5)
    inp['ffn_w2'] = nrm((N_DENSE, D_FF, D), BETA * D_FF ** -0.5)
    inp['moe_w_router'] = nrm((N_MOE, D, N_EXPERTS), D ** -0.5)
    inp['moe_b_router'] = nrm((N_MOE, N_EXPERTS), 0.01)
    inp['moe_w1'] = nrm((N_MOE, N_EXPERTS, D, D_EXPERT), D ** -0.5)
    inp['moe_w3'] = nrm((N_MOE, N_EXPERTS, D, D_EXPERT), D ** -0.5)
    inp['moe_w2'] = nrm((N_MOE, N_EXPERTS, D_EXPERT, D), BETA * D_EXPERT ** -0.5)
    return inp


def reference(x, c, ctx, c_ctx, w_mod, b_mod, w_in, b_gate, q_norm, w_uq, kv_norm, w_ukv, w_branch_mla,
              s5_a_re, s5_a_im, s5_log_dt, s5_b_re, s5_b_im, s5_c_re, s5_c_im, s5_d, w_glu, b_glu,
              w_branch_s5, w_out, ln_mix_g, ln_mix_b, ln_ffn_g, ln_ffn_b, ffn_w1, ffn_w3, ffn_w2,
              moe_w_router, moe_b_router, moe_w1, moe_w3, moe_w2):
    rope = axial_rope_tables(x.shape[1])
    act_x = jax.nn.silu(c)
    act_c = jax.nn.silu(c_ctx)
    xc = ctx
    for l in range(DEPTH):
        need_ctx = l < DEPTH - 1
        mx = jnp.split((act_x @ w_mod[l] + b_mod[l])[:, None, :], 6, axis=-1)
        mc = jnp.split(act_c @ w_mod[l] + b_mod[l], 6, axis=-1)
        s5_params = (s5_a_re[l], s5_a_im[l], s5_log_dt[l], s5_b_re[l], s5_b_im[l], s5_c_re[l], s5_c_im[l])
        ox, oc = token_mixers(modulate(x, mx[0], mx[1]), modulate(xc, mc[0], mc[1]), rope, need_ctx,
                              w_in[l], b_gate[l], q_norm[l], w_uq[l], kv_norm[l], w_ukv[l], w_branch_mla[l],
                              s5_params, s5_d[l], w_glu[l], b_glu[l], w_branch_s5[l], w_out[l])
        x = ln_affine(ALPHA * x + mx[2] * ox, ln_mix_g[l], ln_mix_b[l])
        if need_ctx:
            xc = ln_affine(ALPHA * xc + mc[2] * oc, ln_mix_g[l], ln_mix_b[l])
        if l % 2 == 0:
            ffn = functools.partial(swiglu, w1=ffn_w1[l // 2], w3=ffn_w3[l // 2], w2=ffn_w2[l // 2])
        else:
            ffn = functools.partial(moe_swiglu, w_router=moe_w_router[l // 2], b_router=moe_b_router[l // 2],
                                    w1=moe_w1[l // 2], w3=moe_w3[l // 2], w2=moe_w2[l // 2])
        x = ln_affine(ALPHA * x + mx[5] * ffn(modulate(x, mx[3], mx[4])), ln_ffn_g[l], ln_ffn_b[l])
        if need_ctx:
            xc = ln_affine(ALPHA * xc + mc[5] * ffn(modulate(xc, mc[3], mc[4])), ln_ffn_g[l], ln_ffn_b[l])
    return x
```

```python
import functools
import math

import jax
import jax.numpy as jnp
from jax import lax
from jax.experimental import pallas as pl
from jax.experimental.pallas import tpu as pltpu

N_HEADS = 32
QK_NOPE = 128
QK_ROPE = 64
V_HEAD = 128
ROPE_THETA = 10000.0
GRID_W = 64
S5_GROUP = 16
S5_STATE = 64
TOP_K = 2
LN_EPS = 1e-6
RMS_EPS = 1e-6

HEAD_PAD = 256
S5_L = 16
LANE = 128
VMEM_LIMIT_BYTES = 56 * 2**20

F32 = jnp.float32
BF16 = jnp.bfloat16


def _cparams(sem):
    return pltpu.CompilerParams(dimension_semantics=sem, vmem_limit_bytes=VMEM_LIMIT_BYTES)


def _pick(n, prefs):
    for p in prefs:
        if n % p == 0:
            return p
    raise ValueError(f"no tile in {prefs} divides {n}")


def _mm_body(*refs, n_a, n_pc, n_b, kinds, prologue, epilogue, tn):
    a_refs = refs[:n_a]
    pc_refs = refs[n_a:n_a + n_pc]
    b_refs = refs[n_a + n_pc:n_a + n_pc + n_b]
    ex_refs = refs[n_a + n_pc + n_b:n_a + n_pc + n_b + len(kinds)]
    o_ref = refs[n_a + n_pc + n_b + len(kinds)]
    if prologue is not None:
        a_s = refs[n_a + n_pc + n_b + len(kinds) + 1]

        @pl.when(pl.program_id(1) == 0)
        def _():
            a_s[...] = prologue(*[r[...] for r in a_refs], *[r[...] for r in pc_refs]).astype(BF16)

        a = a_s[...]
    else:
        a = a_refs[0][...]
    accs = [jnp.dot(a, b[...].astype(BF16), preferred_element_type=F32) for b in b_refs]
    exs = []
    for r, kind in zip(ex_refs, kinds):
        v = r[...]
        if kind == "rowtab_tiled":
            v = jnp.tile(v, (1, tn // v.shape[1]))
        exs.append(v)
    o_ref[...] = epilogue(*accs, *exs).astype(o_ref.dtype)


def matmul(a_list, b_list, *, out_dtype, tm, tn, m_rows=None, epilogue=None, extras=(),
           prologue=None, pro_consts=(), name="mm"):
    K = a_list[0].shape[1]
    N = b_list[0].shape[1]
    M = a_list[0].shape[0] if m_rows is None else m_rows
    tn = _pick(N, tuple(t for t in (tn, 512, 256, 128) if t <= tn))
    assert M % tm == 0 and N % tn == 0, (M, tm, N, tn)
    if epilogue is None:
        epilogue = lambda acc: acc
    if prologue is None:
        assert len(a_list) == 1 and a_list[0].dtype == BF16
    in_specs = [pl.BlockSpec((tm, K), lambda i, j: (i, 0)) for _ in a_list]
    in_specs += [pl.BlockSpec(c.shape, lambda i, j: (0, 0)) for c in pro_consts]
    in_specs += [pl.BlockSpec((K, tn), lambda i, j: (0, j)) for _ in b_list]
    kinds = []
    ex_arrays = []
    for arr, kind in extras:
        kinds.append(kind)
        ex_arrays.append(arr)
        if kind == "tile":
            in_specs.append(pl.BlockSpec((tm, tn), lambda i, j: (i, j)))
        elif kind == "col":
            in_specs.append(pl.BlockSpec((1, tn), lambda i, j: (0, j)))
        elif kind in ("rowtab", "rowtab_tiled"):
            in_specs.append(pl.BlockSpec((tm, arr.shape[1]), lambda i, j: (i, 0)))
        else:
            raise ValueError(kind)
    scratch = [pltpu.VMEM((tm, K), BF16)] if prologue is not None else []
    body = functools.partial(_mm_body, n_a=len(a_list), n_pc=len(pro_consts), n_b=len(b_list),
                             kinds=tuple(kinds), prologue=prologue, epilogue=epilogue, tn=tn)
    return pl.pallas_call(
        body,
        grid=(M // tm, N // tn),
        in_specs=in_specs,
        out_specs=pl.BlockSpec((tm, tn), lambda i, j: (i, j)),
        out_shape=jax.ShapeDtypeStruct((M, N), out_dtype),
        scratch_shapes=scratch,
        compiler_params=_cparams(("parallel", "arbitrary")),
        name=name,
    )(*a_list, *pro_consts, *b_list, *ex_arrays)


def _mmk_body(a_ref, b_ref, *rest, n_ex, epilogue, nk):
    ex_refs = rest[:n_ex]
    o_ref = rest[n_ex]
    acc_ref = rest[n_ex + 1]
    k = pl.program_id(2)

    @pl.when(k == 0)
    def _():
        acc_ref[...] = jnp.zeros_like(acc_ref)

    acc_ref[...] += jnp.dot(a_ref[...], b_ref[...], preferred_element_type=F32)

    @pl.when(k == nk - 1)
    def _():
        o_ref[...] = epilogue(acc_ref[...], *[e[...] for e in ex_refs]).astype(o_ref.dtype)


def matmul_ksplit(a, b, *, out_dtype, tm, tn, tk, m_rows=None, epilogue=None, extras=(), name="mmk"):
    K = a.shape[1]
    N = b.shape[1]
    M = a.shape[0] if m_rows is None else m_rows
    tn = _pick(N, tuple(t for t in (tn, 512, 256, 128) if t <= tn))
    assert M % tm == 0 and N % tn == 0 and K % tk == 0, (M, tm, N, tn, K, tk)
    if epilogue is None:
        epilogue = lambda acc: acc
    in_specs = [pl.BlockSpec((tm, tk), lambda i, j, k: (i, k)),
                pl.BlockSpec((tk, tn), lambda i, j, k: (k, j))]
    ex_arrays = []
    for arr, kind in extras:
        ex_arrays.append(arr)
        if kind == "tile":
            in_specs.append(pl.BlockSpec((tm, tn), lambda i, j, k: (i, j)))
        elif kind == "rowtab":
            in_specs.append(pl.BlockSpec((tm, arr.shape[1]), lambda i, j, k: (i, 0)))
        else:
            raise ValueError(kind)
    nk = K // tk
    body = functools.partial(_mmk_body, n_ex=len(ex_arrays), epilogue=epilogue, nk=nk)
    return pl.pallas_call(
        body,
        grid=(M // tm, N // tn, nk),
        in_specs=in_specs,
        out_specs=pl.BlockSpec((tm, tn), lambda i, j, k: (i, j)),
        out_shape=jax.ShapeDtypeStruct((M, N), out_dtype),
        scratch_shapes=[pltpu.VMEM((tm, tn), F32)],
        compiler_params=_cparams(("parallel", "parallel", "arbitrary")),
        name=name,
    )(a, b, *ex_arrays)


def _mod_body(c_ref, w_ref, b_ref, o_ref):
    c = c_ref[...]
    act = (c * jax.nn.sigmoid(c)).astype(BF16)
    o_ref[...] = jnp.dot(act, w_ref[...].astype(BF16), preferred_element_type=F32) + b_ref[...]


def mod_vectors(cond, w_mod, b_mod):
    depth, d, n = w_mod.shape
    r = cond.shape[0]
    tn = _pick(n, (1024, 512, 256, 128))
    return pl.pallas_call(
        _mod_body,
        grid=(depth, n // tn),
        in_specs=[pl.BlockSpec((r, d), lambda l, j: (0, 0)),
                  pl.BlockSpec((None, d, tn), lambda l, j: (l, 0, j)),
                  pl.BlockSpec((None, 1, tn), lambda l, j: (l, 0, j))],
        out_specs=pl.BlockSpec((None, r, tn), lambda l, j: (l, 0, j)),
        out_shape=jax.ShapeDtypeStruct((depth, r, n), F32),
        compiler_params=_cparams(("parallel", "parallel")),
        name="mod_vectors",
    )(cond, w_mod, b_mod.reshape(depth, 1, n))


def _ln_rows(x):
    mu = jnp.mean(x, axis=-1, keepdims=True)
    xc = x - mu
    var = jnp.mean(xc * xc, axis=-1, keepdims=True)
    return xc * lax.rsqrt(var + LN_EPS)


def _modln_body(x_ref, sh_ref, sc_ref, h_ref):
    h_ref[...] = (_ln_rows(x_ref[...]) * (1.0 + sc_ref[...]) + sh_ref[...]).astype(h_ref.dtype)


def modulate_ln(x, shift, scale, *, rows_per_group, m_rows=None, tr=256):
    M = x.shape[0] if m_rows is None else m_rows
    d = x.shape[1]
    assert M % tr == 0 and rows_per_group % tr == 0
    gmap = lambda i: ((i * tr) // rows_per_group, 0, 0)
    return pl.pallas_call(
        _modln_body,
        grid=(M // tr,),
        in_specs=[pl.BlockSpec((tr, d), lambda i: (i, 0)),
                  pl.BlockSpec((None, 1, d), gmap),
                  pl.BlockSpec((None, 1, d), gmap)],
        out_specs=pl.BlockSpec((tr, d), lambda i: (i, 0)),
        out_shape=jax.ShapeDtypeStruct((M, d), BF16),
        compiler_params=_cparams(("parallel",)),
        name="modulate_ln",
    )(x, shift, scale)


def _resln_body(x_ref, y_ref, gate_ref, g_ref, b_ref, sh_ref, sc_ref, xo_ref, h_ref, *, alpha):
    xn = _ln_rows(alpha * x_ref[...] + gate_ref[...] * y_ref[...]) * g_ref[...] + b_ref[...]
    xo_ref[...] = xn
    h_ref[...] = (_ln_rows(xn) * (1.0 + sc_ref[...]) + sh_ref[...]).astype(h_ref.dtype)


def _resln_last_body(x_ref, y_ref, gate_ref, g_ref, b_ref, xo_ref, *, alpha):
    xo_ref[...] = _ln_rows(alpha * x_ref[...] + gate_ref[...] * y_ref[...]) * g_ref[...] + b_ref[...]


def residual_ln(x, y, gate, ln_g, ln_b, shift, scale, *, alpha, rows_per_group, m_rows=None, tr=256):
    M = x.shape[0] if m_rows is None else m_rows
    d = x.shape[1]
    assert M % tr == 0 and rows_per_group % tr == 0
    gmap = lambda i: ((i * tr) // rows_per_group, 0, 0)
    row = pl.BlockSpec((tr, d), lambda i: (i, 0))
    vec = pl.BlockSpec((1, d), lambda i: (0, 0))
    gvec = pl.BlockSpec((None, 1, d), gmap)
    if shift is None:
        return pl.pallas_call(
            functools.partial(_resln_last_body, alpha=alpha),
            grid=(M // tr,),
            in_specs=[row, row, gvec, vec, vec],
            out_specs=row,
            out_shape=jax.ShapeDtypeStruct((M, d), F32),
            compiler_params=_cparams(("parallel",)),
            name="residual_ln_last",
        )(x, y, gate, ln_g.reshape(1, d), ln_b.reshape(1, d)), None
    return pl.pallas_call(
        functools.partial(_resln_body, alpha=alpha),
        grid=(M // tr,),
        in_specs=[row, row, gvec, vec, vec, gvec, gvec],
        out_specs=[row, row],
        out_shape=[jax.ShapeDtypeStruct((M, d), F32), jax.ShapeDtypeStruct((M, d), BF16)],
        compiler_params=_cparams(("parallel",)),
        name="residual_ln",
    )(x, y, gate, ln_g.reshape(1, d), ln_b.reshape(1, d), shift, scale)


def _flash_body(q_ref, *refs, seg_lens, tk, scale):
    n_seg = len(seg_lens)
    kv_refs = refs[:2 * n_seg]
    o_ref = refs[2 * n_seg]
    m_ref, l_ref, acc_ref = refs[2 * n_seg + 1:]
    q = q_ref[...]
    m_ref[...] = jnp.full_like(m_ref, -jnp.inf)
    l_ref[...] = jnp.zeros_like(l_ref)
    acc_ref[...] = jnp.zeros_like(acc_ref)
    for s in range(n_seg):
        k_ref, v_ref = kv_refs[2 * s], kv_refs[2 * s + 1]
        tks = min(tk, seg_lens[s])

        def step(c, carry, k_ref=k_ref, v_ref=v_ref, tks=tks):
            start = pl.multiple_of(c * tks, tks)
            k = k_ref[pl.ds(start, tks), :]
            v = v_ref[pl.ds(start, tks), :]
            sc = lax.dot_general(q, k, (((1,), (1,)), ((), ())), preferred_element_type=F32) * scale
            m_old = m_ref[...]
            m_new = jnp.maximum(m_old, jnp.max(sc, axis=-1, keepdims=True))
            p = jnp.exp(sc - m_new)
            corr = jnp.exp(m_old - m_new)
            l_ref[...] = corr * l_ref[...] + jnp.sum(p, axis=-1, keepdims=True)
            acc_ref[...] = corr * acc_ref[...] + jnp.dot(p.astype(BF16), v, preferred_element_type=F32)
            m_ref[...] = m_new
            return carry

        lax.fori_loop(0, seg_lens[s] // tks, step, 0)
    o_ref[...] = (acc_ref[...] / l_ref[...]).astype(o_ref.dtype)


def flash_attention(q, k, v, *, n_batch, q_row0, q_len, segs, tq, tk, scale, name="flash"):
    h = N_HEADS
    nq = q_len // tq
    assert q_len % tq == 0 and q_row0 % tq == 0
    in_specs = [pl.BlockSpec((tq, HEAD_PAD), lambda b, hh, i: (q_row0 // tq + b * nq + i, hh))]
    args = [q]
    for row0, ln in segs:
        assert row0 % ln == 0
        in_specs.append(pl.BlockSpec((ln, HEAD_PAD), lambda b, hh, i, r=row0 // ln: (r + b, hh)))
        in_specs.append(pl.BlockSpec((ln, V_HEAD), lambda b, hh, i, r=row0 // ln: (r + b, hh)))
        args += [k, v]
    body = functools.partial(_flash_body, seg_lens=tuple(ln for _, ln in segs), tk=tk, scale=scale)
    return pl.pallas_call(
        body,
        grid=(n_batch, h, nq),
        in_specs=in_specs,
        out_specs=pl.BlockSpec((tq, V_HEAD), lambda b, hh, i: (b * nq + i, hh)),
        out_shape=jax.ShapeDtypeStruct((n_batch * q_len, h * V_HEAD), BF16),
        scratch_shapes=[pltpu.VMEM((tq, 1), F32), pltpu.VMEM((tq, 1), F32), pltpu.VMEM((tq, V_HEAD), F32)],
        compiler_params=_cparams(("parallel", "parallel", "arbitrary")),
        name=name,
    )(*args)


def _s5_drive_body(u_ref, w_ref, o_ref):
    for g in range(u_ref.shape[0]):
        o_ref[g] = jnp.dot(u_ref[g], w_ref[g], preferred_element_type=F32)


def _s5_out_body(u_ref, s_ref, t_ref, q_ref, o_ref):
    for g in range(u_ref.shape[0]):
        o_ref[g] = (jnp.dot(u_ref[g], t_ref[g], preferred_element_type=F32)
                    + jnp.dot(s_ref[g], q_ref[g], preferred_element_type=F32))


def _s5_scan_body(wre_ref, wim_ref, lr_ref, li_ref, sre_ref, sim_ref, st_re, st_im):
    @pl.when(pl.program_id(1) == 0)
    def _():
        st_re[...] = jnp.zeros_like(st_re)
        st_im[...] = jnp.zeros_like(st_im)

    lr = lr_ref[...]
    li = li_ref[...]

    def step(c, carry):
        s_re, s_im = carry
        sre_ref[c] = s_re
        sim_ref[c] = s_im
        n_re = lr * s_re - li * s_im + wre_ref[c]
        n_im = lr * s_im + li * s_re + wim_ref[c]
        return n_re, n_im

    s_re, s_im = lax.fori_loop(0, wre_ref.shape[0], step, (st_re[...], st_im[...]))
    st_re[...] = s_re
    st_im[...] = s_im


def s5_scan(w_re, w_im, lam_re, lam_im, *, cb):
    b, nc, g, p2 = w_re.shape
    assert nc % cb == 0
    blk = pl.BlockSpec((None, cb, g, p2), lambda bi, j: (bi, j, 0, 0))
    coef = pl.BlockSpec((g, p2), lambda bi, j: (0, 0))
    return pl.pallas_call(
        _s5_scan_body,
        grid=(b, nc // cb),
        in_specs=[blk, blk, coef, coef],
        out_specs=[blk, blk],
        out_shape=[jax.ShapeDtypeStruct(w_re.shape, F32)] * 2,
        scratch_shapes=[pltpu.VMEM((g, p2), F32), pltpu.VMEM((g, p2), F32)],
        compiler_params=_cparams(("parallel", "arbitrary")),
        name="s5_scan",
    )(w_re, w_im, lam_re, lam_im)


def _s5_tables(a_re, a_im, log_dt, b_re, b_im, c_re, c_im):
    L, P, Hh = S5_L, S5_STATE, S5_GROUP
    hp = lax.Precision.HIGHEST
    dt = jnp.exp(log_dt.astype(F32))[..., None]
    ar, ai = a_re.astype(F32), a_im.astype(F32)
    j = jnp.arange(L + 1, dtype=F32)[:, None, None, None]
    mag = jnp.exp(j * ar * dt)
    pr, pi = mag * jnp.cos(j * ai * dt), mag * jnp.sin(j * ai * dt)
    lr, li = pr[1], pi[1]
    nr = lr - 1.0
    den = ar * ar + ai * ai
    f_re = ((nr * ar + li * ai) / den)[..., None]
    f_im = ((li * ar - nr * ai) / den)[..., None]
    br, bi = b_re.astype(F32), b_im.astype(F32)
    bb_re = f_re * br - f_im * bi
    bb_im = f_re * bi + f_im * br
    cr, ci = c_re.astype(F32), c_im.astype(F32)

    zr = pr[:L, ..., None] * bb_re - pi[:L, ..., None] * bb_im
    zi = pr[:L, ..., None] * bb_im + pi[:L, ..., None] * bb_re
    kj = (jnp.einsum('dghp,jdgpk->jdghk', cr, zr, precision=hp)
          - jnp.einsum('dghp,jdgpk->jdghk', ci, zi, precision=hp))
    t_idx = jnp.arange(L)
    lag = t_idx[None, :] - t_idx[:, None]
    kf = kj[:, 0][jnp.clip(lag, 0, L - 1)]
    kb = kj[:, 1][jnp.clip(-lag, 0, L - 1)]
    tm4 = (jnp.where((lag >= 0)[:, :, None, None, None], kf, 0.0)
           + jnp.where((lag <= 0)[:, :, None, None, None], kb, 0.0))
    g = tm4.shape[2]
    tmat = jnp.transpose(tm4, (2, 0, 4, 1, 3)).reshape(g, L * Hh, L * Hh)

    kk = jnp.arange(L)
    pf_r, pf_i = pr[L - 1 - kk, 0], pi[L - 1 - kk, 0]
    pb_r, pb_i = pr[kk, 1], pi[kk, 1]
    wf_re = pf_r[..., None] * bb_re[0] - pf_i[..., None] * bb_im[0]
    wf_im = pf_r[..., None] * bb_im[0] + pf_i[..., None] * bb_re[0]
    wb_re = pb_r[..., None] * bb_re[1] - pb_i[..., None] * bb_im[1]
    wb_im = pb_r[..., None] * bb_im[1] + pb_i[..., None] * bb_re[1]
    wcat = jnp.concatenate([wf_re, wb_re, wf_im, wb_im], axis=2)
    wmat = jnp.transpose(wcat, (1, 0, 3, 2)).reshape(g, L * Hh, 4 * P)

    qf_r, qf_i = pr[kk + 1, 0], pi[kk + 1, 0]
    qb_r, qb_i = pr[L - kk, 1], pi[L - kk, 1]

    def qpair(c_r, c_i, q_r, q_i):
        return (c_r[None] * q_r[:, :, None, :] - c_i[None] * q_i[:, :, None, :],
                -c_r[None] * q_i[:, :, None, :] - c_i[None] * q_r[:, :, None, :])

    qf_re, qf_im = qpair(cr[0], ci[0], qf_r, qf_i)
    qb_re, qb_im = qpair(cr[1], ci[1], qb_r, qb_i)
    qcat = jnp.concatenate([qf_re, qb_re, qf_im, qb_im], axis=3)
    qmat = jnp.transpose(qcat, (1, 3, 0, 2)).reshape(g, 4 * P, L * Hh)

    lam_re = jnp.concatenate([pr[L, 0], pr[L, 1]], axis=-1)
    lam_im = jnp.concatenate([pi[L, 0], pi[L, 1]], axis=-1)
    return tmat.astype(BF16), wmat.astype(BF16), qmat.astype(BF16), lam_re, lam_im


def s5_mix(u, tables, *, n_batch, seq, ctx_len, gb=8):
    tmat, wmat, qmat, lam_re, lam_im = tables
    L, Hh, P = S5_L, S5_GROUP, S5_STATE
    w_tot = u.shape[1]
    g = w_tot // Hh
    ncx, ncc = seq // L, ctx_len // L
    nc = ncx + ncc
    nx = n_batch * seq
    ux = u[:nx].reshape(n_batch, ncx, L, g, Hh)
    uc = u[nx:].reshape(n_batch, ncc, L, g, Hh)
    uall = jnp.concatenate([uc, ux], axis=1)
    ug = jnp.transpose(uall, (3, 0, 1, 2, 4)).reshape(g, n_batch * nc, L * Hh).astype(BF16)
    rows = n_batch * nc
    gb = math.gcd(gb, g)
    grp = lambda r, c: pl.BlockSpec((gb, r, c), lambda i: (i, 0, 0))

    w = pl.pallas_call(
        _s5_drive_body,
        grid=(g // gb,),
        in_specs=[grp(rows, L * Hh), grp(L * Hh, 4 * P)],
        out_specs=grp(rows, 4 * P),
        out_shape=jax.ShapeDtypeStruct((g, rows, 4 * P), F32),
        compiler_params=_cparams(("parallel",)),
        name="s5_drive",
    )(ug, wmat)

    w5 = w.reshape(g, n_batch, nc, 4, P)
    w5 = jnp.transpose(w5, (1, 2, 0, 3, 4))
    brev = lambda a: jnp.concatenate([a[:, :ncc][:, ::-1], a[:, ncc:][:, ::-1]], axis=1)
    w_re = jnp.concatenate([w5[:, :, :, 0], brev(w5[:, :, :, 1])], axis=-1)
    w_im = jnp.concatenate([w5[:, :, :, 2], brev(w5[:, :, :, 3])], axis=-1)
    cb = _pick(nc, (48, 44, 33, 32, 24, 22, 16, 12, 11, 8, 6, 4, 3, 2, 1))
    s_re, s_im = s5_scan(w_re, w_im, lam_re, lam_im, cb=cb)
    s4 = jnp.stack([s_re[..., :P], brev(s_re[..., P:]), s_im[..., :P], brev(s_im[..., P:])], axis=3)
    sg = jnp.transpose(s4, (2, 0, 1, 3, 4)).reshape(g, rows, 4 * P).astype(BF16)

    y = pl.pallas_call(
        _s5_out_body,
        grid=(g // gb,),
        in_specs=[grp(rows, L * Hh), grp(rows, 4 * P), grp(L * Hh, L * Hh), grp(4 * P, L * Hh)],
        out_specs=grp(rows, L * Hh),
        out_shape=jax.ShapeDtypeStruct((g, rows, L * Hh), F32),
        compiler_params=_cparams(("parallel",)),
        name="s5_out",
    )(ug, sg, tmat, qmat)

    y5 = jnp.transpose(y.reshape(g, n_batch, nc, L, Hh), (1, 2, 3, 0, 4))
    yx = y5[:, ncc:].reshape(nx, w_tot)
    yc = y5[:, :ncc].reshape(n_batch * ctx_len, w_tot)
    return jnp.concatenate([yx, yc], axis=0)


def _router_body(lg_ref, b_ref, o_ref, *, n_exp):
    lg = lg_ref[...] + b_ref[...]
    lane = lax.broadcasted_iota(jnp.int32, lg.shape, 1).astype(F32)
    neg = jnp.float32(-jnp.inf)
    lg = jnp.where(lane < n_exp, lg, neg)
    m1 = jnp.max(lg, axis=-1, keepdims=True)
    i1 = jnp.min(jnp.where(lg == m1, lane, float(LANE)), axis=-1, keepdims=True)
    lg2 = jnp.where(lane == i1, neg, lg)
    m2 = jnp.max(lg2, axis=-1, keepdims=True)
    i2 = jnp.min(jnp.where(lg2 == m2, lane, float(LANE)), axis=-1, keepdims=True)
    e2 = jnp.exp(m2 - m1)
    den = 1.0 + e2
    o_ref[...] = jnp.where(lane == i1, 1.0 / den, 0.0) + jnp.where(lane == i2, e2 / den, 0.0)


def router_combine(logits, b_router_pad, *, n_exp, tr=512):
    m = logits.shape[0]
    tr = _pick(m, (tr, 256, 128, 64, 32, 16, 8))
    return pl.pallas_call(
        functools.partial(_router_body, n_exp=n_exp),
        grid=(m // tr,),
        in_specs=[pl.BlockSpec((tr, LANE), lambda i: (i, 0)), pl.BlockSpec((1, LANE), lambda i: (0, 0))],
        out_specs=pl.BlockSpec((tr, LANE), lambda i: (i, 0)),
        out_shape=jax.ShapeDtypeStruct((m, LANE), F32),
        compiler_params=_cparams(("parallel",)),
        name="router_top2",
    )(logits, b_router_pad)


def _rms_pro(x, gain):
    return x * lax.rsqrt(jnp.mean(x * x, axis=-1, keepdims=True) + RMS_EPS) * gain


def _gelu_tanh(x):
    return 0.5 * x * (1.0 + jnp.tanh(math.sqrt(2.0 / math.pi) * (x + 0.044715 * (x * x * x))))


def _rope_apply(x, cos, sin_up, sin_dn):
    n = x.shape[-1]
    return x * cos + pltpu.roll(x, n - QK_ROPE // 4, 1) * sin_up + pltpu.roll(x, QK_ROPE // 4, 1) * sin_dn


def _rope_tables(n_batch, seq, n_ctx_rows):
    nf = QK_ROPE // 4
    pos = jnp.arange(seq)
    row = (pos // GRID_W).astype(F32)
    col = (pos % GRID_W).astype(F32)
    inv = ROPE_THETA ** (-jnp.arange(nf, dtype=F32) / nf)
    ar, ac = row[:, None] * inv, col[:, None] * inv
    z = jnp.zeros((seq, nf), F32)
    cos64 = jnp.concatenate([jnp.cos(ar), jnp.cos(ar), jnp.cos(ac), jnp.cos(ac)], axis=1)
    up64 = jnp.concatenate([-jnp.sin(ar), z, -jnp.sin(ac), z], axis=1)
    dn64 = jnp.concatenate([z, jnp.sin(ar), z, jnp.sin(ac)], axis=1)

    def place(t64, fill):
        full = jnp.full((seq, HEAD_PAD), fill, F32).at[:, QK_NOPE:QK_NOPE + QK_ROPE].set(t64)
        full = jnp.tile(full, (n_batch, 1))
        return jnp.concatenate([full, jnp.full((n_ctx_rows, HEAD_PAD), fill, F32)], axis=0)

    return place(cos64, 1.0), place(up64, 0.0), place(dn64, 0.0)


def _pad_cols(w, n):
    return jnp.pad(w, ((0, 0), (0, n - w.shape[1])))


def _head_cat_cols(w_a, w_b, da, db):
    k = w_a.shape[0]
    parts = [w_a.reshape(k, N_HEADS, da)]
    if w_b is not None:
        parts.append(w_b.reshape(k, N_HEADS, db))
    used = da + (db if w_b is not None else 0)
    parts.append(jnp.zeros((k, N_HEADS, HEAD_PAD - used), w_a.dtype))
    return jnp.concatenate(parts, axis=2).reshape(k, N_HEADS * HEAD_PAD)


def kernel(x, c, ctx, c_ctx, w_mod, b_mod, w_in, b_gate, q_norm, w_uq, kv_norm, w_ukv, w_branch_mla,
           s5_a_re, s5_a_im, s5_log_dt, s5_b_re, s5_b_im, s5_c_re, s5_c_im, s5_d, w_glu, b_glu,
           w_branch_s5, w_out, ln_mix_g, ln_mix_b, ln_ffn_g, ln_ffn_b, ffn_w1, ffn_w3, ffn_w2,
           moe_w_router, moe_b_router, moe_w1, moe_w3, moe_w2):
    B, N, D = x.shape
    C = ctx.shape[1]
    depth = w_mod.shape[0]
    QL, KL = q_norm.shape[1], kv_norm.shape[1]
    SW = s5_d.shape[1]
    H = N_HEADS
    NX, NC_ROWS = B * N, B * C
    T = NX + NC_ROWS
    alpha = (2 * depth) ** 0.25
    scale = (QK_NOPE + QK_ROPE) ** -0.5
    o_ckv, o_kr, o_u, o_g = QL, QL + KL, QL + KL + QK_ROPE, QL + KL + QK_ROPE + SW
    assert N % C == 0 and N % 256 == 0 and NC_ROWS % 256 == 0

    tm_all = _pick(T, (1536, 1024, 768, 512, 384, 256, 128))
    tm_x = _pick(NX, (1024, 512, 256, 128))
    tile_n = lambda n: _pick(n, (512, 256, 128))

    n_cond = B + 1
    cond = jnp.concatenate([c, c_ctx[None], jnp.zeros((-n_cond % 8, D), F32)], axis=0)
    mods = mod_vectors(cond, w_mod, b_mod)
    mods = mods.reshape(depth, cond.shape[0], 6, D)

    def mvec(l, k):
        return mods[l, :n_cond, k][:, None, :]

    rope_cos, rope_up, rope_dn = _rope_tables(B, N, NC_ROWS)
    xt = jnp.concatenate([x.reshape(NX, D), ctx.reshape(NC_ROWS, D)], axis=0)
    h = modulate_ln(xt, mvec(0, 0), mvec(0, 1), rows_per_group=N)

    for l in range(depth):
        need_ctx = l < depth - 1
        rows = T if need_ctx else NX
        tm_r = tm_all if need_ctx else tm_x

        wi = w_in[l]
        w_cq = wi[:, :o_ckv].astype(BF16)
        w_ckv = wi[:, o_ckv:o_kr].astype(BF16)
        w_kr = jnp.concatenate(
            [jnp.zeros((D, QK_NOPE), F32), wi[:, o_kr:o_u],
             jnp.zeros((D, HEAD_PAD - QK_NOPE - QK_ROPE), F32)], axis=1).astype(BF16)
        w_u = wi[:, o_u:o_g].astype(BF16)
        w_gm = wi[:, o_g:o_g + D].astype(BF16)
        w_gs = wi[:, o_g + D:].astype(BF16)
        wq = w_uq[l].reshape(QL, H, QK_NOPE + QK_ROPE)
        w_q = _head_cat_cols(wq[:, :, :QK_NOPE].reshape(QL, -1), wq[:, :, QK_NOPE:].reshape(QL, -1),
                             QK_NOPE, QK_ROPE).astype(BF16)
        wkv = w_ukv[l].reshape(KL, H, QK_NOPE + V_HEAD)
        w_k = _head_cat_cols(wkv[:, :, :QK_NOPE].reshape(KL, -1), None, QK_NOPE, 0).astype(BF16)
        w_v = wkv[:, :, QK_NOPE:].reshape(KL, H * V_HEAD).astype(BF16)

        cq = matmul([h], [w_cq], out_dtype=F32, tm=tm_r, tn=tile_n(QL), m_rows=rows, name="in_cq")
        ckv = matmul([h], [w_ckv], out_dtype=F32, tm=tm_all, tn=tile_n(KL), name="in_ckv")
        krp = matmul([h], [w_kr], out_dtype=F32, tm=tm_all, tn=HEAD_PAD, name="in_kr")
        u = matmul([h], [w_u], out_dtype=F32, tm=tm_all, tn=tile_n(SW), name="in_s5")

        q_gain = q_norm[l].reshape(1, QL)
        kv_gain = kv_norm[l].reshape(1, KL)
        qh = matmul([cq], [w_q], out_dtype=BF16, tm=tm_r, tn=512, m_rows=rows,
                    prologue=_rms_pro, pro_consts=(q_gain,),
                    epilogue=_rope_apply,
                    extras=((rope_cos, "rowtab_tiled"), (rope_up, "rowtab_tiled"), (rope_dn, "rowtab_tiled")),
                    name="mla_q")
        kh = matmul([ckv], [w_k], out_dtype=BF16, tm=tm_all, tn=512,
                    prologue=_rms_pro, pro_consts=(kv_gain,),
                    epilogue=lambda acc, kr, cs, up, dn: acc + jnp.tile(_rope_apply(kr, cs, up, dn),
                                                                        (1, acc.shape[1] // HEAD_PAD)),
                    extras=((krp, "rowtab"), (rope_cos, "rowtab"), (rope_up, "rowtab"), (rope_dn, "rowtab")),
                    name="mla_k")
        vh = matmul([ckv], [w_v], out_dtype=BF16, tm=tm_all, tn=512,
                    prologue=_rms_pro, pro_consts=(kv_gain,), name="mla_v")
        tq = _pick(N, (512, 256, 128))
        o_x = flash_attention(qh, kh, vh, n_batch=B, q_row0=0, q_len=N, segs=[(0, N), (NX, C)],
                              tq=tq, tk=512, scale=scale, name="flash_x")
        if need_ctx:
            o_c = flash_attention(qh, kh, vh, n_batch=B, q_row0=NX, q_len=C, segs=[(NX, C)],
                                  tq=_pick(C, (256, 128)), tk=512, scale=scale, name="flash_ctx")
            o_all = jnp.concatenate([o_x, o_c], axis=0)
        else:
            o_all = o_x

        tables = _s5_tables(s5_a_re[l], s5_a_im[l], s5_log_dt[l], s5_b_re[l], s5_b_im[l],
                            s5_c_re[l], s5_c_im[l])
        y = s5_mix(u, tables, n_batch=B, seq=N, ctx_len=C)
        d_row = s5_d[l].reshape(1, SW)
        glu_pro = lambda yv, uv, dv: _gelu_tanh(yv + dv * uv)
        ys = matmul([y, u], [w_glu[l].astype(BF16)], out_dtype=BF16, tm=tm_r // 2, tn=tile_n(SW), m_rows=rows,
                    prologue=glu_pro, pro_consts=(d_row,),
                    epilogue=lambda acc, yv, uv, dv, bv: (lambda gg: gg * jax.nn.sigmoid(acc + bv))(
                        _gelu_tanh(yv + dv * uv)),
                    extras=((y, "tile"), (u, "tile"), (d_row, "col"), (b_glu[l].reshape(1, SW), "col")),
                    name="s5_glu")

        bg = b_gate[l]
        m1 = matmul([o_all], [w_branch_mla[l].astype(BF16)], out_dtype=F32, tm=tm_r, tn=512, m_rows=rows,
                    name="branch_mla")
        gm = matmul([h], [w_gm], out_dtype=F32, tm=tm_r, tn=512, m_rows=rows,
                    epilogue=lambda acc, bv, mv: jax.nn.sigmoid(acc + bv) * mv,
                    extras=((bg[:D].reshape(1, D), "col"), (m1, "tile")), name="gate_mla")
        m2 = matmul([ys], [w_branch_s5[l].astype(BF16)], out_dtype=F32, tm=tm_r, tn=512, m_rows=rows,
                    name="branch_s5")
        merged = matmul([h], [w_gs], out_dtype=BF16, tm=tm_r, tn=512, m_rows=rows,
                        epilogue=lambda acc, bv, mv, pv: jax.nn.sigmoid(acc + bv) * mv + pv,
                        extras=((bg[D:].reshape(1, D), "col"), (m2, "tile"), (gm, "tile")), name="gate_s5")
        mix = matmul([merged], [w_out[l].astype(BF16)], out_dtype=F32, tm=tm_r, tn=512, m_rows=rows,
                     name="out_proj")
        xt, h2 = residual_ln(xt, mix, mvec(l, 2), ln_mix_g[l], ln_mix_b[l], mvec(l, 3), mvec(l, 4),
                             alpha=alpha, rows_per_group=N, m_rows=rows)

        if l % 2 == 0:
            fi = l // 2
            dff = ffn_w1.shape[2]
            dff_p = -(-dff // 512) * 512
            w1 = _pad_cols(ffn_w1[fi], dff_p).astype(BF16)
            w3 = _pad_cols(ffn_w3[fi], dff_p).astype(BF16)
            w2 = jnp.pad(ffn_w2[fi], ((0, dff_p - dff), (0, 0))).astype(BF16)
            act = matmul([h2], [w1, w3], out_dtype=BF16, tm=tm_r, tn=256, m_rows=rows,
                         epilogue=lambda a, b: a * jax.nn.sigmoid(a) * b, name="ffn_up")
            tk = _pick(dff_p, (2816, 2048, 1024, 512))
            ff = matmul_ksplit(act, w2, out_dtype=F32, tm=tm_r, tn=512, tk=tk, m_rows=rows, name="ffn_down")
        else:
            mi = l // 2
            n_exp = moe_w_router.shape[2]
            w_r = _pad_cols(moe_w_router[mi], LANE).astype(BF16)
            b_r = jnp.pad(moe_b_router[mi], (0, LANE - n_exp)).reshape(1, LANE)
            logits = matmul([h2], [w_r], out_dtype=F32, tm=tm_r, tn=LANE, m_rows=rows, name="router_logits")
            comb = router_combine(logits, b_r, n_exp=n_exp)
            ff = None
            for e in range(n_exp):
                act = matmul([h2], [moe_w1[mi, e].astype(BF16), moe_w3[mi, e].astype(BF16)], out_dtype=BF16,
                             tm=tm_r, tn=256, m_rows=rows,
                             epilogue=lambda a, b: a * jax.nn.sigmoid(a) * b, name="moe_up")
                if ff is None:
                    ff = matmul([act], [moe_w2[mi, e].astype(BF16)], out_dtype=F32, tm=tm_r, tn=512, m_rows=rows,
                                epilogue=lambda acc, cw, e=e: cw[:, e:e + 1] * acc,
                                extras=((comb, "rowtab"),), name="moe_down")
                else:
                    ff = matmul([act], [moe_w2[mi, e].astype(BF16)], out_dtype=F32, tm=tm_r, tn=512, m_rows=rows,
                                epilogue=lambda acc, cw, prev, e=e: prev + cw[:, e:e + 1] * acc,
                                extras=((comb, "rowtab"), (ff, "tile")), name="moe_down")
        if need_ctx:
            xt, h = residual_ln(xt, ff, mvec(l, 5), ln_ffn_g[l], ln_ffn_b[l], mvec(l + 1, 0), mvec(l + 1, 1),
                                alpha=alpha, rows_per_group=N, m_rows=rows)
        else:
            xt, _ = residual_ln(xt, ff, mvec(l, 5), ln_ffn_g[l], ln_ffn_b[l], None, None,
                                alpha=alpha, rows_per_group=N, m_rows=rows)
    return xt[:NX].reshape(B, N, D)
```

```python
import functools
import math

import jax
import jax.numpy as jnp
from jax import lax
from jax.experimental import pallas as pl
from jax.experimental.pallas import tpu as pltpu

N_HEADS = 32
QK_NOPE = 128
QK_ROPE = 64
V_HEAD = 128
ROPE_THETA = 10000.0
GRID_W = 64
S5_GROUP = 16
S5_STATE = 64
TOP_K = 2
LN_EPS = 1e-6
RMS_EPS = 1e-6

HEAD_PAD = 256
S5_L = 16
FLASH_ONES = 16
LANE = 128
VMEM_LIMIT_BYTES = 56 * 2**20

F32 = jnp.float32
BF16 = jnp.bfloat16


def _cparams(sem):
    return pltpu.CompilerParams(dimension_semantics=sem, vmem_limit_bytes=VMEM_LIMIT_BYTES)


def _pick(n, prefs):
    for p in prefs:
        if n % p == 0:
            return p
    raise ValueError(f"no tile in {prefs} divides {n}")


def _mm_body(*refs, n_a, n_pc, n_b, kinds, prologue, epilogue, tm, tn, nt):
    a_refs = refs[:n_a]
    pc_refs = refs[n_a:n_a + n_pc]
    b_refs = refs[n_a + n_pc:n_a + n_pc + n_b]
    ex_refs = refs[n_a + n_pc + n_b:n_a + n_pc + n_b + len(kinds)]
    o_ref = refs[n_a + n_pc + n_b + len(kinds)]
    if prologue is not None:
        a_s = refs[n_a + n_pc + n_b + len(kinds) + 1]

        @pl.when(pl.program_id(1) == 0)
        def _():
            a_s[...] = prologue(*[r[...] for r in a_refs], *[r[...] for r in pc_refs]).astype(BF16)

        a = a_s[...]
    else:
        a = a_refs[0][...]
    if nt:
        accs = [lax.dot_general(a, b[...], (((1,), (1,)), ((), ())), preferred_element_type=F32)
                for b in b_refs]
    else:
        accs = [jnp.dot(a, b[...].astype(BF16), preferred_element_type=F32) for b in b_refs]
    exs = []
    for r, kind in zip(ex_refs, kinds):
        v = r[...]
        if kind == "rowtab_tiled":
            v = jnp.tile(v, (1, tn // v.shape[1]))
        elif kind == "coltab_tiled":
            v = jnp.tile(v, (tm // v.shape[0], 1))
        exs.append(v)
    o_ref[...] = epilogue(*accs, *exs).astype(o_ref.dtype)


def matmul(a_list, b_list, *, out_dtype, tm, tn, m_rows=None, n_cols=None, nt=False, epilogue=None,
           extras=(), prologue=None, pro_consts=(), name="mm"):
    K = a_list[0].shape[1]
    N = b_list[0].shape[0 if nt else 1] if n_cols is None else n_cols
    M = a_list[0].shape[0] if m_rows is None else m_rows
    tn = _pick(N, tuple(t for t in (tn, 512, 256, 128) if t <= tn))
    assert M % tm == 0 and N % tn == 0, (M, tm, N, tn)
    if epilogue is None:
        epilogue = lambda acc: acc
    if prologue is None:
        assert len(a_list) == 1 and a_list[0].dtype == BF16
    in_specs = [pl.BlockSpec((tm, K), lambda i, j: (i, 0)) for _ in a_list]
    in_specs += [pl.BlockSpec(c.shape, lambda i, j: (0, 0)) for c in pro_consts]
    if nt:
        in_specs += [pl.BlockSpec((tn, K), lambda i, j: (j, 0)) for _ in b_list]
    else:
        in_specs += [pl.BlockSpec((K, tn), lambda i, j: (0, j)) for _ in b_list]
    kinds = []
    ex_arrays = []
    for arr, kind in extras:
        kinds.append(kind)
        ex_arrays.append(arr)
        if kind == "tile":
            in_specs.append(pl.BlockSpec((tm, tn), lambda i, j: (i, j)))
        elif kind == "col":
            in_specs.append(pl.BlockSpec((1, tn), lambda i, j: (0, j)))
        elif kind in ("rowtab", "rowtab_tiled"):
            in_specs.append(pl.BlockSpec((tm, arr.shape[1]), lambda i, j: (i, 0)))
        elif kind == "coltab_tiled":
            in_specs.append(pl.BlockSpec((arr.shape[0], tn), lambda i, j: (0, j)))
        else:
            raise ValueError(kind)
    scratch = [pltpu.VMEM((tm, K), BF16)] if prologue is not None else []
    body = functools.partial(_mm_body, n_a=len(a_list), n_pc=len(pro_consts), n_b=len(b_list),
                             kinds=tuple(kinds), prologue=prologue, epilogue=epilogue, tm=tm, tn=tn, nt=nt)
    return pl.pallas_call(
        body,
        grid=(M // tm, N // tn),
        in_specs=in_specs,
        out_specs=pl.BlockSpec((tm, tn), lambda i, j: (i, j)),
        out_shape=jax.ShapeDtypeStruct((M, N), out_dtype),
        scratch_shapes=scratch,
        compiler_params=_cparams(("parallel", "arbitrary")),
        name=name,
    )(*a_list, *pro_consts, *b_list, *ex_arrays)


def _mmk_body(a_ref, b_ref, *rest, n_ex, epilogue, nk):
    ex_refs = rest[:n_ex]
    o_ref = rest[n_ex]
    acc_ref = rest[n_ex + 1]
    k = pl.program_id(2)

    @pl.when(k == 0)
    def _():
        acc_ref[...] = jnp.zeros_like(acc_ref)

    acc_ref[...] += jnp.dot(a_ref[...], b_ref[...], preferred_element_type=F32)

    @pl.when(k == nk - 1)
    def _():
        o_ref[...] = epilogue(acc_ref[...], *[e[...] for e in ex_refs]).astype(o_ref.dtype)


def matmul_ksplit(a, b, *, out_dtype, tm, tn, tk, m_rows=None, epilogue=None, extras=(), name="mmk"):
    K = a.shape[1]
    N = b.shape[1]
    M = a.shape[0] if m_rows is None else m_rows
    tn = _pick(N, tuple(t for t in (tn, 512, 256, 128) if t <= tn))
    assert M % tm == 0 and N % tn == 0 and K % tk == 0, (M, tm, N, tn, K, tk)
    if epilogue is None:
        epilogue = lambda acc: acc
    in_specs = [pl.BlockSpec((tm, tk), lambda i, j, k: (i, k)),
                pl.BlockSpec((tk, tn), lambda i, j, k: (k, j))]
    ex_arrays = []
    for arr, kind in extras:
        ex_arrays.append(arr)
        if kind == "tile":
            in_specs.append(pl.BlockSpec((tm, tn), lambda i, j, k: (i, j)))
        elif kind == "rowtab":
            in_specs.append(pl.BlockSpec((tm, arr.shape[1]), lambda i, j, k: (i, 0)))
        else:
            raise ValueError(kind)
    nk = K // tk
    body = functools.partial(_mmk_body, n_ex=len(ex_arrays), epilogue=epilogue, nk=nk)
    return pl.pallas_call(
        body,
        grid=(M // tm, N // tn, nk),
        in_specs=in_specs,
        out_specs=pl.BlockSpec((tm, tn), lambda i, j, k: (i, j)),
        out_shape=jax.ShapeDtypeStruct((M, N), out_dtype),
        scratch_shapes=[pltpu.VMEM((tm, tn), F32)],
        compiler_params=_cparams(("parallel", "parallel", "arbitrary")),
        name=name,
    )(a, b, *ex_arrays)


def _mod_body(c_ref, w_ref, b_ref, o_ref):
    c = c_ref[...]
    act = (c * jax.nn.sigmoid(c)).astype(BF16)
    o_ref[...] = jnp.dot(act, w_ref[...].astype(BF16), preferred_element_type=F32) + b_ref[...]


def mod_vectors(cond, w_mod, b_mod):
    depth, d, n = w_mod.shape
    r = cond.shape[0]
    tn = _pick(n, (1024, 512, 256, 128))
    return pl.pallas_call(
        _mod_body,
        grid=(depth, n // tn),
        in_specs=[pl.BlockSpec((r, d), lambda l, j: (0, 0)),
                  pl.BlockSpec((None, d, tn), lambda l, j: (l, 0, j)),
                  pl.BlockSpec((None, 1, tn), lambda l, j: (l, 0, j))],
        out_specs=pl.BlockSpec((None, r, tn), lambda l, j: (l, 0, j)),
        out_shape=jax.ShapeDtypeStruct((depth, r, n), F32),
        compiler_params=_cparams(("parallel", "parallel")),
        name="mod_vectors",
    )(cond, w_mod, b_mod.reshape(depth, 1, n))


def _ln_rows(x):
    mu = jnp.mean(x, axis=-1, keepdims=True)
    xc = x - mu
    var = jnp.mean(xc * xc, axis=-1, keepdims=True)
    return xc * lax.rsqrt(var + LN_EPS)


def _modln_body(x_ref, sh_ref, sc_ref, h_ref):
    h_ref[...] = (_ln_rows(x_ref[...]) * (1.0 + sc_ref[...]) + sh_ref[...]).astype(h_ref.dtype)


def modulate_ln(x, shift, scale, *, rows_per_group, m_rows=None, tr=256):
    M = x.shape[0] if m_rows is None else m_rows
    d = x.shape[1]
    assert M % tr == 0 and rows_per_group % tr == 0
    gmap = lambda i: ((i * tr) // rows_per_group, 0, 0)
    return pl.pallas_call(
        _modln_body,
        grid=(M // tr,),
        in_specs=[pl.BlockSpec((tr, d), lambda i: (i, 0)),
                  pl.BlockSpec((None, 1, d), gmap),
                  pl.BlockSpec((None, 1, d), gmap)],
        out_specs=pl.BlockSpec((tr, d), lambda i: (i, 0)),
        out_shape=jax.ShapeDtypeStruct((M, d), BF16),
        compiler_params=_cparams(("parallel",)),
        name="modulate_ln",
    )(x, shift, scale)


def _resln_body(x_ref, y_ref, gate_ref, g_ref, b_ref, sh_ref, sc_ref, xo_ref, h_ref, *, alpha):
    xn = _ln_rows(alpha * x_ref[...] + gate_ref[...] * y_ref[...]) * g_ref[...] + b_ref[...]
    xo_ref[...] = xn
    h_ref[...] = (_ln_rows(xn) * (1.0 + sc_ref[...]) + sh_ref[...]).astype(h_ref.dtype)


def _resln_last_body(x_ref, y_ref, gate_ref, g_ref, b_ref, xo_ref, *, alpha):
    xo_ref[...] = _ln_rows(alpha * x_ref[...] + gate_ref[...] * y_ref[...]) * g_ref[...] + b_ref[...]


def residual_ln(x, y, gate, ln_g, ln_b, shift, scale, *, alpha, rows_per_group, m_rows=None, tr=256):
    M = x.shape[0] if m_rows is None else m_rows
    d = x.shape[1]
    assert M % tr == 0 and rows_per_group % tr == 0
    gmap = lambda i: ((i * tr) // rows_per_group, 0, 0)
    row = pl.BlockSpec((tr, d), lambda i: (i, 0))
    vec = pl.BlockSpec((1, d), lambda i: (0, 0))
    gvec = pl.BlockSpec((None, 1, d), gmap)
    if shift is None:
        return pl.pallas_call(
            functools.partial(_resln_last_body, alpha=alpha),
            grid=(M // tr,),
            in_specs=[row, row, gvec, vec, vec],
            out_specs=row,
            out_shape=jax.ShapeDtypeStruct((M, d), F32),
            compiler_params=_cparams(("parallel",)),
            name="residual_ln_last",
        )(x, y, gate, ln_g.reshape(1, d), ln_b.reshape(1, d)), None
    return pl.pallas_call(
        functools.partial(_resln_body, alpha=alpha),
        grid=(M // tr,),
        in_specs=[row, row, gvec, vec, vec, gvec, gvec],
        out_specs=[row, row],
        out_shape=[jax.ShapeDtypeStruct((M, d), F32), jax.ShapeDtypeStruct((M, d), BF16)],
        compiler_params=_cparams(("parallel",)),
        name="residual_ln",
    )(x, y, gate, ln_g.reshape(1, d), ln_b.reshape(1, d), shift, scale)


def _flash_body(qt_ref, *refs, seg_lens, tk):
    n_seg = len(seg_lens)
    kv_refs = refs[:2 * n_seg]
    o_ref = refs[2 * n_seg]
    acc_ref, st_a, st_b = refs[2 * n_seg + 1:]
    bufs = (st_a, st_b)
    qt = qt_ref[...]
    tq = qt.shape[1]
    acc_ref[...] = jnp.zeros_like(acc_ref)
    m = jnp.full((1, tq), -jnp.inf, F32)

    def scores(seg, c, tks):
        k = kv_refs[2 * seg][pl.ds(pl.multiple_of(c * tks, tks), tks), :]
        return jnp.dot(k, qt, preferred_element_type=F32)

    def absorb(st, seg, c, tks, m_old):
        vt = kv_refs[2 * seg + 1][:, pl.ds(pl.multiple_of(c * tks, tks), tks)]
        vt1 = jnp.concatenate([vt, jnp.ones((FLASH_ONES, tks), BF16)], axis=0)
        m_new = jnp.maximum(m_old, jnp.max(st, axis=0, keepdims=True))
        p = jnp.exp2(st - m_new).astype(BF16)
        corr = jnp.exp2(m_old - m_new)
        acc_ref[...] = corr * acc_ref[...] + jnp.dot(vt1, p, preferred_element_type=F32)
        return m_new

    def run_static(chunks, cur, m):
        for i, (seg, c, tks) in enumerate(chunks):
            if i + 1 < len(chunks):
                nseg, nc, ntks = chunks[i + 1]
                bufs[1 - cur][:ntks] = scores(nseg, nc, ntks)
            m = absorb(bufs[cur][:tks], seg, c, tks, m)
            cur = 1 - cur
        return m

    chunk_counts = [(s, min(tk, ln), ln // min(tk, ln)) for s, ln in enumerate(seg_lens)]
    s0, tk0, n0 = chunk_counts[0]
    rest = [(s, c, tks) for s, tks, n in chunk_counts[1:] for c in range(n)]
    if n0 >= 4 and n0 % 2 == 0:
        st_a[...] = scores(s0, 0, tk0)

        def pair(j, m):
            c0 = 2 * j
            st_b[...] = scores(s0, c0 + 1, tk0)
            m = absorb(st_a[...], s0, c0, tk0, m)
            st_a[...] = scores(s0, c0 + 2, tk0)
            return absorb(st_b[...], s0, c0 + 1, tk0, m)

        m = lax.fori_loop(0, n0 // 2 - 1, pair, m)
        m = run_static([(s0, n0 - 2, tk0), (s0, n0 - 1, tk0)] + rest, 0, m)
    else:
        chunks = [(s0, c, tk0) for c in range(n0)] + rest
        st_a[:tk0] = scores(s0, 0, tk0)
        m = run_static(chunks, 0, m)
    acc = acc_ref[...]
    o_ref[...] = (acc[:V_HEAD] / acc[V_HEAD:V_HEAD + 1]).T.astype(o_ref.dtype)


def flash_attention(qt, k, vt, *, n_batch, q_row0, q_len, segs, tq, tk, name="flash"):
    h = N_HEADS
    nq = q_len // tq
    assert q_len % tq == 0 and q_row0 % tq == 0
    in_specs = [pl.BlockSpec((HEAD_PAD, tq), lambda b, hh, i: (hh, q_row0 // tq + b * nq + i))]
    args = [qt]
    for row0, ln in segs:
        assert row0 % ln == 0
        in_specs.append(pl.BlockSpec((ln, HEAD_PAD), lambda b, hh, i, r=row0 // ln: (r + b, hh)))
        in_specs.append(pl.BlockSpec((V_HEAD, ln), lambda b, hh, i, r=row0 // ln: (hh, r + b)))
        args += [k, vt]
    body = functools.partial(_flash_body, seg_lens=tuple(ln for _, ln in segs), tk=tk)
    return pl.pallas_call(
        body,
        grid=(n_batch, h, nq),
        in_specs=in_specs,
        out_specs=pl.BlockSpec((tq, V_HEAD), lambda b, hh, i: (b * nq + i, hh)),
        out_shape=jax.ShapeDtypeStruct((n_batch * q_len, h * V_HEAD), BF16),
        scratch_shapes=[pltpu.VMEM((V_HEAD + FLASH_ONES, tq), F32),
                        pltpu.VMEM((tk, tq), F32), pltpu.VMEM((tk, tq), F32)],
        compiler_params=_cparams(("parallel", "parallel", "arbitrary")),
        name=name,
    )(*args)


def _s5_drive_body(u_ref, w_ref, o_ref):
    for g in range(u_ref.shape[0]):
        o_ref[g] = jnp.dot(u_ref[g], w_ref[g], preferred_element_type=F32)


def _s5_out_body(u_ref, s_ref, t_ref, q_ref, o_ref):
    for g in range(u_ref.shape[0]):
        o_ref[g] = (jnp.dot(u_ref[g], t_ref[g], preferred_element_type=F32)
                    + jnp.dot(s_ref[g], q_ref[g], preferred_element_type=F32))


def _s5_scan_body(wre_ref, wim_ref, lr_ref, li_ref, sre_ref, sim_ref, st_re, st_im):
    @pl.when(pl.program_id(1) == 0)
    def _():
        st_re[...] = jnp.zeros_like(st_re)
        st_im[...] = jnp.zeros_like(st_im)

    lr = lr_ref[...]
    li = li_ref[...]

    def step(c, carry):
        s_re, s_im = carry
        sre_ref[c] = s_re
        sim_ref[c] = s_im
        n_re = lr * s_re - li * s_im + wre_ref[c]
        n_im = lr * s_im + li * s_re + wim_ref[c]
        return n_re, n_im

    s_re, s_im = lax.fori_loop(0, wre_ref.shape[0], step, (st_re[...], st_im[...]))
    st_re[...] = s_re
    st_im[...] = s_im


def s5_scan(w_re, w_im, lam_re, lam_im, *, cb):
    b, nc, g, p2 = w_re.shape
    assert nc % cb == 0
    blk = pl.BlockSpec((None, cb, g, p2), lambda bi, j: (bi, j, 0, 0))
    coef = pl.BlockSpec((g, p2), lambda bi, j: (0, 0))
    return pl.pallas_call(
        _s5_scan_body,
        grid=(b, nc // cb),
        in_specs=[blk, blk, coef, coef],
        out_specs=[blk, blk],
        out_shape=[jax.ShapeDtypeStruct(w_re.shape, F32)] * 2,
        scratch_shapes=[pltpu.VMEM((g, p2), F32), pltpu.VMEM((g, p2), F32)],
        compiler_params=_cparams(("parallel", "arbitrary")),
        name="s5_scan",
    )(w_re, w_im, lam_re, lam_im)


def _s5_tables(a_re, a_im, log_dt, b_re, b_im, c_re, c_im):
    L, P, Hh = S5_L, S5_STATE, S5_GROUP
    hp = lax.Precision.HIGHEST
    dt = jnp.exp(log_dt.astype(F32))[..., None]
    ar, ai = a_re.astype(F32), a_im.astype(F32)
    j = jnp.arange(L + 1, dtype=F32)[:, None, None, None]
    mag = jnp.exp(j * ar * dt)
    pr, pi = mag * jnp.cos(j * ai * dt), mag * jnp.sin(j * ai * dt)
    lr, li = pr[1], pi[1]
    nr = lr - 1.0
    den = ar * ar + ai * ai
    f_re = ((nr * ar + li * ai) / den)[..., None]
    f_im = ((li * ar - nr * ai) / den)[..., None]
    br, bi = b_re.astype(F32), b_im.astype(F32)
    bb_re = f_re * br - f_im * bi
    bb_im = f_re * bi + f_im * br
    cr, ci = c_re.astype(F32), c_im.astype(F32)

    zr = pr[:L, ..., None] * bb_re - pi[:L, ..., None] * bb_im
    zi = pr[:L, ..., None] * bb_im + pi[:L, ..., None] * bb_re
    kj = (jnp.einsum('dghp,jdgpk->jdghk', cr, zr, precision=hp)
          - jnp.einsum('dghp,jdgpk->jdghk', ci, zi, precision=hp))
    t_idx = jnp.arange(L)
    lag = t_idx[None, :] - t_idx[:, None]
    kf = kj[:, 0][jnp.clip(lag, 0, L - 1)]
    kb = kj[:, 1][jnp.clip(-lag, 0, L - 1)]
    tm4 = (jnp.where((lag >= 0)[:, :, None, None, None], kf, 0.0)
           + jnp.where((lag <= 0)[:, :, None, None, None], kb, 0.0))
    g = tm4.shape[2]
    tmat = jnp.transpose(tm4, (2, 0, 4, 1, 3)).reshape(g, L * Hh, L * Hh)

    kk = jnp.arange(L)
    pf_r, pf_i = pr[L - 1 - kk, 0], pi[L - 1 - kk, 0]
    pb_r, pb_i = pr[kk, 1], pi[kk, 1]
    wf_re = pf_r[..., None] * bb_re[0] - pf_i[..., None] * bb_im[0]
    wf_im = pf_r[..., None] * bb_im[0] + pf_i[..., None] * bb_re[0]
    wb_re = pb_r[..., None] * bb_re[1] - pb_i[..., None] * bb_im[1]
    wb_im = pb_r[..., None] * bb_im[1] + pb_i[..., None] * bb_re[1]
    wcat = jnp.concatenate([wf_re, wb_re, wf_im, wb_im], axis=2)
    wmat = jnp.transpose(wcat, (1, 0, 3, 2)).reshape(g, L * Hh, 4 * P)

    qf_r, qf_i = pr[kk + 1, 0], pi[kk + 1, 0]
    qb_r, qb_i = pr[L - kk, 1], pi[L - kk, 1]

    def qpair(c_r, c_i, q_r, q_i):
        return (c_r[None] * q_r[:, :, None, :] - c_i[None] * q_i[:, :, None, :],
                -c_r[None] * q_i[:, :, None, :] - c_i[None] * q_r[:, :, None, :])

    qf_re, qf_im = qpair(cr[0], ci[0], qf_r, qf_i)
    qb_re, qb_im = qpair(cr[1], ci[1], qb_r, qb_i)
    qcat = jnp.concatenate([qf_re, qb_re, qf_im, qb_im], axis=3)
    qmat = jnp.transpose(qcat, (1, 3, 0, 2)).reshape(g, 4 * P, L * Hh)

    lam_re = jnp.concatenate([pr[L, 0], pr[L, 1]], axis=-1)
    lam_im = jnp.concatenate([pi[L, 0], pi[L, 1]], axis=-1)
    return tmat.astype(BF16), wmat.astype(BF16), qmat.astype(BF16), lam_re, lam_im


def s5_mix(u, tables, *, n_batch, seq, ctx_len, gb=8):
    tmat, wmat, qmat, lam_re, lam_im = tables
    L, Hh, P = S5_L, S5_GROUP, S5_STATE
    w_tot = u.shape[1]
    g = w_tot // Hh
    ncx, ncc = seq // L, ctx_len // L
    nc = ncx + ncc
    nx = n_batch * seq
    ux = u[:nx].reshape(n_batch, ncx, L, g, Hh)
    uc = u[nx:].reshape(n_batch, ncc, L, g, Hh)
    uall = jnp.concatenate([uc, ux], axis=1)
    ug = jnp.transpose(uall, (3, 0, 1, 2, 4)).reshape(g, n_batch * nc, L * Hh).astype(BF16)
    rows = n_batch * nc
    gb = math.gcd(gb, g)
    grp = lambda r, c: pl.BlockSpec((gb, r, c), lambda i: (i, 0, 0))

    w = pl.pallas_call(
        _s5_drive_body,
        grid=(g // gb,),
        in_specs=[grp(rows, L * Hh), grp(L * Hh, 4 * P)],
        out_specs=grp(rows, 4 * P),
        out_shape=jax.ShapeDtypeStruct((g, rows, 4 * P), F32),
        compiler_params=_cparams(("parallel",)),
        name="s5_drive",
    )(ug, wmat)

    w5 = w.reshape(g, n_batch, nc, 4, P)
    w5 = jnp.transpose(w5, (1, 2, 0, 3, 4))
    brev = lambda a: jnp.concatenate([a[:, :ncc][:, ::-1], a[:, ncc:][:, ::-1]], axis=1)
    w_re = jnp.concatenate([w5[:, :, :, 0], brev(w5[:, :, :, 1])], axis=-1)
    w_im = jnp.concatenate([w5[:, :, :, 2], brev(w5[:, :, :, 3])], axis=-1)
    cb = _pick(nc, (48, 44, 33, 32, 24, 22, 16, 12, 11, 8, 6, 4, 3, 2, 1))
    s_re, s_im = s5_scan(w_re, w_im, lam_re, lam_im, cb=cb)
    s4 = jnp.stack([s_re[..., :P], brev(s_re[..., P:]), s_im[..., :P], brev(s_im[..., P:])], axis=3)
    sg = jnp.transpose(s4, (2, 0, 1, 3, 4)).reshape(g, rows, 4 * P).astype(BF16)

    y = pl.pallas_call(
        _s5_out_body,
        grid=(g // gb,),
        in_specs=[grp(rows, L * Hh), grp(rows, 4 * P), grp(L * Hh, L * Hh), grp(4 * P, L * Hh)],
        out_specs=grp(rows, L * Hh),
        out_shape=jax.ShapeDtypeStruct((g, rows, L * Hh), F32),
        compiler_params=_cparams(("parallel",)),
        name="s5_out",
    )(ug, sg, tmat, qmat)

    y5 = jnp.transpose(y.reshape(g, n_batch, nc, L, Hh), (1, 2, 3, 0, 4))
    yx = y5[:, ncc:].reshape(nx, w_tot)
    yc = y5[:, :ncc].reshape(n_batch * ctx_len, w_tot)
    return jnp.concatenate([yx, yc], axis=0)


def _router_body(lg_ref, b_ref, o_ref, *, n_exp):
    lg = lg_ref[...] + b_ref[...]
    lane = lax.broadcasted_iota(jnp.int32, lg.shape, 1).astype(F32)
    neg = jnp.float32(-jnp.inf)
    lg = jnp.where(lane < n_exp, lg, neg)
    m1 = jnp.max(lg, axis=-1, keepdims=True)
    i1 = jnp.min(jnp.where(lg == m1, lane, float(LANE)), axis=-1, keepdims=True)
    lg2 = jnp.where(lane == i1, neg, lg)
    m2 = jnp.max(lg2, axis=-1, keepdims=True)
    i2 = jnp.min(jnp.where(lg2 == m2, lane, float(LANE)), axis=-1, keepdims=True)
    e2 = jnp.exp(m2 - m1)
    den = 1.0 + e2
    o_ref[...] = jnp.where(lane == i1, 1.0 / den, 0.0) + jnp.where(lane == i2, e2 / den, 0.0)


def router_combine(logits, b_router_pad, *, n_exp, tr=512):
    m = logits.shape[0]
    tr = _pick(m, (tr, 256, 128, 64, 32, 16, 8))
    return pl.pallas_call(
        functools.partial(_router_body, n_exp=n_exp),
        grid=(m // tr,),
        in_specs=[pl.BlockSpec((tr, LANE), lambda i: (i, 0)), pl.BlockSpec((1, LANE), lambda i: (0, 0))],
        out_specs=pl.BlockSpec((tr, LANE), lambda i: (i, 0)),
        out_shape=jax.ShapeDtypeStruct((m, LANE), F32),
        compiler_params=_cparams(("parallel",)),
        name="router_top2",
    )(logits, b_router_pad)


def _rms_pro(x, gain):
    return x * lax.rsqrt(jnp.mean(x * x, axis=-1, keepdims=True) + RMS_EPS) * gain


def _gelu_tanh(x):
    return 0.5 * x * (1.0 + jnp.tanh(math.sqrt(2.0 / math.pi) * (x + 0.044715 * (x * x * x))))


def _rope_apply(x, cos, sin_up, sin_dn):
    n = x.shape[-1]
    return x * cos + pltpu.roll(x, n - QK_ROPE // 4, 1) * sin_up + pltpu.roll(x, QK_ROPE // 4, 1) * sin_dn


def _rope_apply_t(x, cos, sin_up, sin_dn):
    n = x.shape[0]
    return x * cos + pltpu.roll(x, n - QK_ROPE // 4, 0) * sin_up + pltpu.roll(x, QK_ROPE // 4, 0) * sin_dn


def _rope_tables(n_batch, seq, n_ctx_rows):
    nf = QK_ROPE // 4
    pos = jnp.arange(seq)
    row = (pos // GRID_W).astype(F32)
    col = (pos % GRID_W).astype(F32)
    inv = ROPE_THETA ** (-jnp.arange(nf, dtype=F32) / nf)
    ar, ac = row[:, None] * inv, col[:, None] * inv
    z = jnp.zeros((seq, nf), F32)
    cos64 = jnp.concatenate([jnp.cos(ar), jnp.cos(ar), jnp.cos(ac), jnp.cos(ac)], axis=1)
    up64 = jnp.concatenate([-jnp.sin(ar), z, -jnp.sin(ac), z], axis=1)
    dn64 = jnp.concatenate([z, jnp.sin(ar), z, jnp.sin(ac)], axis=1)

    def place(t64, fill):
        full = jnp.full((seq, HEAD_PAD), fill, F32).at[:, QK_NOPE:QK_NOPE + QK_ROPE].set(t64)
        full = jnp.tile(full, (n_batch, 1))
        return jnp.concatenate([full, jnp.full((n_ctx_rows, HEAD_PAD), fill, F32)], axis=0)

    return place(cos64, 1.0), place(up64, 0.0), place(dn64, 0.0)


def _pad_cols(w, n):
    return jnp.pad(w, ((0, 0), (0, n - w.shape[1])))


def _head_cat_cols(w_a, w_b, da, db):
    k = w_a.shape[0]
    parts = [w_a.reshape(k, N_HEADS, da)]
    if w_b is not None:
        parts.append(w_b.reshape(k, N_HEADS, db))
    used = da + (db if w_b is not None else 0)
    parts.append(jnp.zeros((k, N_HEADS, HEAD_PAD - used), w_a.dtype))
    return jnp.concatenate(parts, axis=2).reshape(k, N_HEADS * HEAD_PAD)


def kernel(x, c, ctx, c_ctx, w_mod, b_mod, w_in, b_gate, q_norm, w_uq, kv_norm, w_ukv, w_branch_mla,
           s5_a_re, s5_a_im, s5_log_dt, s5_b_re, s5_b_im, s5_c_re, s5_c_im, s5_d, w_glu, b_glu,
           w_branch_s5, w_out, ln_mix_g, ln_mix_b, ln_ffn_g, ln_ffn_b, ffn_w1, ffn_w3, ffn_w2,
           moe_w_router, moe_b_router, moe_w1, moe_w3, moe_w2):
    B, N, D = x.shape
    C = ctx.shape[1]
    depth = w_mod.shape[0]
    QL, KL = q_norm.shape[1], kv_norm.shape[1]
    SW = s5_d.shape[1]
    H = N_HEADS
    NX, NC_ROWS = B * N, B * C
    T = NX + NC_ROWS
    alpha = (2 * depth) ** 0.25
    q_scale = (QK_NOPE + QK_ROPE) ** -0.5 * math.log2(math.e)
    o_ckv, o_kr, o_u, o_g = QL, QL + KL, QL + KL + QK_ROPE, QL + KL + QK_ROPE + SW
    assert N % C == 0 and N % 256 == 0 and NC_ROWS % 256 == 0

    tm_all = _pick(T, (1536, 1024, 768, 512, 384, 256, 128))
    tm_x = _pick(NX, (1024, 512, 256, 128))
    tile_n = lambda n: _pick(n, (512, 256, 128))

    n_cond = B + 1
    cond = jnp.concatenate([c, c_ctx[None], jnp.zeros((-n_cond % 8, D), F32)], axis=0)
    mods = mod_vectors(cond, w_mod, b_mod)
    mods = mods.reshape(depth, cond.shape[0], 6, D)

    def mvec(l, k):
        return mods[l, :n_cond, k][:, None, :]

    rope_cos, rope_up, rope_dn = _rope_tables(B, N, NC_ROWS)
    rope_cos_t, rope_up_t, rope_dn_t = rope_cos.T, rope_up.T, rope_dn.T
    xt =jnp.concatenate([x.reshape(NX, D), ctx.reshape(NC_ROWS, D)], axis=0)
    h = modulate_ln(xt, mvec(0, 0), mvec(0, 1), rows_per_group=N)

    for l in range(depth):
        need_ctx = l < depth - 1
        rows = T if need_ctx else NX
        tm_r = tm_all if need_ctx else tm_x

        wi = w_in[l]
        w_cq = wi[:, :o_ckv].astype(BF16)
        w_ckv = wi[:, o_ckv:o_kr].astype(BF16)
        w_kr = jnp.concatenate(
            [jnp.zeros((D, QK_NOPE), F32), wi[:, o_kr:o_u],
             jnp.zeros((D, HEAD_PAD - QK_NOPE - QK_ROPE), F32)], axis=1).astype(BF16)
        w_u = wi[:, o_u:o_g].astype(BF16)
        w_gm = wi[:, o_g:o_g + D].astype(BF16)
        w_gs = wi[:, o_g + D:].astype(BF16)
        wq = w_uq[l].reshape(QL, H, QK_NOPE + QK_ROPE)
        w_q = _head_cat_cols(wq[:, :, :QK_NOPE].reshape(QL, -1), wq[:, :, QK_NOPE:].reshape(QL, -1),
                             QK_NOPE, QK_ROPE).astype(BF16)
        wkv = w_ukv[l].reshape(KL, H, QK_NOPE + V_HEAD)
        w_k = _head_cat_cols(wkv[:, :, :QK_NOPE].reshape(KL, -1), None, QK_NOPE, 0).astype(BF16)
        w_v = wkv[:, :, QK_NOPE:].reshape(KL, H * V_HEAD).astype(BF16)

        cqn = matmul([h], [w_cq], out_dtype=BF16, tm=tm_r // 2, tn=QL, m_rows=rows,
                     epilogue=_rms_pro, extras=((q_norm[l].reshape(1, QL), "col"),), name="in_cq")
        ckvn = matmul([h], [w_ckv], out_dtype=BF16, tm=tm_all, tn=KL,
                      epilogue=_rms_pro, extras=((kv_norm[l].reshape(1, KL), "col"),), name="in_ckv")
        krp = matmul([h], [w_kr], out_dtype=F32, tm=tm_all, tn=HEAD_PAD, name="in_kr")
        u = matmul([h], [w_u], out_dtype=F32, tm=tm_all, tn=tile_n(SW), name="in_s5")

        qt = matmul([w_q.T], [cqn], nt=True, out_dtype=BF16, tm=_pick(H * HEAD_PAD, (1024, 512, 256)), tn=512,
                    n_cols=rows, epilogue=lambda acc, cs, up, dn: _rope_apply_t(acc, cs, up, dn) * q_scale,
                    extras=((rope_cos_t, "coltab_tiled"), (rope_up_t, "coltab_tiled"), (rope_dn_t, "coltab_tiled")),
                    name="mla_qt")
        kh = matmul([ckvn], [w_k], out_dtype=BF16, tm=tm_all, tn=512,
                    epilogue=lambda acc, kr, cs, up, dn: acc + jnp.tile(_rope_apply(kr, cs, up, dn),
                                                                        (1, acc.shape[1] // HEAD_PAD)),
                    extras=((krp, "rowtab"), (rope_cos, "rowtab"), (rope_up, "rowtab"), (rope_dn, "rowtab")),
                    name="mla_k")
        vt = matmul([w_v.T], [ckvn], nt=True, out_dtype=BF16, tm=_pick(H * V_HEAD, (1024, 512, 256, 128)),
                    tn=512, name="mla_vt")
        tq = _pick(N, (512, 256, 128))
        o_x = flash_attention(qt, kh, vt, n_batch=B, q_row0=0, q_len=N, segs=[(0, N), (NX, C)],
                              tq=tq, tk=512, name="flash_x")
        if need_ctx:
            o_c = flash_attention(qt, kh, vt, n_batch=B, q_row0=NX, q_len=C, segs=[(NX, C)],
                                  tq=_pick(C, (256, 128)), tk=512, name="flash_ctx")
            o_all = jnp.concatenate([o_x, o_c], axis=0)
        else:
            o_all = o_x

        tables = _s5_tables(s5_a_re[l], s5_a_im[l], s5_log_dt[l], s5_b_re[l], s5_b_im[l],
                            s5_c_re[l], s5_c_im[l])
        y = s5_mix(u, tables, n_batch=B, seq=N, ctx_len=C)
        d_row = s5_d[l].reshape(1, SW)
        glu_pro = lambda yv, uv, dv: _gelu_tanh(yv + dv * uv)
        ys = matmul([y, u], [w_glu[l].astype(BF16)], out_dtype=BF16, tm=tm_r // 2, tn=tile_n(SW), m_rows=rows,
                    prologue=glu_pro, pro_consts=(d_row,),
                    epilogue=lambda acc, yv, uv, dv, bv: (lambda gg: gg * jax.nn.sigmoid(acc + bv))(
                        _gelu_tanh(yv + dv * uv)),
                    extras=((y, "tile"), (u, "tile"), (d_row, "col"), (b_glu[l].reshape(1, SW), "col")),
                    name="s5_glu")

        bg = b_gate[l]
        m1 = matmul([o_all], [w_branch_mla[l].astype(BF16)], out_dtype=F32, tm=tm_r, tn=512, m_rows=rows,
                    name="branch_mla")
        gm = matmul([h], [w_gm], out_dtype=F32, tm=tm_r, tn=512, m_rows=rows,
                    epilogue=lambda acc, bv, mv: jax.nn.sigmoid(acc + bv) * mv,
                    extras=((bg[:D].reshape(1, D), "col"), (m1, "tile")), name="gate_mla")
        m2 = matmul([ys], [w_branch_s5[l].astype(BF16)], out_dtype=F32, tm=tm_r, tn=512, m_rows=rows,
                    name="branch_s5")
        merged = matmul([h], [w_gs], out_dtype=BF16, tm=tm_r, tn=512, m_rows=rows,
                        epilogue=lambda acc, bv, mv, pv: jax.nn.sigmoid(acc + bv) * mv + pv,
                        extras=((bg[D:].reshape(1, D), "col"), (m2, "tile"), (gm, "tile")), name="gate_s5")
        mix = matmul([merged], [w_out[l].astype(BF16)], out_dtype=F32, tm=tm_r, tn=512, m_rows=rows,
                     name="out_proj")
        xt, h2 = residual_ln(xt, mix, mvec(l, 2), ln_mix_g[l], ln_mix_b[l], mvec(l, 3), mvec(l, 4),
                             alpha=alpha, rows_per_group=N, m_rows=rows)

        if l % 2 == 0:
            fi = l // 2
            dff = ffn_w1.shape[2]
            dff_p = -(-dff // 512) * 512
            w1 = _pad_cols(ffn_w1[fi], dff_p).astype(BF16)
            w3 = _pad_cols(ffn_w3[fi], dff_p).astype(BF16)
            w2 = jnp.pad(ffn_w2[fi], ((0, dff_p - dff), (0, 0))).astype(BF16)
            act = matmul([h2], [w1, w3], out_dtype=BF16, tm=tm_r, tn=256, m_rows=rows,
                         epilogue=lambda a, b: a * jax.nn.sigmoid(a) * b, name="ffn_up")
            tk = _pick(dff_p, (2816, 2048, 1024, 512))
            ff = matmul_ksplit(act, w2, out_dtype=F32, tm=tm_r, tn=512, tk=tk, m_rows=rows, name="ffn_down")
        else:
            mi = l // 2
            n_exp = moe_w_router.shape[2]
            w_r = _pad_cols(moe_w_router[mi], LANE).astype(BF16)
            b_r = jnp.pad(moe_b_router[mi], (0, LANE - n_exp)).reshape(1, LANE)
            logits = matmul([h2], [w_r], out_dtype=F32, tm=tm_r, tn=LANE, m_rows=rows, name="router_logits")
            comb = router_combine(logits, b_r, n_exp=n_exp)
            ff = None
            for e in range(n_exp):
                act = matmul([h2], [moe_w1[mi, e].astype(BF16), moe_w3[mi, e].astype(BF16)], out_dtype=BF16,
                             tm=tm_r, tn=256, m_rows=rows,
                             epilogue=lambda a, b: a * jax.nn.sigmoid(a) * b, name="moe_up")
                if ff is None:
                    ff = matmul([act], [moe_w2[mi, e].astype(BF16)], out_dtype=F32, tm=tm_r, tn=512, m_rows=rows,
                                epilogue=lambda acc, cw, e=e: cw[:, e:e + 1] * acc,
                                extras=((comb, "rowtab"),), name="moe_down")
                else:
                    ff = matmul([act], [moe_w2[mi, e].astype(BF16)], out_dtype=F32, tm=tm_r, tn=512, m_rows=rows,
                                epilogue=lambda acc, cw, prev, e=e: prev + cw[:, e:e + 1] * acc,
                                extras=((comb, "rowtab"), (ff, "tile")), name="moe_down")
        if need_ctx:
            xt, h = residual_ln(xt, ff, mvec(l, 5), ln_ffn_g[l], ln_ffn_b[l], mvec(l + 1, 0), mvec(l + 1, 1),
                                alpha=alpha, rows_per_group=N, m_rows=rows)
        else:
            xt, _ = residual_ln(xt, ff, mvec(l, 5), ln_ffn_g[l], ln_ffn_b[l], None, None,
                                alpha=alpha, rows_per_group=N, m_rows=rows)
    return xt[:NX].reshape(B, N, D)
```

```python
import functools
import math

import jax
import jax.numpy as jnp
from jax import lax
from jax.experimental import pallas as pl
from jax.experimental.pallas import tpu as pltpu

N_HEADS = 32
QK_NOPE = 128
QK_ROPE = 64
V_HEAD = 128
ROPE_THETA = 10000.0
GRID_W = 64
S5_GROUP = 16
S5_STATE = 64
TOP_K = 2
LN_EPS = 1e-6
RMS_EPS = 1e-6

HEAD_PAD = 256
S5_L = 16
FLASH_ONES = 16
LANE = 128
VMEM_LIMIT_BYTES = 56 * 2**20

F32 = jnp.float32
BF16 = jnp.bfloat16


def _cparams(sem):
    return pltpu.CompilerParams(dimension_semantics=sem, vmem_limit_bytes=VMEM_LIMIT_BYTES)


def _pick(n, prefs):
    for p in prefs:
        if n % p == 0:
            return p
    raise ValueError(f"no tile in {prefs} divides {n}")


def _mm_body(*refs, n_a, n_pc, n_b, kinds, prologue, epilogue, tm, tn, nt):
    a_refs = refs[:n_a]
    pc_refs = refs[n_a:n_a + n_pc]
    b_refs = refs[n_a + n_pc:n_a + n_pc + n_b]
    ex_refs = refs[n_a + n_pc + n_b:n_a + n_pc + n_b + len(kinds)]
    o_ref = refs[n_a + n_pc + n_b + len(kinds)]
    if prologue is not None:
        a_s = refs[n_a + n_pc + n_b + len(kinds) + 1]

        @pl.when(pl.program_id(1) == 0)
        def _():
            a_s[...] = prologue(*[r[...] for r in a_refs], *[r[...] for r in pc_refs]).astype(BF16)

        a = a_s[...]
    else:
        a = a_refs[0][...]
    if nt:
        accs = [lax.dot_general(a, b[...], (((1,), (1,)), ((), ())), preferred_element_type=F32)
                for b in b_refs]
    else:
        accs = [jnp.dot(a, b[...].astype(BF16), preferred_element_type=F32) for b in b_refs]
    exs = []
    for r, kind in zip(ex_refs, kinds):
        v = r[...]
        if kind == "rowtab_tiled":
            v = jnp.tile(v, (1, tn // v.shape[1]))
        elif kind == "coltab_tiled":
            v = jnp.tile(v, (tm // v.shape[0], 1))
        exs.append(v)
    o_ref[...] = epilogue(*accs, *exs).astype(o_ref.dtype)


def matmul(a_list, b_list, *, out_dtype, tm, tn, m_rows=None, n_cols=None, nt=False, epilogue=None,
           extras=(), prologue=None, pro_consts=(), name="mm"):
    K = a_list[0].shape[1]
    N = b_list[0].shape[0 if nt else 1] if n_cols is None else n_cols
    M = a_list[0].shape[0] if m_rows is None else m_rows
    tn = _pick(N, tuple(t for t in (tn, 512, 256, 128) if t <= tn))
    assert M % tm == 0 and N % tn == 0, (M, tm, N, tn)
    if epilogue is None:
        epilogue = lambda acc: acc
    if prologue is None:
        assert len(a_list) == 1 and a_list[0].dtype == BF16
    in_specs = [pl.BlockSpec((tm, K), lambda i, j: (i, 0)) for _ in a_list]
    in_specs += [pl.BlockSpec(c.shape, lambda i, j: (0, 0)) for c in pro_consts]
    if nt:
        in_specs += [pl.BlockSpec((tn, K), lambda i, j: (j, 0)) for _ in b_list]
    else:
        in_specs += [pl.BlockSpec((K, tn), lambda i, j: (0, j)) for _ in b_list]
    kinds = []
    ex_arrays = []
    for arr, kind in extras:
        kinds.append(kind)
        ex_arrays.append(arr)
        if kind == "tile":
            in_specs.append(pl.BlockSpec((tm, tn), lambda i, j: (i, j)))
        elif kind == "col":
            in_specs.append(pl.BlockSpec((1, tn), lambda i, j: (0, j)))
        elif kind in ("rowtab", "rowtab_tiled"):
            in_specs.append(pl.BlockSpec((tm, arr.shape[1]), lambda i, j: (i, 0)))
        elif kind == "coltab_tiled":
            in_specs.append(pl.BlockSpec((arr.shape[0], tn), lambda i, j: (0, j)))
        else:
            raise ValueError(kind)
    scratch = [pltpu.VMEM((tm, K), BF16)] if prologue is not None else []
    body = functools.partial(_mm_body, n_a=len(a_list), n_pc=len(pro_consts), n_b=len(b_list),
                             kinds=tuple(kinds), prologue=prologue, epilogue=epilogue, tm=tm, tn=tn, nt=nt)
    return pl.pallas_call(
        body,
        grid=(M // tm, N // tn),
        in_specs=in_specs,
        out_specs=pl.BlockSpec((tm, tn), lambda i, j: (i, j)),
        out_shape=jax.ShapeDtypeStruct((M, N), out_dtype),
        scratch_shapes=scratch,
        compiler_params=_cparams(("parallel", "arbitrary")),
        name=name,
    )(*a_list, *pro_consts, *b_list, *ex_arrays)


def _mmk_body(a_ref, b_ref, *rest, n_ex, epilogue, nk):
    ex_refs = rest[:n_ex]
    o_ref = rest[n_ex]
    acc_ref = rest[n_ex + 1]
    k = pl.program_id(2)

    @pl.when(k == 0)
    def _():
        acc_ref[...] = jnp.zeros_like(acc_ref)

    acc_ref[...] += jnp.dot(a_ref[...], b_ref[...], preferred_element_type=F32)

    @pl.when(k == nk - 1)
    def _():
        o_ref[...] = epilogue(acc_ref[...], *[e[...] for e in ex_refs]).astype(o_ref.dtype)


def matmul_ksplit(a, b, *, out_dtype, tm, tn, tk, m_rows=None, epilogue=None, extras=(), name="mmk"):
    K = a.shape[1]
    N = b.shape[1]
    M = a.shape[0] if m_rows is None else m_rows
    tn = _pick(N, tuple(t for t in (tn, 512, 256, 128) if t <= tn))
    assert M % tm == 0 and N % tn == 0 and K % tk == 0, (M, tm, N, tn, K, tk)
    if epilogue is None:
        epilogue = lambda acc: acc
    in_specs = [pl.BlockSpec((tm, tk), lambda i, j, k: (i, k)),
                pl.BlockSpec((tk, tn), lambda i, j, k: (k, j))]
    ex_arrays = []
    for arr, kind in extras:
        ex_arrays.append(arr)
        if kind == "tile":
            in_specs.append(pl.BlockSpec((tm, tn), lambda i, j, k: (i, j)))
        elif kind == "rowtab":
            in_specs.append(pl.BlockSpec((tm, arr.shape[1]), lambda i, j, k: (i, 0)))
        else:
            raise ValueError(kind)
    nk = K // tk
    body = functools.partial(_mmk_body, n_ex=len(ex_arrays), epilogue=epilogue, nk=nk)
    return pl.pallas_call(
        body,
        grid=(M // tm, N // tn, nk),
        in_specs=in_specs,
        out_specs=pl.BlockSpec((tm, tn), lambda i, j, k: (i, j)),
        out_shape=jax.ShapeDtypeStruct((M, N), out_dtype),
        scratch_shapes=[pltpu.VMEM((tm, tn), F32)],
        compiler_params=_cparams(("parallel", "parallel", "arbitrary")),
        name=name,
    )(a, b, *ex_arrays)


def _mod_body(c_ref, w_ref, b_ref, o_ref):
    c = c_ref[...]
    act = (c * jax.nn.sigmoid(c)).astype(BF16)
    o_ref[...] = jnp.dot(act, w_ref[...].astype(BF16), preferred_element_type=F32) + b_ref[...]


def mod_vectors(cond, w_mod, b_mod):
    depth, d, n = w_mod.shape
    r = cond.shape[0]
    tn = _pick(n, (1024, 512, 256, 128))
    return pl.pallas_call(
        _mod_body,
        grid=(depth, n // tn),
        in_specs=[pl.BlockSpec((r, d), lambda l, j: (0, 0)),
                  pl.BlockSpec((None, d, tn), lambda l, j: (l, 0, j)),
                  pl.BlockSpec((None, 1, tn), lambda l, j: (l, 0, j))],
        out_specs=pl.BlockSpec((None, r, tn), lambda l, j: (l, 0, j)),
        out_shape=jax.ShapeDtypeStruct((depth, r, n), F32),
        compiler_params=_cparams(("parallel", "parallel")),
        name="mod_vectors",
    )(cond, w_mod, b_mod.reshape(depth, 1, n))


def _ln_rows(x):
    mu = jnp.mean(x, axis=-1, keepdims=True)
    xc = x - mu
    var = jnp.mean(xc * xc, axis=-1, keepdims=True)
    return xc * lax.rsqrt(var + LN_EPS)


def _modln_body(x_ref, sh_ref, sc_ref, h_ref):
    h_ref[...] = (_ln_rows(x_ref[...]) * (1.0 + sc_ref[...]) + sh_ref[...]).astype(h_ref.dtype)


def modulate_ln(x, shift, scale, *, rows_per_group, m_rows=None, tr=256):
    M = x.shape[0] if m_rows is None else m_rows
    d = x.shape[1]
    assert M % tr == 0 and rows_per_group % tr == 0
    gmap = lambda i: ((i * tr) // rows_per_group, 0, 0)
    return pl.pallas_call(
        _modln_body,
        grid=(M // tr,),
        in_specs=[pl.BlockSpec((tr, d), lambda i: (i, 0)),
                  pl.BlockSpec((None, 1, d), gmap),
                  pl.BlockSpec((None, 1, d), gmap)],
        out_specs=pl.BlockSpec((tr, d), lambda i: (i, 0)),
        out_shape=jax.ShapeDtypeStruct((M, d), BF16),
        compiler_params=_cparams(("parallel",)),
        name="modulate_ln",
    )(x, shift, scale)


def _resln_body(x_ref, y_ref, gate_ref, g_ref, b_ref, sh_ref, sc_ref, xo_ref, h_ref, *, alpha):
    xn = _ln_rows(alpha * x_ref[...] + gate_ref[...] * y_ref[...]) * g_ref[...] + b_ref[...]
    xo_ref[...] = xn
    h_ref[...] = (_ln_rows(xn) * (1.0 + sc_ref[...]) + sh_ref[...]).astype(h_ref.dtype)


def _resln_last_body(x_ref, y_ref, gate_ref, g_ref, b_ref, xo_ref, *, alpha):
    xo_ref[...] = _ln_rows(alpha * x_ref[...] + gate_ref[...] * y_ref[...]) * g_ref[...] + b_ref[...]


def residual_ln(x, y, gate, ln_g, ln_b, shift, scale, *, alpha, rows_per_group, m_rows=None, tr=256,
                h_dtype=None):
    M = x.shape[0] if m_rows is None else m_rows
    d = x.shape[1]
    assert M % tr == 0 and rows_per_group % tr == 0
    gmap = lambda i: ((i * tr) // rows_per_group, 0, 0)
    row = pl.BlockSpec((tr, d), lambda i: (i, 0))
    vec = pl.BlockSpec((1, d), lambda i: (0, 0))
    gvec = pl.BlockSpec((None, 1, d), gmap)
    if shift is None:
        return pl.pallas_call(
            functools.partial(_resln_last_body, alpha=alpha),
            grid=(M // tr,),
            in_specs=[row, row, gvec, vec, vec],
            out_specs=row,
            out_shape=jax.ShapeDtypeStruct((M, d), F32),
            compiler_params=_cparams(("parallel",)),
            name="residual_ln_last",
        )(x, y, gate, ln_g.reshape(1, d), ln_b.reshape(1, d)), None
    return pl.pallas_call(
        functools.partial(_resln_body, alpha=alpha),
        grid=(M // tr,),
        in_specs=[row, row, gvec, vec, vec, gvec, gvec],
        out_specs=[row, row],
        out_shape=[jax.ShapeDtypeStruct((M, d), F32), jax.ShapeDtypeStruct((M, d), h_dtype or BF16)],
        compiler_params=_cparams(("parallel",)),
        name="residual_ln",
    )(x, y, gate, ln_g.reshape(1, d), ln_b.reshape(1, d), shift, scale)


def _flash_body(qt_ref, *refs, seg_lens, tk):
    n_seg = len(seg_lens)
    kv_refs = refs[:2 * n_seg]
    o_ref = refs[2 * n_seg]
    acc_ref, st_a, st_b = refs[2 * n_seg + 1:]
    bufs = (st_a, st_b)
    qt = qt_ref[...]
    tq = qt.shape[1]
    acc_ref[...] = jnp.zeros_like(acc_ref)
    m = jnp.full((1, tq), -jnp.inf, F32)

    def scores(seg, c, tks):
        k = kv_refs[2 * seg][pl.ds(pl.multiple_of(c * tks, tks), tks), :]
        return jnp.dot(k, qt, preferred_element_type=F32)

    def absorb(st, seg, c, tks, m_old):
        vt = kv_refs[2 * seg + 1][:, pl.ds(pl.multiple_of(c * tks, tks), tks)]
        vt1 = jnp.concatenate([vt, jnp.ones((FLASH_ONES, tks), BF16)], axis=0)
        m_new = jnp.maximum(m_old, jnp.max(st, axis=0, keepdims=True))
        p = jnp.exp2(st - m_new).astype(BF16)
        corr = jnp.exp2(m_old - m_new)
        acc_ref[...] = corr * acc_ref[...] + jnp.dot(vt1, p, preferred_element_type=F32)
        return m_new

    def run_static(chunks, cur, m):
        for i, (seg, c, tks) in enumerate(chunks):
            if i + 1 < len(chunks):
                nseg, nc, ntks = chunks[i + 1]
                bufs[1 - cur][:ntks] = scores(nseg, nc, ntks)
            m = absorb(bufs[cur][:tks], seg, c, tks, m)
            cur = 1 - cur
        return m

    chunk_counts = [(s, min(tk, ln), ln // min(tk, ln)) for s, ln in enumerate(seg_lens)]
    s0, tk0, n0 = chunk_counts[0]
    rest = [(s, c, tks) for s, tks, n in chunk_counts[1:] for c in range(n)]
    if n0 >= 4 and n0 % 2 == 0:
        st_a[...] = scores(s0, 0, tk0)

        def pair(j, m):
            c0 = 2 * j
            st_b[...] = scores(s0, c0 + 1, tk0)
            m = absorb(st_a[...], s0, c0, tk0, m)
            st_a[...] = scores(s0, c0 + 2, tk0)
            return absorb(st_b[...], s0, c0 + 1, tk0, m)

        m = lax.fori_loop(0, n0 // 2 - 1, pair, m)
        m = run_static([(s0, n0 - 2, tk0), (s0, n0 - 1, tk0)] + rest, 0, m)
    else:
        chunks = [(s0, c, tk0) for c in range(n0)] + rest
        st_a[:tk0] = scores(s0, 0, tk0)
        m = run_static(chunks, 0, m)
    acc = acc_ref[...]
    o_ref[...] = (acc[:V_HEAD] / acc[V_HEAD:V_HEAD + 1]).T.astype(o_ref.dtype)


def flash_attention(qt, k, vt, *, n_batch, q_row0, q_len, segs, tq, tk, name="flash"):
    h = N_HEADS
    nq = q_len // tq
    assert q_len % tq == 0 and q_row0 % tq == 0
    in_specs = [pl.BlockSpec((HEAD_PAD, tq), lambda b, hh, i: (hh, q_row0 // tq + b * nq + i))]
    args = [qt]
    for row0, ln in segs:
        assert row0 % ln == 0
        in_specs.append(pl.BlockSpec((ln, HEAD_PAD), lambda b, hh, i, r=row0 // ln: (r + b, hh)))
        in_specs.append(pl.BlockSpec((V_HEAD, ln), lambda b, hh, i, r=row0 // ln: (hh, r + b)))
        args += [k, vt]
    body = functools.partial(_flash_body, seg_lens=tuple(ln for _, ln in segs), tk=tk)
    return pl.pallas_call(
        body,
        grid=(n_batch, h, nq),
        in_specs=in_specs,
        out_specs=pl.BlockSpec((tq, V_HEAD), lambda b, hh, i: (b * nq + i, hh)),
        out_shape=jax.ShapeDtypeStruct((n_batch * q_len, h * V_HEAD), BF16),
        scratch_shapes=[pltpu.VMEM((V_HEAD + FLASH_ONES, tq), F32),
                        pltpu.VMEM((tk, tq), F32), pltpu.VMEM((tk, tq), F32)],
        compiler_params=_cparams(("parallel", "parallel", "arbitrary")),
        name=name,
    )(*args)


def _s5_drive_body(u_ref, w_ref, o_ref):
    for g in range(u_ref.shape[0]):
        o_ref[g] = jnp.dot(u_ref[g], w_ref[g], preferred_element_type=F32)


def _s5_out_body(u_ref, s_ref, t_ref, q_ref, o_ref):
    for g in range(u_ref.shape[0]):
        o_ref[g] = (jnp.dot(u_ref[g], t_ref[g], preferred_element_type=F32)
                    + jnp.dot(s_ref[g], q_ref[g], preferred_element_type=F32))


def _s5_scan_body(wre_ref, wim_ref, lr_ref, li_ref, sre_ref, sim_ref, st_re, st_im):
    @pl.when(pl.program_id(1) == 0)
    def _():
        st_re[...] = jnp.zeros_like(st_re)
        st_im[...] = jnp.zeros_like(st_im)

    lr = lr_ref[...]
    li = li_ref[...]

    def step(c, carry):
        s_re, s_im = carry
        sre_ref[c] = s_re
        sim_ref[c] = s_im
        n_re = lr * s_re - li * s_im + wre_ref[c]
        n_im = lr * s_im + li * s_re + wim_ref[c]
        return n_re, n_im

    s_re, s_im = lax.fori_loop(0, wre_ref.shape[0], step, (st_re[...], st_im[...]))
    st_re[...] = s_re
    st_im[...] = s_im


def s5_scan(w_re, w_im, lam_re, lam_im, *, cb):
    b, nc, g, p2 = w_re.shape
    assert nc % cb == 0
    blk = pl.BlockSpec((None, cb, g, p2), lambda bi, j: (bi, j, 0, 0))
    coef = pl.BlockSpec((g, p2), lambda bi, j: (0, 0))
    return pl.pallas_call(
        _s5_scan_body,
        grid=(b, nc // cb),
        in_specs=[blk, blk, coef, coef],
        out_specs=[blk, blk],
        out_shape=[jax.ShapeDtypeStruct(w_re.shape, F32)] * 2,
        scratch_shapes=[pltpu.VMEM((g, p2), F32), pltpu.VMEM((g, p2), F32)],
        compiler_params=_cparams(("parallel", "arbitrary")),
        name="s5_scan",
    )(w_re, w_im, lam_re, lam_im)


def _s5_tables(a_re, a_im, log_dt, b_re, b_im, c_re, c_im):
    L, P, Hh = S5_L, S5_STATE, S5_GROUP
    hp = lax.Precision.HIGHEST
    dt = jnp.exp(log_dt.astype(F32))[..., None]
    ar, ai = a_re.astype(F32), a_im.astype(F32)
    j = jnp.arange(L + 1, dtype=F32)[:, None, None, None]
    mag = jnp.exp(j * ar * dt)
    pr, pi = mag * jnp.cos(j * ai * dt), mag * jnp.sin(j * ai * dt)
    lr, li = pr[1], pi[1]
    nr = lr - 1.0
    den = ar * ar + ai * ai
    f_re = ((nr * ar + li * ai) / den)[..., None]
    f_im = ((li * ar - nr * ai) / den)[..., None]
    br, bi = b_re.astype(F32), b_im.astype(F32)
    bb_re = f_re * br - f_im * bi
    bb_im = f_re * bi + f_im * br
    cr, ci = c_re.astype(F32), c_im.astype(F32)

    zr = pr[:L, ..., None] * bb_re - pi[:L, ..., None] * bb_im
    zi = pr[:L, ..., None] * bb_im + pi[:L, ..., None] * bb_re
    kj = (jnp.einsum('dghp,jdgpk->jdghk', cr, zr, precision=hp)
          - jnp.einsum('dghp,jdgpk->jdghk', ci, zi, precision=hp))
    t_idx = jnp.arange(L)
    lag = t_idx[None, :] - t_idx[:, None]
    kf = kj[:, 0][jnp.clip(lag, 0, L - 1)]
    kb = kj[:, 1][jnp.clip(-lag, 0, L - 1)]
    tm4 = (jnp.where((lag >= 0)[:, :, None, None, None], kf, 0.0)
           + jnp.where((lag <= 0)[:, :, None, None, None], kb, 0.0))
    g = tm4.shape[2]
    tmat = jnp.transpose(tm4, (2, 0, 4, 1, 3)).reshape(g, L * Hh, L * Hh)

    kk = jnp.arange(L)
    pf_r, pf_i = pr[L - 1 - kk, 0], pi[L - 1 - kk, 0]
    pb_r, pb_i = pr[kk, 1], pi[kk, 1]
    wf_re = pf_r[..., None] * bb_re[0] - pf_i[..., None] * bb_im[0]
    wf_im = pf_r[..., None] * bb_im[0] + pf_i[..., None] * bb_re[0]
    wb_re = pb_r[..., None] * bb_re[1] - pb_i[..., None] * bb_im[1]
    wb_im = pb_r[..., None] * bb_im[1] + pb_i[..., None] * bb_re[1]
    wcat = jnp.concatenate([wf_re, wb_re, wf_im, wb_im], axis=2)
    wmat = jnp.transpose(wcat, (1, 0, 3, 2)).reshape(g, L * Hh, 4 * P)

    qf_r, qf_i = pr[kk + 1, 0], pi[kk + 1, 0]
    qb_r, qb_i = pr[L - kk, 1], pi[L - kk, 1]

    def qpair(c_r, c_i, q_r, q_i):
        return (c_r[None] * q_r[:, :, None, :] - c_i[None] * q_i[:, :, None, :],
                -c_r[None] * q_i[:, :, None, :] - c_i[None] * q_r[:, :, None, :])

    qf_re, qf_im = qpair(cr[0], ci[0], qf_r, qf_i)
    qb_re, qb_im = qpair(cr[1], ci[1], qb_r, qb_i)
    qcat = jnp.concatenate([qf_re, qb_re, qf_im, qb_im], axis=3)
    qmat = jnp.transpose(qcat, (1, 3, 0, 2)).reshape(g, 4 * P, L * Hh)

    lam_re = jnp.concatenate([pr[L, 0], pr[L, 1]], axis=-1)
    lam_im = jnp.concatenate([pi[L, 0], pi[L, 1]], axis=-1)
    return tmat.astype(BF16), wmat.astype(BF16), qmat.astype(BF16), lam_re, lam_im


def s5_mix(u, tables, *, n_batch, seq, ctx_len, gb=8):
    tmat, wmat, qmat, lam_re, lam_im = tables
    L, Hh, P = S5_L, S5_GROUP, S5_STATE
    w_tot = u.shape[1]
    g = w_tot // Hh
    ncx, ncc = seq // L, ctx_len // L
    nc = ncx + ncc
    nx = n_batch * seq
    ux = u[:nx].reshape(n_batch, ncx, L, g, Hh)
    uc = u[nx:].reshape(n_batch, ncc, L, g, Hh)
    uall = jnp.concatenate([uc, ux], axis=1)
    ug = jnp.transpose(uall, (3, 0, 1, 2, 4)).reshape(g, n_batch * nc, L * Hh).astype(BF16)
    rows = n_batch * nc
    gb = math.gcd(gb, g)
    grp = lambda r, c: pl.BlockSpec((gb, r, c), lambda i: (i, 0, 0))

    w = pl.pallas_call(
        _s5_drive_body,
        grid=(g // gb,),
        in_specs=[grp(rows, L * Hh), grp(L * Hh, 4 * P)],
        out_specs=grp(rows, 4 * P),
        out_shape=jax.ShapeDtypeStruct((g, rows, 4 * P), F32),
        compiler_params=_cparams(("parallel",)),
        name="s5_drive",
    )(ug, wmat)

    w5 = w.reshape(g, n_batch, nc, 4, P)
    w5 = jnp.transpose(w5, (1, 2, 0, 3, 4))
    brev = lambda a: jnp.concatenate([a[:, :ncc][:, ::-1], a[:, ncc:][:, ::-1]], axis=1)
    w_re = jnp.concatenate([w5[:, :, :, 0], brev(w5[:, :, :, 1])], axis=-1)
    w_im = jnp.concatenate([w5[:, :, :, 2], brev(w5[:, :, :, 3])], axis=-1)
    cb = _pick(nc, (48, 44, 33, 32, 24, 22, 16, 12, 11, 8, 6, 4, 3, 2, 1))
    s_re, s_im = s5_scan(w_re, w_im, lam_re, lam_im, cb=cb)
    s4 = jnp.stack([s_re[..., :P], brev(s_re[..., P:]), s_im[..., :P], brev(s_im[..., P:])], axis=3)
    sg = jnp.transpose(s4, (2, 0, 1, 3, 4)).reshape(g, rows, 4 * P).astype(BF16)

    y = pl.pallas_call(
        _s5_out_body,
        grid=(g // gb,),
        in_specs=[grp(rows, L * Hh), grp(rows, 4 * P), grp(L * Hh, L * Hh), grp(4 * P, L * Hh)],
        out_specs=grp(rows, L * Hh),
        out_shape=jax.ShapeDtypeStruct((g, rows, L * Hh), F32),
        compiler_params=_cparams(("parallel",)),
        name="s5_out",
    )(ug, sg, tmat, qmat)

    y5 = jnp.transpose(y.reshape(g, n_batch, nc, L, Hh), (1, 2, 3, 0, 4))
    yx = y5[:, ncc:].reshape(nx, w_tot)
    yc = y5[:, :ncc].reshape(n_batch * ctx_len, w_tot)
    return jnp.concatenate([yx, yc], axis=0)


R_E1, R_E2, R_W1, R_W2, R_RANK1, R_RANK2 = range(6)


def _router_body(lg_ref, b_ref, meta_ref, cnt_ref, carry_ref, *, n_exp):
    @pl.when(pl.program_id(0) == 0)
    def _():
        carry_ref[...] = jnp.zeros_like(carry_ref)

    lg = lg_ref[...] + b_ref[...]
    tr = lg.shape[0]
    lane = lax.broadcasted_iota(jnp.int32, lg.shape, 1).astype(F32)
    neg = jnp.float32(-jnp.inf)
    lg = jnp.where(lane < n_exp, lg, neg)
    m1 = jnp.max(lg, axis=-1, keepdims=True)
    i1 = jnp.min(jnp.where(lg == m1, lane, float(LANE)), axis=-1, keepdims=True)
    lg2 = jnp.where(lane == i1, neg, lg)
    m2 = jnp.max(lg2, axis=-1, keepdims=True)
    i2 = jnp.min(jnp.where(lg2 == m2, lane, float(LANE)), axis=-1, keepdims=True)
    e2 = jnp.exp(m2 - m1)
    den = 1.0 + e2
    sel = jnp.where((lane == i1) | (lane == i2), 1.0, 0.0)
    r_i = lax.broadcasted_iota(jnp.int32, (tr, tr), 0)
    c_i = lax.broadcasted_iota(jnp.int32, (tr, tr), 1)
    tri = jnp.where(r_i > c_i, 1.0, 0.0).astype(BF16)
    before = jnp.dot(tri, sel.astype(BF16), preferred_element_type=F32) + carry_ref[...]
    rank1 = jnp.sum(jnp.where(lane == i1, before, 0.0), axis=-1, keepdims=True)
    rank2 = jnp.sum(jnp.where(lane == i2, before, 0.0), axis=-1, keepdims=True)
    total = carry_ref[...] + jnp.sum(sel, axis=0, keepdims=True)
    carry_ref[...] = total
    cnt_ref[...] = jnp.broadcast_to(total, cnt_ref.shape)
    meta = jnp.zeros_like(lg)
    for idx, val in ((R_E1, i1), (R_E2, i2), (R_W1, 1.0 / den), (R_W2, e2 / den),
                     (R_RANK1, rank1), (R_RANK2, rank2)):
        meta = jnp.where(lane == idx, val, meta)
    meta_ref[...] = meta


def router_top2(logits, b_router_pad, *, n_exp, tr=512):
    m = logits.shape[0]
    tr = _pick(m, (tr, 256, 128, 64, 32, 16, 8))
    return pl.pallas_call(
        functools.partial(_router_body, n_exp=n_exp),
        grid=(m // tr,),
        in_specs=[pl.BlockSpec((tr, LANE), lambda i: (i, 0)), pl.BlockSpec((1, LANE), lambda i: (0, 0))],
        out_specs=[pl.BlockSpec((tr, LANE), lambda i: (i, 0)), pl.BlockSpec((8, LANE), lambda i: (0, 0))],
        out_shape=[jax.ShapeDtypeStruct((m, LANE), F32), jax.ShapeDtypeStruct((8, LANE), F32)],
        scratch_shapes=[pltpu.VMEM((1, LANE), F32)],
        compiler_params=_cparams(("arbitrary",)),
        name="router_top2",
    )(logits, b_router_pad)


def _row_copy_wait(src_ref, dst_ref, sem):
    pltpu.make_async_copy(src_ref.at[pl.ds(0, 1), :], dst_ref.at[pl.ds(0, 1), :], sem).wait()


def _scatter_rows_body(pos_ref, src_ref, dst_in_ref, dst_ref, sem, *, tb):
    del dst_in_ref
    base = pl.program_id(0) * tb

    def issue(t, carry):
        for k in range(TOP_K):
            pltpu.make_async_copy(src_ref.at[pl.ds(base + t, 1), :],
                                  dst_ref.at[pl.ds(pos_ref[0, TOP_K * t + k], 1), :], sem).start()
        return carry

    lax.fori_loop(0, tb, issue, 0)

    def drain(t, carry):
        for k in range(TOP_K):
            _row_copy_wait(src_ref, dst_ref, sem)
        return carry

    lax.fori_loop(0, tb, drain, 0)


def scatter_rows(src, pos, n_rows, *, tb=256):
    m, d = pos.shape[0], src.shape[1]
    tb = _pick(m, (tb, 128, 64, 32, 16, 8))
    pos3 = pos.reshape(m // tb, 1, TOP_K * tb)
    return pl.pallas_call(
        functools.partial(_scatter_rows_body, tb=tb),
        grid=(m // tb,),
        in_specs=[pl.BlockSpec((None, 1, TOP_K * tb), lambda i: (i, 0, 0), memory_space=pltpu.SMEM),
                  pl.BlockSpec(memory_space=pl.ANY),
                  pl.BlockSpec(memory_space=pl.ANY)],
        out_specs=pl.BlockSpec(memory_space=pl.ANY),
        out_shape=jax.ShapeDtypeStruct((n_rows, d), src.dtype),
        scratch_shapes=[pltpu.SemaphoreType.DMA(())],
        input_output_aliases={2: 0},
        compiler_params=_cparams(("arbitrary",)),
        name="moe_scatter_rows",
    )(pos3, src, jnp.zeros((n_rows, d), src.dtype))


def _gather_combine_body(pos_ref, meta_ref, y_ref, o_ref, buf, sem, *, tb):
    def issue(t, carry):
        for k in range(TOP_K):
            pltpu.make_async_copy(y_ref.at[pl.ds(pos_ref[0, TOP_K * t + k], 1), :],
                                  buf.at[k, pl.ds(t, 1), :], sem).start()
        return carry

    lax.fori_loop(0, tb, issue, 0)

    def drain(t, carry):
        for k in range(TOP_K):
            _row_copy_wait(y_ref, buf.at[0], sem)
        return carry

    lax.fori_loop(0, tb, drain, 0)
    meta = meta_ref[...]
    o_ref[...] = meta[:, R_W1:R_W1 + 1] * buf[0] + meta[:, R_W2:R_W2 + 1] * buf[1]


def gather_combine(y, pos, meta, *, tb=256):
    m, d = pos.shape[0], y.shape[1]
    tb = _pick(m, (tb, 128, 64, 32, 16, 8))
    pos3 = pos.reshape(m // tb, 1, TOP_K * tb)
    return pl.pallas_call(
        functools.partial(_gather_combine_body, tb=tb),
        grid=(m // tb,),
        in_specs=[pl.BlockSpec((None, 1, TOP_K * tb), lambda i: (i, 0, 0), memory_space=pltpu.SMEM),
                  pl.BlockSpec((tb, LANE), lambda i: (i, 0)),
                  pl.BlockSpec(memory_space=pl.ANY)],
        out_specs=pl.BlockSpec((tb, d), lambda i: (i, 0)),
        out_shape=jax.ShapeDtypeStruct((m, d), F32),
        scratch_shapes=[pltpu.VMEM((TOP_K, tb, d), F32), pltpu.SemaphoreType.DMA(())],
        compiler_params=_cparams(("arbitrary",)),
        name="moe_gather_combine",
    )(pos3, meta, y)


def _gmm_up_body(te_ref, nv_ref, a_ref, b1_ref, b3_ref, o_ref, a_s):
    del te_ref
    live = pl.program_id(0) < nv_ref[0]
    first_col = pl.program_id(1) == 0

    @pl.when(live)
    def _():
        @pl.when(first_col)
        def _():
            a_s[...] = a_ref[...].astype(BF16)

        a = a_s[...]
        g = jnp.dot(a, b1_ref[...], preferred_element_type=F32)
        u = jnp.dot(a, b3_ref[...], preferred_element_type=F32)
        o_ref[...] = (g * jax.nn.sigmoid(g) * u).astype(o_ref.dtype)

    @pl.when(jnp.logical_not(live))
    def _():
        o_ref[...] = jnp.zeros_like(o_ref)


def _gmm_down_body(te_ref, nv_ref, a_ref, b_ref, o_ref):
    del te_ref
    live = pl.program_id(0) < nv_ref[0]

    @pl.when(live)
    def _():
        o_ref[...] = jnp.dot(a_ref[...], b_ref[...], preferred_element_type=F32)

    @pl.when(jnp.logical_not(live))
    def _():
        o_ref[...] = jnp.zeros_like(o_ref)


def grouped_swiglu(xs, w1, w3, w2, tile_expert, n_valid, *, tm):
    r, d = xs.shape
    f = w1.shape[2]
    n_tiles = r // tm
    tn_up = _pick(f, (512, 256, 128))
    tn_dn = _pick(d, (512, 256, 128))
    act = pl.pallas_call(
        _gmm_up_body,
        grid_spec=pltpu.PrefetchScalarGridSpec(
            num_scalar_prefetch=2,
            grid=(n_tiles, f // tn_up),
            in_specs=[pl.BlockSpec((tm, d), lambda i, j, te, nv: (i, 0)),
                      pl.BlockSpec((None, d, tn_up), lambda i, j, te, nv: (te[i], 0, j)),
                      pl.BlockSpec((None, d, tn_up), lambda i, j, te, nv: (te[i], 0, j))],
            out_specs=pl.BlockSpec((tm, tn_up), lambda i, j, te, nv: (i, j)),
            scratch_shapes=[pltpu.VMEM((tm, d), BF16)]),
        out_shape=jax.ShapeDtypeStruct((r, f), BF16),
        compiler_params=_cparams(("arbitrary", "arbitrary")),
        name="moe_up",
    )(tile_expert, n_valid, xs, w1, w3)
    return pl.pallas_call(
        _gmm_down_body,
        grid_spec=pltpu.PrefetchScalarGridSpec(
            num_scalar_prefetch=2,
            grid=(n_tiles, d // tn_dn),
            in_specs=[pl.BlockSpec((tm, f), lambda i, j, te, nv: (i, 0)),
                      pl.BlockSpec((None, f, tn_dn), lambda i, j, te, nv: (te[i], 0, j))],
            out_specs=pl.BlockSpec((tm, tn_dn), lambda i, j, te, nv: (i, j))),
        out_shape=jax.ShapeDtypeStruct((r, d), F32),
        compiler_params=_cparams(("arbitrary", "arbitrary")),
        name="moe_down",
    )(tile_expert, n_valid, act, w2)


def sparse_moe(hf, meta, counts, w1, w3, w2, *, tm=512):
    m = hf.shape[0]
    n_exp = w1.shape[0]
    tm = _pick(m, (tm, 256, 128))
    cnt = counts.astype(jnp.int32)
    padded = (cnt + tm - 1) // tm * tm
    ends = jnp.cumsum(padded)
    offs = ends - padded
    e1 = meta[:, R_E1].astype(jnp.int32)
    e2 = meta[:, R_E2].astype(jnp.int32)
    pos = jnp.stack([offs[e1] + meta[:, R_RANK1].astype(jnp.int32),
                     offs[e2] + meta[:, R_RANK2].astype(jnp.int32)], axis=1)
    n_tiles = TOP_K * m // tm + n_exp
    tile_expert = jnp.minimum(jnp.searchsorted(ends, jnp.arange(n_tiles) * tm, side="right"),
                              n_exp - 1).astype(jnp.int32)
    n_valid = (ends[-1:] // tm).astype(jnp.int32)
    xs = scatter_rows(hf, pos, n_tiles * tm)
    y = grouped_swiglu(xs, w1, w3, w2, tile_expert, n_valid, tm=tm)
    return gather_combine(y, pos, meta)


def _rms_pro(x, gain):
    return x * lax.rsqrt(jnp.mean(x * x, axis=-1, keepdims=True) + RMS_EPS) * gain


def _gelu_tanh(x):
    return 0.5 * x * (1.0 + jnp.tanh(math.sqrt(2.0 / math.pi) * (x + 0.044715 * (x * x * x))))


def _rope_apply(x, cos, sin_up, sin_dn):
    n = x.shape[-1]
    return x * cos + pltpu.roll(x, n - QK_ROPE // 4, 1) * sin_up + pltpu.roll(x, QK_ROPE // 4, 1) * sin_dn


def _rope_apply_t(x, cos, sin_up, sin_dn):
    n = x.shape[0]
    return x * cos + pltpu.roll(x, n - QK_ROPE // 4, 0) * sin_up + pltpu.roll(x, QK_ROPE // 4, 0) * sin_dn


def _rope_tables(n_batch, seq, n_ctx_rows):
    nf = QK_ROPE // 4
    pos = jnp.arange(seq)
    row = (pos // GRID_W).astype(F32)
    col = (pos % GRID_W).astype(F32)
    inv = ROPE_THETA ** (-jnp.arange(nf, dtype=F32) / nf)
    ar, ac = row[:, None] * inv, col[:, None] * inv
    z = jnp.zeros((seq, nf), F32)
    cos64 = jnp.concatenate([jnp.cos(ar), jnp.cos(ar), jnp.cos(ac), jnp.cos(ac)], axis=1)
    up64 = jnp.concatenate([-jnp.sin(ar), z, -jnp.sin(ac), z], axis=1)
    dn64 = jnp.concatenate([z, jnp.sin(ar), z, jnp.sin(ac)], axis=1)

    def place(t64, fill):
        full = jnp.full((seq, HEAD_PAD), fill, F32).at[:, QK_NOPE:QK_NOPE + QK_ROPE].set(t64)
        full = jnp.tile(full, (n_batch, 1))
        return jnp.concatenate([full, jnp.full((n_ctx_rows, HEAD_PAD), fill, F32)], axis=0)

    return place(cos64, 1.0), place(up64, 0.0), place(dn64, 0.0)


def _pad_cols(w, n):
    return jnp.pad(w, ((0, 0), (0, n - w.shape[1])))


def _head_cat_cols(w_a, w_b, da, db):
    k = w_a.shape[0]
    parts = [w_a.reshape(k, N_HEADS, da)]
    if w_b is not None:
        parts.append(w_b.reshape(k, N_HEADS, db))
    used = da + (db if w_b is not None else 0)
    parts.append(jnp.zeros((k, N_HEADS, HEAD_PAD - used), w_a.dtype))
    return jnp.concatenate(parts, axis=2).reshape(k, N_HEADS * HEAD_PAD)


def kernel(x, c, ctx, c_ctx, w_mod, b_mod, w_in, b_gate, q_norm, w_uq, kv_norm, w_ukv, w_branch_mla,
           s5_a_re, s5_a_im, s5_log_dt, s5_b_re, s5_b_im, s5_c_re, s5_c_im, s5_d, w_glu, b_glu,
           w_branch_s5, w_out, ln_mix_g, ln_mix_b, ln_ffn_g, ln_ffn_b, ffn_w1, ffn_w3, ffn_w2,
           moe_w_router, moe_b_router, moe_w1, moe_w3, moe_w2):
    B, N, D = x.shape
    C = ctx.shape[1]
    depth = w_mod.shape[0]
    QL, KL = q_norm.shape[1], kv_norm.shape[1]
    SW = s5_d.shape[1]
    H = N_HEADS
    NX, NC_ROWS = B * N, B * C
    T = NX + NC_ROWS
    alpha = (2 * depth) ** 0.25
    q_scale = (QK_NOPE + QK_ROPE) ** -0.5 * math.log2(math.e)
    o_ckv, o_kr, o_u, o_g = QL, QL + KL, QL + KL + QK_ROPE, QL + KL + QK_ROPE + SW
    assert N % C == 0 and N % 256 == 0 and NC_ROWS % 256 == 0

    tm_all = _pick(T, (1536, 1024, 768, 512, 384, 256, 128))
    tm_x = _pick(NX, (1024, 512, 256, 128))
    tile_n = lambda n: _pick(n, (512, 256, 128))

    n_cond = B + 1
    cond = jnp.concatenate([c, c_ctx[None], jnp.zeros((-n_cond % 8, D), F32)], axis=0)
    mods = mod_vectors(cond, w_mod, b_mod)
    mods = mods.reshape(depth, cond.shape[0], 6, D)

    def mvec(l, k):
        return mods[l, :n_cond, k][:, None, :]

    rope_cos, rope_up, rope_dn = _rope_tables(B, N, NC_ROWS)
    rope_cos_t, rope_up_t, rope_dn_t = rope_cos.T, rope_up.T, rope_dn.T
    xt =jnp.concatenate([x.reshape(NX, D), ctx.reshape(NC_ROWS, D)], axis=0)
    h = modulate_ln(xt, mvec(0, 0), mvec(0, 1), rows_per_group=N)

    for l in range(depth):
        need_ctx = l < depth - 1
        rows = T if need_ctx else NX
        tm_r = tm_all if need_ctx else tm_x

        wi = w_in[l]
        w_cq = wi[:, :o_ckv].astype(BF16)
        w_ckv = wi[:, o_ckv:o_kr].astype(BF16)
        w_kr = jnp.concatenate(
            [jnp.zeros((D, QK_NOPE), F32), wi[:, o_kr:o_u],
             jnp.zeros((D, HEAD_PAD - QK_NOPE - QK_ROPE), F32)], axis=1).astype(BF16)
        w_u = wi[:, o_u:o_g].astype(BF16)
        w_gm = wi[:, o_g:o_g + D].astype(BF16)
        w_gs = wi[:, o_g + D:].astype(BF16)
        wq = w_uq[l].reshape(QL, H, QK_NOPE + QK_ROPE)
        w_q = _head_cat_cols(wq[:, :, :QK_NOPE].reshape(QL, -1), wq[:, :, QK_NOPE:].reshape(QL, -1),
                             QK_NOPE, QK_ROPE).astype(BF16)
        wkv = w_ukv[l].reshape(KL, H, QK_NOPE + V_HEAD)
        w_k = _head_cat_cols(wkv[:, :, :QK_NOPE].reshape(KL, -1), None, QK_NOPE, 0).astype(BF16)
        w_v = wkv[:, :, QK_NOPE:].reshape(KL, H * V_HEAD).astype(BF16)

        cqn = matmul([h], [w_cq], out_dtype=BF16, tm=tm_r // 2, tn=QL, m_rows=rows,
                     epilogue=_rms_pro, extras=((q_norm[l].reshape(1, QL), "col"),), name="in_cq")
        ckvn = matmul([h], [w_ckv], out_dtype=BF16, tm=tm_all, tn=KL,
                      epilogue=_rms_pro, extras=((kv_norm[l].reshape(1, KL), "col"),), name="in_ckv")
        krp = matmul([h], [w_kr], out_dtype=F32, tm=tm_all, tn=HEAD_PAD, name="in_kr")
        u = matmul([h], [w_u], out_dtype=F32, tm=tm_all, tn=tile_n(SW), name="in_s5")

        qt = matmul([w_q.T], [cqn], nt=True, out_dtype=BF16, tm=_pick(H * HEAD_PAD, (1024, 512, 256)), tn=512,
                    n_cols=rows, epilogue=lambda acc, cs, up, dn: _rope_apply_t(acc, cs, up, dn) * q_scale,
                    extras=((rope_cos_t, "coltab_tiled"), (rope_up_t, "coltab_tiled"), (rope_dn_t, "coltab_tiled")),
                    name="mla_qt")
        kh = matmul([ckvn], [w_k], out_dtype=BF16, tm=tm_all, tn=512,
                    epilogue=lambda acc, kr, cs, up, dn: acc + jnp.tile(_rope_apply(kr, cs, up, dn),
                                                                        (1, acc.shape[1] // HEAD_PAD)),
                    extras=((krp, "rowtab"), (rope_cos, "rowtab"), (rope_up, "rowtab"), (rope_dn, "rowtab")),
                    name="mla_k")
        vt = matmul([w_v.T], [ckvn], nt=True, out_dtype=BF16, tm=_pick(H * V_HEAD, (1024, 512, 256, 128)),
                    tn=512, name="mla_vt")
        tq = _pick(N, (512, 256, 128))
        o_x = flash_attention(qt, kh, vt, n_batch=B, q_row0=0, q_len=N, segs=[(0, N), (NX, C)],
                              tq=tq, tk=1024, name="flash_x")
        if need_ctx:
            o_c = flash_attention(qt, kh, vt, n_batch=B, q_row0=NX, q_len=C, segs=[(NX, C)],
                                  tq=_pick(C, (256, 128)), tk=512, name="flash_ctx")
            o_all = jnp.concatenate([o_x, o_c], axis=0)
        else:
            o_all = o_x

        tables = _s5_tables(s5_a_re[l], s5_a_im[l], s5_log_dt[l], s5_b_re[l], s5_b_im[l],
                            s5_c_re[l], s5_c_im[l])
        y = s5_mix(u, tables, n_batch=B, seq=N, ctx_len=C)
        d_row = s5_d[l].reshape(1, SW)
        glu_pro = lambda yv, uv, dv: _gelu_tanh(yv + dv * uv)
        ys = matmul([y, u], [w_glu[l].astype(BF16)], out_dtype=BF16, tm=tm_r // 2, tn=tile_n(SW), m_rows=rows,
                    prologue=glu_pro, pro_consts=(d_row,),
                    epilogue=lambda acc, yv, uv, dv, bv: (lambda gg: gg * jax.nn.sigmoid(acc + bv))(
                        _gelu_tanh(yv + dv * uv)),
                    extras=((y, "tile"), (u, "tile"), (d_row, "col"), (b_glu[l].reshape(1, SW), "col")),
                    name="s5_glu")

        bg = b_gate[l]
        m1 = matmul([o_all], [w_branch_mla[l].astype(BF16)], out_dtype=F32, tm=tm_r, tn=512, m_rows=rows,
                    name="branch_mla")
        gm = matmul([h], [w_gm], out_dtype=F32, tm=tm_r, tn=512, m_rows=rows,
                    epilogue=lambda acc, bv, mv: jax.nn.sigmoid(acc + bv) * mv,
                    extras=((bg[:D].reshape(1, D), "col"), (m1, "tile")), name="gate_mla")
        m2 = matmul([ys], [w_branch_s5[l].astype(BF16)], out_dtype=F32, tm=tm_r, tn=512, m_rows=rows,
                    name="branch_s5")
        merged = matmul([h], [w_gs], out_dtype=BF16, tm=tm_r, tn=512, m_rows=rows,
                        epilogue=lambda acc, bv, mv, pv: jax.nn.sigmoid(acc + bv) * mv + pv,
                        extras=((bg[D:].reshape(1, D), "col"), (m2, "tile"), (gm, "tile")), name="gate_s5")
        mix = matmul([merged], [w_out[l].astype(BF16)], out_dtype=F32, tm=tm_r, tn=512, m_rows=rows,
                     name="out_proj")
        xt, h2 = residual_ln(xt, mix, mvec(l, 2), ln_mix_g[l], ln_mix_b[l], mvec(l, 3), mvec(l, 4),
                             alpha=alpha, rows_per_group=N, m_rows=rows, h_dtype=BF16 if l % 2 == 0 else F32)

        if l % 2 == 0:
            fi = l // 2
            dff = ffn_w1.shape[2]
            dff_p = -(-dff // 512) * 512
            w1 = _pad_cols(ffn_w1[fi], dff_p).astype(BF16)
            w3 = _pad_cols(ffn_w3[fi], dff_p).astype(BF16)
            w2 = jnp.pad(ffn_w2[fi], ((0, dff_p - dff), (0, 0))).astype(BF16)
            act = matmul([h2], [w1, w3], out_dtype=BF16, tm=tm_r, tn=256, m_rows=rows,
                         epilogue=lambda a, b: a * jax.nn.sigmoid(a) * b, name="ffn_up")
            tk = _pick(dff_p, (2816, 2048, 1024, 512))
            ff = matmul_ksplit(act, w2, out_dtype=F32, tm=tm_r, tn=512, tk=tk, m_rows=rows, name="ffn_down")
        else:
            mi = l // 2
            n_exp = moe_w_router.shape[2]
            w_r = _pad_cols(moe_w_router[mi], LANE).astype(BF16)
            b_r = jnp.pad(moe_b_router[mi], (0, LANE - n_exp)).reshape(1, LANE)
            logits = matmul([h2], [w_r], out_dtype=F32, tm=tm_r // 2, tn=LANE, m_rows=rows,
                            prologue=lambda a: a, name="router_logits")
            meta, counts = router_top2(logits, b_r, n_exp=n_exp)
            ff = sparse_moe(h2, meta, counts[0, :n_exp], moe_w1[mi].astype(BF16), moe_w3[mi].astype(BF16),
                            moe_w2[mi].astype(BF16))
        if need_ctx:
            xt, h = residual_ln(xt, ff, mvec(l, 5), ln_ffn_g[l], ln_ffn_b[l], mvec(l + 1, 0), mvec(l + 1, 1),
                                alpha=alpha, rows_per_group=N, m_rows=rows)
        else:
            xt, _ = residual_ln(xt, ff, mvec(l, 5), ln_ffn_g[l], ln_ffn_b[l], None, None,
                                alpha=alpha, rows_per_group=N, m_rows=rows)
    return xt[:NX].reshape(B, N, D)
```

```python
import functools
import math

import jax
import jax.numpy as jnp
from jax import lax
from jax.experimental import pallas as pl
from jax.experimental.pallas import tpu as pltpu

N_HEADS = 32
QK_NOPE = 128
QK_ROPE = 64
V_HEAD = 128
ROPE_THETA = 10000.0
GRID_W = 64
S5_GROUP = 16
S5_STATE = 64
TOP_K = 2
LN_EPS = 1e-6
RMS_EPS = 1e-6

HEAD_PAD = 256
S5_L = 16
FLASH_ONES = 16
LANE = 128
S5_SG = LANE // S5_GROUP
VMEM_LIMIT_BYTES = 56 * 2**20

F32 = jnp.float32
BF16 = jnp.bfloat16


def _cparams(sem):
    return pltpu.CompilerParams(dimension_semantics=sem, vmem_limit_bytes=VMEM_LIMIT_BYTES)


def _pick(n, prefs):
    for p in prefs:
        if n % p == 0:
            return p
    raise ValueError(f"no tile in {prefs} divides {n}")


def _mm_body(*refs, n_a, n_pc, n_b, kinds, prologue, epilogue, tm, tn, nt):
    a_refs = refs[:n_a]
    pc_refs = refs[n_a:n_a + n_pc]
    b_refs = refs[n_a + n_pc:n_a + n_pc + n_b]
    ex_refs = refs[n_a + n_pc + n_b:n_a + n_pc + n_b + len(kinds)]
    o_ref = refs[n_a + n_pc + n_b + len(kinds)]
    if prologue is not None:
        a_s = refs[n_a + n_pc + n_b + len(kinds) + 1]

        @pl.when(pl.program_id(1) == 0)
        def _():
            a_s[...] = prologue(*[r[...] for r in a_refs], *[r[...] for r in pc_refs]).astype(BF16)

        a = a_s[...]
    else:
        a = a_refs[0][...]
    if nt:
        accs = [lax.dot_general(a, b[...], (((1,), (1,)), ((), ())), preferred_element_type=F32)
                for b in b_refs]
    else:
        accs = [jnp.dot(a, b[...].astype(BF16), preferred_element_type=F32) for b in b_refs]
    exs = []
    for r, kind in zip(ex_refs, kinds):
        v = r[...]
        if kind == "rowtab_tiled":
            v = jnp.tile(v, (1, tn // v.shape[1]))
        elif kind == "coltab_tiled":
            v = jnp.tile(v, (tm // v.shape[0], 1))
        exs.append(v)
    o_ref[...] = epilogue(*accs, *exs).astype(o_ref.dtype)


def matmul(a_list, b_list, *, out_dtype, tm, tn, m_rows=None, n_cols=None, nt=False, epilogue=None,
           extras=(), prologue=None, pro_consts=(), name="mm"):
    K = a_list[0].shape[1]
    N = b_list[0].shape[0 if nt else 1] if n_cols is None else n_cols
    M = a_list[0].shape[0] if m_rows is None else m_rows
    tn = _pick(N, tuple(t for t in (tn, 512, 256, 128) if t <= tn))
    assert M % tm == 0 and N % tn == 0, (M, tm, N, tn)
    if epilogue is None:
        epilogue = lambda acc: acc
    if prologue is None:
        assert len(a_list) == 1 and a_list[0].dtype == BF16
    in_specs = [pl.BlockSpec((tm, K), lambda i, j: (i, 0)) for _ in a_list]
    in_specs += [pl.BlockSpec(c.shape, lambda i, j: (0, 0)) for c in pro_consts]
    if nt:
        in_specs += [pl.BlockSpec((tn, K), lambda i, j: (j, 0)) for _ in b_list]
    else:
        in_specs += [pl.BlockSpec((K, tn), lambda i, j: (0, j)) for _ in b_list]
    kinds = []
    ex_arrays = []
    for arr, kind in extras:
        kinds.append(kind)
        ex_arrays.append(arr)
        if kind == "tile":
            in_specs.append(pl.BlockSpec((tm, tn), lambda i, j: (i, j)))
        elif kind == "col":
            in_specs.append(pl.BlockSpec((1, tn), lambda i, j: (0, j)))
        elif kind in ("rowtab", "rowtab_tiled"):
            in_specs.append(pl.BlockSpec((tm, arr.shape[1]), lambda i, j: (i, 0)))
        elif kind == "coltab_tiled":
            in_specs.append(pl.BlockSpec((arr.shape[0], tn), lambda i, j: (0, j)))
        else:
            raise ValueError(kind)
    scratch = [pltpu.VMEM((tm, K), BF16)] if prologue is not None else []
    body = functools.partial(_mm_body, n_a=len(a_list), n_pc=len(pro_consts), n_b=len(b_list),
                             kinds=tuple(kinds), prologue=prologue, epilogue=epilogue, tm=tm, tn=tn, nt=nt)
    return pl.pallas_call(
        body,
        grid=(M // tm, N // tn),
        in_specs=in_specs,
        out_specs=pl.BlockSpec((tm, tn), lambda i, j: (i, j)),
        out_shape=jax.ShapeDtypeStruct((M, N), out_dtype),
        scratch_shapes=scratch,
        compiler_params=_cparams(("parallel", "arbitrary")),
        name=name,
    )(*a_list, *pro_consts, *b_list, *ex_arrays)


def _mmk_body(a_ref, b_ref, *rest, n_ex, epilogue, nk):
    ex_refs = rest[:n_ex]
    o_ref = rest[n_ex]
    acc_ref = rest[n_ex + 1]
    k = pl.program_id(2)

    @pl.when(k == 0)
    def _():
        acc_ref[...] = jnp.zeros_like(acc_ref)

    acc_ref[...] += jnp.dot(a_ref[...], b_ref[...], preferred_element_type=F32)

    @pl.when(k == nk - 1)
    def _():
        o_ref[...] = epilogue(acc_ref[...], *[e[...] for e in ex_refs]).astype(o_ref.dtype)


def matmul_ksplit(a, b, *, out_dtype, tm, tn, tk, m_rows=None, epilogue=None, extras=(), name="mmk"):
    K = a.shape[1]
    N = b.shape[1]
    M = a.shape[0] if m_rows is None else m_rows
    tn = _pick(N, tuple(t for t in (tn, 512, 256, 128) if t <= tn))
    assert M % tm == 0 and N % tn == 0 and K % tk == 0, (M, tm, N, tn, K, tk)
    if epilogue is None:
        epilogue = lambda acc: acc
    in_specs = [pl.BlockSpec((tm, tk), lambda i, j, k: (i, k)),
                pl.BlockSpec((tk, tn), lambda i, j, k: (k, j))]
    ex_arrays = []
    for arr, kind in extras:
        ex_arrays.append(arr)
        if kind == "tile":
            in_specs.append(pl.BlockSpec((tm, tn), lambda i, j, k: (i, j)))
        elif kind == "rowtab":
            in_specs.append(pl.BlockSpec((tm, arr.shape[1]), lambda i, j, k: (i, 0)))
        else:
            raise ValueError(kind)
    nk = K // tk
    body = functools.partial(_mmk_body, n_ex=len(ex_arrays), epilogue=epilogue, nk=nk)
    return pl.pallas_call(
        body,
        grid=(M // tm, N // tn, nk),
        in_specs=in_specs,
        out_specs=pl.BlockSpec((tm, tn), lambda i, j, k: (i, j)),
        out_shape=jax.ShapeDtypeStruct((M, N), out_dtype),
        scratch_shapes=[pltpu.VMEM((tm, tn), F32)],
        compiler_params=_cparams(("parallel", "parallel", "arbitrary")),
        name=name,
    )(a, b, *ex_arrays)


def _mod_body(c_ref, w_ref, b_ref, o_ref):
    c = c_ref[...]
    act = (c * jax.nn.sigmoid(c)).astype(BF16)
    o_ref[...] = jnp.dot(act, w_ref[...].astype(BF16), preferred_element_type=F32) + b_ref[...]


def mod_vectors(cond, w_mod, b_mod):
    depth, d, n = w_mod.shape
    r = cond.shape[0]
    tn = _pick(n, (1024, 512, 256, 128))
    return pl.pallas_call(
        _mod_body,
        grid=(depth, n // tn),
        in_specs=[pl.BlockSpec((r, d), lambda l, j: (0, 0)),
                  pl.BlockSpec((None, d, tn), lambda l, j: (l, 0, j)),
                  pl.BlockSpec((None, 1, tn), lambda l, j: (l, 0, j))],
        out_specs=pl.BlockSpec((None, r, tn), lambda l, j: (l, 0, j)),
        out_shape=jax.ShapeDtypeStruct((depth, r, n), F32),
        compiler_params=_cparams(("parallel", "parallel")),
        name="mod_vectors",
    )(cond, w_mod, b_mod.reshape(depth, 1, n))


def _ln_rows(x):
    mu = jnp.mean(x, axis=-1, keepdims=True)
    xc = x - mu
    var = jnp.mean(xc * xc, axis=-1, keepdims=True)
    return xc * lax.rsqrt(var + LN_EPS)


def _modln_body(x_ref, sh_ref, sc_ref, h_ref):
    h_ref[...] = (_ln_rows(x_ref[...]) * (1.0 + sc_ref[...]) + sh_ref[...]).astype(h_ref.dtype)


def modulate_ln(x, shift, scale, *, rows_per_group, m_rows=None, tr=256):
    M = x.shape[0] if m_rows is None else m_rows
    d = x.shape[1]
    assert M % tr == 0 and rows_per_group % tr == 0
    gmap = lambda i: ((i * tr) // rows_per_group, 0, 0)
    return pl.pallas_call(
        _modln_body,
        grid=(M // tr,),
        in_specs=[pl.BlockSpec((tr, d), lambda i: (i, 0)),
                  pl.BlockSpec((None, 1, d), gmap),
                  pl.BlockSpec((None, 1, d), gmap)],
        out_specs=pl.BlockSpec((tr, d), lambda i: (i, 0)),
        out_shape=jax.ShapeDtypeStruct((M, d), BF16),
        compiler_params=_cparams(("parallel",)),
        name="modulate_ln",
    )(x, shift, scale)


def _resln_body(x_ref, y_ref, gate_ref, g_ref, b_ref, sh_ref, sc_ref, xo_ref, h_ref, *, alpha):
    xn = _ln_rows(alpha * x_ref[...] + gate_ref[...] * y_ref[...]) * g_ref[...] + b_ref[...]
    xo_ref[...] = xn
    h_ref[...] = (_ln_rows(xn) * (1.0 + sc_ref[...]) + sh_ref[...]).astype(h_ref.dtype)


def _resln_last_body(x_ref, y_ref, gate_ref, g_ref, b_ref, xo_ref, *, alpha):
    xo_ref[...] = _ln_rows(alpha * x_ref[...] + gate_ref[...] * y_ref[...]) * g_ref[...] + b_ref[...]


def residual_ln(x, y, gate, ln_g, ln_b, shift, scale, *, alpha, rows_per_group, m_rows=None, tr=256,
                h_dtype=None):
    M = x.shape[0] if m_rows is None else m_rows
    d = x.shape[1]
    assert M % tr == 0 and rows_per_group % tr == 0
    gmap = lambda i: ((i * tr) // rows_per_group, 0, 0)
    row = pl.BlockSpec((tr, d), lambda i: (i, 0))
    vec = pl.BlockSpec((1, d), lambda i: (0, 0))
    gvec = pl.BlockSpec((None, 1, d), gmap)
    if shift is None:
        return pl.pallas_call(
            functools.partial(_resln_last_body, alpha=alpha),
            grid=(M // tr,),
            in_specs=[row, row, gvec, vec, vec],
            out_specs=row,
            out_shape=jax.ShapeDtypeStruct((M, d), F32),
            compiler_params=_cparams(("parallel",)),
            name="residual_ln_last",
        )(x, y, gate, ln_g.reshape(1, d), ln_b.reshape(1, d)), None
    return pl.pallas_call(
        functools.partial(_resln_body, alpha=alpha),
        grid=(M // tr,),
        in_specs=[row, row, gvec, vec, vec, gvec, gvec],
        out_specs=[row, row],
        out_shape=[jax.ShapeDtypeStruct((M, d), F32), jax.ShapeDtypeStruct((M, d), h_dtype or BF16)],
        compiler_params=_cparams(("parallel",)),
        name="residual_ln",
    )(x, y, gate, ln_g.reshape(1, d), ln_b.reshape(1, d), shift, scale)


def _flash_body(qt_ref, *refs, seg_lens, tk):
    n_seg = len(seg_lens)
    kv_refs = refs[:2 * n_seg]
    o_ref = refs[2 * n_seg]
    acc_ref, st_a, st_b = refs[2 * n_seg + 1:]
    bufs = (st_a, st_b)
    qt = qt_ref[...]
    tq = qt.shape[1]
    acc_ref[...] = jnp.zeros_like(acc_ref)
    m = jnp.full((1, tq), -jnp.inf, F32)

    def scores(seg, c, tks):
        k = kv_refs[2 * seg][pl.ds(pl.multiple_of(c * tks, tks), tks), :]
        return jnp.dot(k, qt, preferred_element_type=F32)

    def absorb(st, seg, c, tks, m_old):
        vt = kv_refs[2 * seg + 1][:, pl.ds(pl.multiple_of(c * tks, tks), tks)]
        vt1 = jnp.concatenate([vt, jnp.ones((FLASH_ONES, tks), BF16)], axis=0)
        m_new = jnp.maximum(m_old, jnp.max(st, axis=0, keepdims=True))
        p = jnp.exp2(st - m_new).astype(BF16)
        corr = jnp.exp2(m_old - m_new)
        acc_ref[...] = corr * acc_ref[...] + jnp.dot(vt1, p, preferred_element_type=F32)
        return m_new

    def run_static(chunks, cur, m):
        for i, (seg, c, tks) in enumerate(chunks):
            if i + 1 < len(chunks):
                nseg, nc, ntks = chunks[i + 1]
                bufs[1 - cur][:ntks] = scores(nseg, nc, ntks)
            m = absorb(bufs[cur][:tks], seg, c, tks, m)
            cur = 1 - cur
        return m

    chunk_counts = [(s, min(tk, ln), ln // min(tk, ln)) for s, ln in enumerate(seg_lens)]
    s0, tk0, n0 = chunk_counts[0]
    rest = [(s, c, tks) for s, tks, n in chunk_counts[1:] for c in range(n)]
    if n0 >= 4 and n0 % 2 == 0:
        st_a[...] = scores(s0, 0, tk0)

        def pair(j, m):
            c0 = 2 * j
            st_b[...] = scores(s0, c0 + 1, tk0)
            m = absorb(st_a[...], s0, c0, tk0, m)
            st_a[...] = scores(s0, c0 + 2, tk0)
            return absorb(st_b[...], s0, c0 + 1, tk0, m)

        m = lax.fori_loop(0, n0 // 2 - 1, pair, m)
        m = run_static([(s0, n0 - 2, tk0), (s0, n0 - 1, tk0)] + rest, 0, m)
    else:
        chunks = [(s0, c, tk0) for c in range(n0)] + rest
        st_a[:tk0] = scores(s0, 0, tk0)
        m = run_static(chunks, 0, m)
    acc = acc_ref[...]
    o_ref[...] = (acc[:V_HEAD] / acc[V_HEAD:V_HEAD + 1]).T.astype(o_ref.dtype)


def flash_attention(qt, k, vt, *, n_batch, q_row0, q_len, segs, tq, tk, name="flash"):
    h = N_HEADS
    nq = q_len // tq
    assert q_len % tq == 0 and q_row0 % tq == 0
    in_specs = [pl.BlockSpec((HEAD_PAD, tq), lambda b, hh, i: (hh, q_row0 // tq + b * nq + i))]
    args = [qt]
    for row0, ln in segs:
        assert row0 % ln == 0
        in_specs.append(pl.BlockSpec((ln, HEAD_PAD), lambda b, hh, i, r=row0 // ln: (r + b, hh)))
        in_specs.append(pl.BlockSpec((V_HEAD, ln), lambda b, hh, i, r=row0 // ln: (hh, r + b)))
        args += [k, vt]
    body = functools.partial(_flash_body, seg_lens=tuple(ln for _, ln in segs), tk=tk)
    return pl.pallas_call(
        body,
        grid=(n_batch, h, nq),
        in_specs=in_specs,
        out_specs=pl.BlockSpec((tq, V_HEAD), lambda b, hh, i: (b * nq + i, hh)),
        out_shape=jax.ShapeDtypeStruct((n_batch * q_len, h * V_HEAD), BF16),
        scratch_shapes=[pltpu.VMEM((V_HEAD + FLASH_ONES, tq), F32),
                        pltpu.VMEM((tk, tq), F32), pltpu.VMEM((tk, tq), F32)],
        compiler_params=_cparams(("parallel", "parallel", "arbitrary")),
        name=name,
    )(*args)


def _chunk_rows(x_ref):
    return jnp.concatenate([x_ref[:, t, :] for t in range(S5_L)], axis=1).astype(BF16)


def _s5_drive_body(x_ref, w_ref, o_ref):
    res = jnp.dot(_chunk_rows(x_ref), w_ref[...], preferred_element_type=F32)
    for gl in range(S5_SG):
        o_ref[:, gl, :] = res[:, gl * 4 * S5_STATE:(gl + 1) * 4 * S5_STATE]


def _s5_out_body(x_ref, sf_ref, sb_ref, t_ref, q_ref, o_ref):
    s = jnp.concatenate([sf_ref[:, gl, :] for gl in range(S5_SG)]
                        + [sb_ref[:, gl, :] for gl in range(S5_SG)], axis=1).astype(BF16)
    res = (jnp.dot(_chunk_rows(x_ref), t_ref[...], preferred_element_type=F32)
           + jnp.dot(s, q_ref[...], preferred_element_type=F32))
    for t in range(S5_L):
        o_ref[:, t, :] = res[:, t * LANE:(t + 1) * LANE]


def _s5_scan_body(wf_ref, wb_ref, af_ref, bf_ref, ab_ref, bb_ref, sf_ref, sb_ref, st_f, st_b):
    @pl.when(pl.program_id(1) == 0)
    def _():
        st_f[...] = jnp.zeros_like(st_f)
        st_b[...] = jnp.zeros_like(st_b)

    a_f, b_f, a_b, b_b = af_ref[...], bf_ref[...], ab_ref[...], bb_ref[...]
    cb = wf_ref.shape[0]

    def step(c, carry):
        s_f, s_b = carry
        cr = cb - 1 - c
        sf_ref[c] = s_f
        sb_ref[cr] = s_b
        n_f = a_f * s_f + b_f * pltpu.roll(s_f, S5_STATE, 1) + wf_ref[c]
        n_b = a_b * s_b + b_b * pltpu.roll(s_b, S5_STATE, 1) + wb_ref[cr]
        return n_f, n_b

    s_f, s_b = lax.fori_loop(0, cb, step, (st_f[...], st_b[...]))
    st_f[...] = s_f
    st_b[...] = s_b


def s5_scan(w3, lam, *, n_batch, n_xc, n_cc, cb):
    nch, g, _ = w3.shape
    p2 = 2 * S5_STATE
    n_xb, n_cb = n_xc // cb, n_cc // cb
    ctx0 = n_batch * n_xb

    def fwd_blk(b, j):
        return jnp.where(j < n_cb, ctx0 + b * n_cb + j, b * n_xb + j - n_cb)

    def bwd_blk(b, j):
        return jnp.where(j < n_cb, ctx0 + b * n_cb + (n_cb - 1 - j), b * n_xb + (n_xb - 1 - (j - n_cb)))

    coef = pl.BlockSpec((g, p2), lambda b, j: (0, 0))
    return pl.pallas_call(
        _s5_scan_body,
        grid=(n_batch, n_xb + n_cb),
        in_specs=[pl.BlockSpec((cb, g, p2), lambda b, j: (fwd_blk(b, j), 0, 0)),
                  pl.BlockSpec((cb, g, p2), lambda b, j: (bwd_blk(b, j), 0, 1)),
                  coef, coef, coef, coef],
        out_specs=[pl.BlockSpec((cb, g, p2), lambda b, j: (fwd_blk(b, j), 0, 0)),
                   pl.BlockSpec((cb, g, p2), lambda b, j: (bwd_blk(b, j), 0, 0))],
        out_shape=[jax.ShapeDtypeStruct((nch, g, p2), F32)] * 2,
        scratch_shapes=[pltpu.VMEM((g, p2), F32), pltpu.VMEM((g, p2), F32)],
        compiler_params=_cparams(("arbitrary", "arbitrary")),
        name="s5_scan",
    )(w3, w3, *lam)


def _s5_tables(a_re, a_im, log_dt, b_re, b_im, c_re, c_im):
    L, P, Hh = S5_L, S5_STATE, S5_GROUP
    hp = lax.Precision.HIGHEST
    dt = jnp.exp(log_dt.astype(F32))[..., None]
    ar, ai = a_re.astype(F32), a_im.astype(F32)
    j = jnp.arange(L + 1, dtype=F32)[:, None, None, None]
    mag = jnp.exp(j * ar * dt)
    pr, pi = mag * jnp.cos(j * ai * dt), mag * jnp.sin(j * ai * dt)
    lr, li = pr[1], pi[1]
    nr = lr - 1.0
    den = ar * ar + ai * ai
    f_re = ((nr * ar + li * ai) / den)[..., None]
    f_im = ((li * ar - nr * ai) / den)[..., None]
    br, bi = b_re.astype(F32), b_im.astype(F32)
    bb_re = f_re * br - f_im * bi
    bb_im = f_re * bi + f_im * br
    cr, ci = c_re.astype(F32), c_im.astype(F32)

    zr = pr[:L, ..., None] * bb_re - pi[:L, ..., None] * bb_im
    zi = pr[:L, ..., None] * bb_im + pi[:L, ..., None] * bb_re
    kj = (jnp.einsum('dghp,jdgpk->jdghk', cr, zr, precision=hp)
          - jnp.einsum('dghp,jdgpk->jdghk', ci, zi, precision=hp))
    t_idx = jnp.arange(L)
    lag = t_idx[None, :] - t_idx[:, None]
    kf = kj[:, 0][jnp.clip(lag, 0, L - 1)]
    kb = kj[:, 1][jnp.clip(-lag, 0, L - 1)]
    tm4 = (jnp.where((lag >= 0)[:, :, None, None, None], kf, 0.0)
           + jnp.where((lag <= 0)[:, :, None, None, None], kb, 0.0))
    g = tm4.shape[2]
    tmat = jnp.transpose(tm4, (2, 0, 4, 1, 3)).reshape(g, L * Hh, L * Hh)

    kk = jnp.arange(L)
    pf_r, pf_i = pr[L - 1 - kk, 0], pi[L - 1 - kk, 0]
    pb_r, pb_i = pr[kk, 1], pi[kk, 1]
    wf_re = pf_r[..., None] * bb_re[0] - pf_i[..., None] * bb_im[0]
    wf_im = pf_r[..., None] * bb_im[0] + pf_i[..., None] * bb_re[0]
    wb_re = pb_r[..., None] * bb_re[1] - pb_i[..., None] * bb_im[1]
    wb_im = pb_r[..., None] * bb_im[1] + pb_i[..., None] * bb_re[1]
    wcat = jnp.concatenate([wf_re, wf_im, wb_re, wb_im], axis=2)
    wmat = jnp.transpose(wcat, (1, 0, 3, 2)).reshape(g, L * Hh, 4 * P)

    qf_r, qf_i = pr[kk + 1, 0], pi[kk + 1, 0]
    qb_r, qb_i = pr[L - kk, 1], pi[L - kk, 1]

    def qpair(c_r, c_i, q_r, q_i):
        return (c_r[None] * q_r[:, :, None, :] - c_i[None] * q_i[:, :, None, :],
                -c_r[None] * q_i[:, :, None, :] - c_i[None] * q_r[:, :, None, :])

    qf_re, qf_im = qpair(cr[0], ci[0], qf_r, qf_i)
    qb_re, qb_im = qpair(cr[1], ci[1], qb_r, qb_i)
    qf = jnp.transpose(jnp.concatenate([qf_re, qf_im], axis=3), (1, 3, 0, 2))
    qb = jnp.transpose(jnp.concatenate([qb_re, qb_im], axis=3), (1, 3, 0, 2))

    sg_n, nsg = S5_SG, g // S5_SG
    eye = jnp.eye(sg_n, dtype=F32)
    n_in = L * sg_n * Hh
    w6 = jnp.transpose(wmat.reshape(nsg, sg_n, L, Hh, 4 * P), (0, 2, 1, 3, 4))
    wfull = (w6[:, :, :, :, None, :] * eye[None, None, :, None, :, None]
             ).reshape(nsg, n_in, sg_n * 4 * P)
    t7 = jnp.transpose(tmat.reshape(nsg, sg_n, L, Hh, L, Hh), (0, 2, 1, 3, 4, 5))
    tfull = (t7[:, :, :, :, :, None, :] * eye[None, None, :, None, None, :, None]
             ).reshape(nsg, n_in, n_in)

    def q_spread(q):
        q6 = q.reshape(nsg, sg_n, 2 * P, L, Hh)
        return (q6[:, :, :, :, None, :] * eye[None, :, None, None, :, None]).reshape(nsg, sg_n * 2 * P, n_in)

    qfull = jnp.concatenate([q_spread(qf), q_spread(qb)], axis=1)

    def lam_tiles(d):
        return (jnp.concatenate([pr[L, d], pr[L, d]], axis=-1), jnp.concatenate([-pi[L, d], pi[L, d]], axis=-1))

    lam = lam_tiles(0) + lam_tiles(1)
    return wfull.astype(BF16), tfull.astype(BF16), qfull.astype(BF16), lam


def s5_mix(u, tables, *, n_batch, seq, ctx_len):
    wfull, tfull, qfull, lam = tables
    L, P = S5_L, S5_STATE
    t_rows, w_tot = u.shape
    g = w_tot // S5_GROUP
    nsg = g // S5_SG
    nch = t_rows // L
    n_in = L * LANE
    rb = max(r for r in range(8, 265, 8) if nch % r == 0)
    u3 = u.reshape(nch, L, w_tot)
    xspec = pl.BlockSpec((rb, L, LANE), lambda s, i: (i, 0, s))
    wspec = lambda k, n: pl.BlockSpec((None, k, n), lambda s, i: (s, 0, 0))

    w3 = pl.pallas_call(
        _s5_drive_body,
        grid=(nsg, nch // rb),
        in_specs=[xspec, wspec(n_in, S5_SG * 4 * P)],
        out_specs=pl.BlockSpec((rb, S5_SG, 4 * P), lambda s, i: (i, s, 0)),
        out_shape=jax.ShapeDtypeStruct((nch, g, 4 * P), F32),
        compiler_params=_cparams(("parallel", "parallel")),
        name="s5_drive",
    )(u3, wfull)

    n_xc, n_cc = seq // L, ctx_len // L
    cb = _pick(math.gcd(n_xc, n_cc), (16, 8, 4, 2, 1))
    sf, sb = s5_scan(w3, lam, n_batch=n_batch, n_xc=n_xc, n_cc=n_cc, cb=cb)

    sspec = pl.BlockSpec((rb, S5_SG, 2 * P), lambda s, i: (i, s, 0))
    y3 = pl.pallas_call(
        _s5_out_body,
        grid=(nsg, nch // rb),
        in_specs=[xspec, sspec, sspec, wspec(n_in, n_in), wspec(S5_SG * 4 * P, n_in)],
        out_specs=xspec,
        out_shape=jax.ShapeDtypeStruct((nch, L, w_tot), F32),
        compiler_params=_cparams(("parallel", "parallel")),
        name="s5_out",
    )(u3, sf, sb, tfull, qfull)
    return y3.reshape(t_rows, w_tot)


R_E1, R_E2, R_W1, R_W2, R_RANK1, R_RANK2 = range(6)


def _router_body(lg_ref, b_ref, meta_ref, cnt_ref, carry_ref, *, n_exp):
    @pl.when(pl.program_id(0) == 0)
    def _():
        carry_ref[...] = jnp.zeros_like(carry_ref)

    lg = lg_ref[...] + b_ref[...]
    tr = lg.shape[0]
    lane = lax.broadcasted_iota(jnp.int32, lg.shape, 1).astype(F32)
    neg = jnp.float32(-jnp.inf)
    lg = jnp.where(lane < n_exp, lg, neg)
    m1 = jnp.max(lg, axis=-1, keepdims=True)
    i1 = jnp.min(jnp.where(lg == m1, lane, float(LANE)), axis=-1, keepdims=True)
    lg2 = jnp.where(lane == i1, neg, lg)
    m2 = jnp.max(lg2, axis=-1, keepdims=True)
    i2 = jnp.min(jnp.where(lg2 == m2, lane, float(LANE)), axis=-1, keepdims=True)
    e2 = jnp.exp(m2 - m1)
    den = 1.0 + e2
    sel = jnp.where((lane == i1) | (lane == i2), 1.0, 0.0)
    r_i = lax.broadcasted_iota(jnp.int32, (tr, tr), 0)
    c_i = lax.broadcasted_iota(jnp.int32, (tr, tr), 1)
    tri = jnp.where(r_i > c_i, 1.0, 0.0).astype(BF16)
    before = jnp.dot(tri, sel.astype(BF16), preferred_element_type=F32) + carry_ref[...]
    rank1 = jnp.sum(jnp.where(lane == i1, before, 0.0), axis=-1, keepdims=True)
    rank2 = jnp.sum(jnp.where(lane == i2, before, 0.0), axis=-1, keepdims=True)
    total = carry_ref[...] + jnp.sum(sel, axis=0, keepdims=True)
    carry_ref[...] = total
    cnt_ref[...] = jnp.broadcast_to(total, cnt_ref.shape)
    meta = jnp.zeros_like(lg)
    for idx, val in ((R_E1, i1), (R_E2, i2), (R_W1, 1.0 / den), (R_W2, e2 / den),
                     (R_RANK1, rank1), (R_RANK2, rank2)):
        meta = jnp.where(lane == idx, val, meta)
    meta_ref[...] = meta


def router_top2(logits, b_router_pad, *, n_exp, tr=512):
    m = logits.shape[0]
    tr = _pick(m, (tr, 256, 128, 64, 32, 16, 8))
    return pl.pallas_call(
        functools.partial(_router_body, n_exp=n_exp),
        grid=(m // tr,),
        in_specs=[pl.BlockSpec((tr, LANE), lambda i: (i, 0)), pl.BlockSpec((1, LANE), lambda i: (0, 0))],
        out_specs=[pl.BlockSpec((tr, LANE), lambda i: (i, 0)), pl.BlockSpec((8, LANE), lambda i: (0, 0))],
        out_shape=[jax.ShapeDtypeStruct((m, LANE), F32), jax.ShapeDtypeStruct((8, LANE), F32)],
        scratch_shapes=[pltpu.VMEM((1, LANE), F32)],
        compiler_params=_cparams(("arbitrary",)),
        name="router_top2",
    )(logits, b_router_pad)


def _row_copy_wait(src_ref, dst_ref, sem):
    pltpu.make_async_copy(src_ref.at[pl.ds(0, 1), :], dst_ref.at[pl.ds(0, 1), :], sem).wait()


def _gather_rows_body(idx_ref, src_ref, o_ref, buf, sem, *, tb):
    def issue(t, carry):
        pltpu.make_async_copy(src_ref.at[pl.ds(idx_ref[0, t], 1), :], buf.at[pl.ds(t, 1), :], sem).start()
        return carry

    lax.fori_loop(0, tb, issue, 0)

    def drain(t, carry):
        _row_copy_wait(src_ref, buf, sem)
        return carry

    lax.fori_loop(0, tb, drain, 0)
    o_ref[...] = buf[...]


def gather_rows(src, idx, *, tb=256):
    r, d = idx.shape[0], src.shape[1]
    tb = _pick(r, (tb, 128, 64, 32, 16, 8))
    return pl.pallas_call(
        functools.partial(_gather_rows_body, tb=tb),
        grid=(r // tb,),
        in_specs=[pl.BlockSpec((None, 1, tb), lambda i: (i, 0, 0), memory_space=pltpu.SMEM),
                  pl.BlockSpec(memory_space=pl.ANY)],
        out_specs=pl.BlockSpec((tb, d), lambda i: (i, 0)),
        out_shape=jax.ShapeDtypeStruct((r, d), src.dtype),
        scratch_shapes=[pltpu.VMEM((tb, d), src.dtype), pltpu.SemaphoreType.DMA(())],
        compiler_params=_cparams(("arbitrary",)),
        name="moe_gather_rows",
    )(idx.reshape(r // tb, 1, tb), src)


def _gather_combine_body(pos_ref, meta_ref, y_ref, o_ref, buf, sem, *, tb):
    def issue(t, carry):
        for k in range(TOP_K):
            pltpu.make_async_copy(y_ref.at[pl.ds(pos_ref[0, TOP_K * t + k], 1), :],
                                  buf.at[k, pl.ds(t, 1), :], sem).start()
        return carry

    lax.fori_loop(0, tb, issue, 0)

    def drain(t, carry):
        for k in range(TOP_K):
            _row_copy_wait(y_ref, buf.at[0], sem)
        return carry

    lax.fori_loop(0, tb, drain, 0)
    meta = meta_ref[...]
    o_ref[...] = meta[:, R_W1:R_W1 + 1] * buf[0] + meta[:, R_W2:R_W2 + 1] * buf[1]


def gather_combine(y, pos, meta, *, tb=256):
    m, d = pos.shape[0], y.shape[1]
    tb = _pick(m, (tb, 128, 64, 32, 16, 8))
    pos3 = pos.reshape(m // tb, 1, TOP_K * tb)
    return pl.pallas_call(
        functools.partial(_gather_combine_body, tb=tb),
        grid=(m // tb,),
        in_specs=[pl.BlockSpec((None, 1, TOP_K * tb), lambda i: (i, 0, 0), memory_space=pltpu.SMEM),
                  pl.BlockSpec((tb, LANE), lambda i: (i, 0)),
                  pl.BlockSpec(memory_space=pl.ANY)],
        out_specs=pl.BlockSpec((tb, d), lambda i: (i, 0)),
        out_shape=jax.ShapeDtypeStruct((m, d), F32),
        scratch_shapes=[pltpu.VMEM((TOP_K, tb, d), F32), pltpu.SemaphoreType.DMA(())],
        compiler_params=_cparams(("arbitrary",)),
        name="moe_gather_combine",
    )(pos3, meta, y)


def _gmm_up_body(te_ref, nv_ref, a_ref, b1_ref, b3_ref, o_ref, a_s):
    del te_ref
    live = pl.program_id(0) < nv_ref[0]
    first_col = pl.program_id(1) == 0

    @pl.when(live)
    def _():
        @pl.when(first_col)
        def _():
            a_s[...] = a_ref[...].astype(BF16)

        a = a_s[...]
        g = jnp.dot(a, b1_ref[...], preferred_element_type=F32)
        u = jnp.dot(a, b3_ref[...], preferred_element_type=F32)
        o_ref[...] = (g * jax.nn.sigmoid(g) * u).astype(o_ref.dtype)

    @pl.when(jnp.logical_not(live))
    def _():
        o_ref[...] = jnp.zeros_like(o_ref)


def _gmm_down_body(te_ref, nv_ref, a_ref, b_ref, o_ref):
    del te_ref
    live = pl.program_id(0) < nv_ref[0]

    @pl.when(live)
    def _():
        o_ref[...] = jnp.dot(a_ref[...], b_ref[...], preferred_element_type=F32)

    @pl.when(jnp.logical_not(live))
    def _():
        o_ref[...] = jnp.zeros_like(o_ref)


def grouped_swiglu(xs, w1, w3, w2, tile_expert, n_valid, *, tm):
    r, d = xs.shape
    f = w1.shape[2]
    n_tiles = r // tm
    tn_up = _pick(f, (512, 256, 128))
    tn_dn = _pick(d, (512, 256, 128))
    act = pl.pallas_call(
        _gmm_up_body,
        grid_spec=pltpu.PrefetchScalarGridSpec(
            num_scalar_prefetch=2,
            grid=(n_tiles, f // tn_up),
            in_specs=[pl.BlockSpec((tm, d), lambda i, j, te, nv: (i, 0)),
                      pl.BlockSpec((None, d, tn_up), lambda i, j, te, nv: (te[i], 0, j)),
                      pl.BlockSpec((None, d, tn_up), lambda i, j, te, nv: (te[i], 0, j))],
            out_specs=pl.BlockSpec((tm, tn_up), lambda i, j, te, nv: (i, j)),
            scratch_shapes=[pltpu.VMEM((tm, d), BF16)]),
        out_shape=jax.ShapeDtypeStruct((r, f), BF16),
        compiler_params=_cparams(("arbitrary", "arbitrary")),
        name="moe_up",
    )(tile_expert, n_valid, xs, w1, w3)
    return pl.pallas_call(
        _gmm_down_body,
        grid_spec=pltpu.PrefetchScalarGridSpec(
            num_scalar_prefetch=2,
            grid=(n_tiles, d // tn_dn),
            in_specs=[pl.BlockSpec((tm, f), lambda i, j, te, nv: (i, 0)),
                      pl.BlockSpec((None, f, tn_dn), lambda i, j, te, nv: (te[i], 0, j))],
            out_specs=pl.BlockSpec((tm, tn_dn), lambda i, j, te, nv: (i, j))),
        out_shape=jax.ShapeDtypeStruct((r, d), F32),
        compiler_params=_cparams(("arbitrary", "arbitrary")),
        name="moe_down",
    )(tile_expert, n_valid, act, w2)


def sparse_moe(hf, meta, counts, w1, w3, w2, *, tm=512):
    m = hf.shape[0]
    n_exp = w1.shape[0]
    tm = _pick(m, (tm, 256, 128))
    cnt = counts.astype(jnp.int32)
    padded = (cnt + tm - 1) // tm * tm
    ends = jnp.cumsum(padded)
    offs = ends - padded
    e1 = meta[:, R_E1].astype(jnp.int32)
    e2 = meta[:, R_E2].astype(jnp.int32)
    pos = jnp.stack([offs[e1] + meta[:, R_RANK1].astype(jnp.int32),
                     offs[e2] + meta[:, R_RANK2].astype(jnp.int32)], axis=1)
    n_tiles = TOP_K * m // tm + n_exp
    tile_expert = jnp.minimum(jnp.searchsorted(ends, jnp.arange(n_tiles) * tm, side="right"),
                              n_exp - 1).astype(jnp.int32)
    n_valid = (ends[-1:] // tm).astype(jnp.int32)
    src_tok = jnp.zeros((n_tiles * tm,), jnp.int32).at[pos.reshape(-1)].set(
        jnp.repeat(jnp.arange(m, dtype=jnp.int32), TOP_K))
    xs = gather_rows(hf, src_tok)
    y = grouped_swiglu(xs, w1, w3, w2, tile_expert, n_valid, tm=tm)
    return gather_combine(y, pos, meta)


def _rms_pro(x, gain):
    return x * lax.rsqrt(jnp.mean(x * x, axis=-1, keepdims=True) + RMS_EPS) * gain


def _gelu_tanh(x):
    return 0.5 * x * (1.0 + jnp.tanh(math.sqrt(2.0 / math.pi) * (x + 0.044715 * (x * x * x))))


def _rope_apply(x, cos, sin_up, sin_dn):
    n = x.shape[-1]
    return x * cos + pltpu.roll(x, n - QK_ROPE // 4, 1) * sin_up + pltpu.roll(x, QK_ROPE // 4, 1) * sin_dn


def _rope_apply_t(x, cos, sin_up, sin_dn):
    n = x.shape[0]
    return x * cos + pltpu.roll(x, n - QK_ROPE // 4, 0) * sin_up + pltpu.roll(x, QK_ROPE // 4, 0) * sin_dn


def _rope_tables(n_batch, seq, n_ctx_rows):
    nf = QK_ROPE // 4
    pos = jnp.arange(seq)
    row = (pos // GRID_W).astype(F32)
    col = (pos % GRID_W).astype(F32)
    inv = ROPE_THETA ** (-jnp.arange(nf, dtype=F32) / nf)
    ar, ac = row[:, None] * inv, col[:, None] * inv
    z = jnp.zeros((seq, nf), F32)
    cos64 = jnp.concatenate([jnp.cos(ar), jnp.cos(ar), jnp.cos(ac), jnp.cos(ac)], axis=1)
    up64 = jnp.concatenate([-jnp.sin(ar), z, -jnp.sin(ac), z], axis=1)
    dn64 = jnp.concatenate([z, jnp.sin(ar), z, jnp.sin(ac)], axis=1)

    def place(t64, fill):
        full = jnp.full((seq, HEAD_PAD), fill, F32).at[:, QK_NOPE:QK_NOPE + QK_ROPE].set(t64)
        full = jnp.tile(full, (n_batch, 1))
        return jnp.concatenate([full, jnp.full((n_ctx_rows, HEAD_PAD), fill, F32)], axis=0)

    return place(cos64, 1.0), place(up64, 0.0), place(dn64, 0.0)


def _pad_cols(w, n):
    return jnp.pad(w, ((0, 0), (0, n - w.shape[1])))


def _head_cat_cols(w_a, w_b, da, db):
    k = w_a.shape[0]
    parts = [w_a.reshape(k, N_HEADS, da)]
    if w_b is not None:
        parts.append(w_b.reshape(k, N_HEADS, db))
    used = da + (db if w_b is not None else 0)
    parts.append(jnp.zeros((k, N_HEADS, HEAD_PAD - used), w_a.dtype))
    return jnp.concatenate(parts, axis=2).reshape(k, N_HEADS * HEAD_PAD)


def kernel(x, c, ctx, c_ctx, w_mod, b_mod, w_in, b_gate, q_norm, w_uq, kv_norm, w_ukv, w_branch_mla,
           s5_a_re, s5_a_im, s5_log_dt, s5_b_re, s5_b_im, s5_c_re, s5_c_im, s5_d, w_glu, b_glu,
           w_branch_s5, w_out, ln_mix_g, ln_mix_b, ln_ffn_g, ln_ffn_b, ffn_w1, ffn_w3, ffn_w2,
           moe_w_router, moe_b_router, moe_w1, moe_w3, moe_w2):
    B, N, D = x.shape
    C = ctx.shape[1]
    depth = w_mod.shape[0]
    QL, KL = q_norm.shape[1], kv_norm.shape[1]
    SW = s5_d.shape[1]
    H = N_HEADS
    NX, NC_ROWS = B * N, B * C
    T = NX + NC_ROWS
    alpha = (2 * depth) ** 0.25
    q_scale = (QK_NOPE + QK_ROPE) ** -0.5 * math.log2(math.e)
    o_ckv, o_kr, o_u, o_g = QL, QL + KL, QL + KL + QK_ROPE, QL + KL + QK_ROPE + SW
    assert N % C == 0 and N % 256 == 0 and NC_ROWS % 256 == 0

    tm_all = _pick(T, (1536, 1024, 768, 512, 384, 256, 128))
    tm_x = _pick(NX, (1024, 512, 256, 128))
    tile_n = lambda n: _pick(n, (512, 256, 128))

    n_cond = B + 1
    cond = jnp.concatenate([c, c_ctx[None], jnp.zeros((-n_cond % 8, D), F32)], axis=0)
    mods = mod_vectors(cond, w_mod, b_mod)
    mods = mods.reshape(depth, cond.shape[0], 6, D)

    def mvec(l, k):
        return mods[l, :n_cond, k][:, None, :]

    rope_cos, rope_up, rope_dn = _rope_tables(B, N, NC_ROWS)
    rope_cos_t, rope_up_t, rope_dn_t = rope_cos.T, rope_up.T, rope_dn.T
    xt =jnp.concatenate([x.reshape(NX, D), ctx.reshape(NC_ROWS, D)], axis=0)
    h = modulate_ln(xt, mvec(0, 0), mvec(0, 1), rows_per_group=N)

    for l in range(depth):
        need_ctx = l < depth - 1
        rows = T if need_ctx else NX
        tm_r = tm_all if need_ctx else tm_x

        wi = w_in[l]
        w_cq = wi[:, :o_ckv].astype(BF16)
        w_ckv = wi[:, o_ckv:o_kr].astype(BF16)
        w_kr = jnp.concatenate(
            [jnp.zeros((D, QK_NOPE), F32), wi[:, o_kr:o_u],
             jnp.zeros((D, HEAD_PAD - QK_NOPE - QK_ROPE), F32)], axis=1).astype(BF16)
        w_u = wi[:, o_u:o_g].astype(BF16)
        w_gm = wi[:, o_g:o_g + D].astype(BF16)
        w_gs = wi[:, o_g + D:].astype(BF16)
        wq = w_uq[l].reshape(QL, H, QK_NOPE + QK_ROPE)
        w_q = _head_cat_cols(wq[:, :, :QK_NOPE].reshape(QL, -1), wq[:, :, QK_NOPE:].reshape(QL, -1),
                             QK_NOPE, QK_ROPE).astype(BF16)
        wkv = w_ukv[l].reshape(KL, H, QK_NOPE + V_HEAD)
        w_k = _head_cat_cols(wkv[:, :, :QK_NOPE].reshape(KL, -1), None, QK_NOPE, 0).astype(BF16)
        w_v = wkv[:, :, QK_NOPE:].reshape(KL, H * V_HEAD).astype(BF16)

        cqn = matmul([h], [w_cq], out_dtype=BF16, tm=tm_r // 2, tn=QL, m_rows=rows,
                     epilogue=_rms_pro, extras=((q_norm[l].reshape(1, QL), "col"),), name="in_cq")
        ckvn = matmul([h], [w_ckv], out_dtype=BF16, tm=tm_all, tn=KL,
                      epilogue=_rms_pro, extras=((kv_norm[l].reshape(1, KL), "col"),), name="in_ckv")
        krp = matmul([h], [w_kr], out_dtype=F32, tm=tm_all, tn=HEAD_PAD, name="in_kr")
        u = matmul([h], [w_u], out_dtype=F32, tm=tm_all, tn=tile_n(SW), name="in_s5")

        qt = matmul([w_q.T], [cqn], nt=True, out_dtype=BF16, tm=_pick(H * HEAD_PAD, (1024, 512, 256)), tn=512,
                    n_cols=rows, epilogue=lambda acc, cs, up, dn: _rope_apply_t(acc, cs, up, dn) * q_scale,
                    extras=((rope_cos_t, "coltab_tiled"), (rope_up_t, "coltab_tiled"), (rope_dn_t, "coltab_tiled")),
                    name="mla_qt")
        kh = matmul([ckvn], [w_k], out_dtype=BF16, tm=tm_all, tn=512,
                    epilogue=lambda acc, kr, cs, up, dn: acc + jnp.tile(_rope_apply(kr, cs, up, dn),
                                                                        (1, acc.shape[1] // HEAD_PAD)),
                    extras=((krp, "rowtab"), (rope_cos, "rowtab"), (rope_up, "rowtab"), (rope_dn, "rowtab")),
                    name="mla_k")
        vt = matmul([w_v.T], [ckvn], nt=True, out_dtype=BF16, tm=_pick(H * V_HEAD, (1024, 512, 256, 128)),
                    tn=512, name="mla_vt")
        tq = _pick(N, (512, 256, 128))
        o_x = flash_attention(qt, kh, vt, n_batch=B, q_row0=0, q_len=N, segs=[(0, N), (NX, C)],
                              tq=tq, tk=1024, name="flash_x")
        if need_ctx:
            o_c = flash_attention(qt, kh, vt, n_batch=B, q_row0=NX, q_len=C, segs=[(NX, C)],
                                  tq=_pick(C, (256, 128)), tk=512, name="flash_ctx")
            o_all = jnp.concatenate([o_x, o_c], axis=0)
        else:
            o_all = o_x

        tables = _s5_tables(s5_a_re[l], s5_a_im[l], s5_log_dt[l], s5_b_re[l], s5_b_im[l],
                            s5_c_re[l], s5_c_im[l])
        y = s5_mix(u, tables, n_batch=B, seq=N, ctx_len=C)
        d_row = s5_d[l].reshape(1, SW)
        glu_pro = lambda yv, uv, dv: _gelu_tanh(yv + dv * uv)
        ys = matmul([y, u], [w_glu[l].astype(BF16)], out_dtype=BF16, tm=tm_r // 2, tn=tile_n(SW), m_rows=rows,
                    prologue=glu_pro, pro_consts=(d_row,),
                    epilogue=lambda acc, yv, uv, dv, bv: (lambda gg: gg * jax.nn.sigmoid(acc + bv))(
                        _gelu_tanh(yv + dv * uv)),
                    extras=((y, "tile"), (u, "tile"), (d_row, "col"), (b_glu[l].reshape(1, SW), "col")),
                    name="s5_glu")

        bg = b_gate[l]
        m1 = matmul([o_all], [w_branch_mla[l].astype(BF16)], out_dtype=F32, tm=tm_r, tn=512, m_rows=rows,
                    name="branch_mla")
        gm = matmul([h], [w_gm], out_dtype=F32, tm=tm_r, tn=512, m_rows=rows,
                    epilogue=lambda acc, bv, mv: jax.nn.sigmoid(acc + bv) * mv,
                    extras=((bg[:D].reshape(1, D), "col"), (m1, "tile")), name="gate_mla")
        m2 = matmul([ys], [w_branch_s5[l].astype(BF16)], out_dtype=F32, tm=tm_r, tn=512, m_rows=rows,
                    name="branch_s5")
        merged = matmul([h], [w_gs], out_dtype=BF16, tm=tm_r, tn=512, m_rows=rows,
                        epilogue=lambda acc, bv, mv, pv: jax.nn.sigmoid(acc + bv) * mv + pv,
                        extras=((bg[D:].reshape(1, D), "col"), (m2, "tile"), (gm, "tile")), name="gate_s5")
        mix = matmul([merged], [w_out[l].astype(BF16)], out_dtype=F32, tm=tm_r, tn=512, m_rows=rows,
                     name="out_proj")
        xt, h2 = residual_ln(xt, mix, mvec(l, 2), ln_mix_g[l], ln_mix_b[l], mvec(l, 3), mvec(l, 4),
                             alpha=alpha, rows_per_group=N, m_rows=rows, h_dtype=BF16 if l % 2 == 0 else F32)

        if l % 2 == 0:
            fi = l // 2
            dff = ffn_w1.shape[2]
            dff_p = -(-dff // 512) * 512
            w1 = _pad_cols(ffn_w1[fi], dff_p).astype(BF16)
            w3 = _pad_cols(ffn_w3[fi], dff_p).astype(BF16)
            w2 = jnp.pad(ffn_w2[fi], ((0, dff_p - dff), (0, 0))).astype(BF16)
            act = matmul([h2], [w1, w3], out_dtype=BF16, tm=tm_r, tn=256, m_rows=rows,
                         epilogue=lambda a, b: a * jax.nn.sigmoid(a) * b, name="ffn_up")
            tk = _pick(dff_p, (2816, 2048, 1024, 512))
            ff = matmul_ksplit(act, w2, out_dtype=F32, tm=tm_r, tn=512, tk=tk, m_rows=rows, name="ffn_down")
        else:
            mi = l // 2
            n_exp = moe_w_router.shape[2]
            w_r = _pad_cols(moe_w_router[mi], LANE).astype(BF16)
            b_r = jnp.pad(moe_b_router[mi], (0, LANE - n_exp)).reshape(1, LANE)
            logits = matmul([h2], [w_r], out_dtype=F32, tm=tm_r // 2, tn=LANE, m_rows=rows,
                            prologue=lambda a: a, name="router_logits")
            meta, counts = router_top2(logits, b_r, n_exp=n_exp)
            ff = sparse_moe(h2, meta, counts[0, :n_exp], moe_w1[mi].astype(BF16), moe_w3[mi].astype(BF16),
                            moe_w2[mi].astype(BF16))
        if need_ctx:
            xt, h = residual_ln(xt, ff, mvec(l, 5), ln_ffn_g[l], ln_ffn_b[l], mvec(l + 1, 0), mvec(l + 1, 1),
                                alpha=alpha, rows_per_group=N, m_rows=rows)
        else:
            xt, _ = residual_ln(xt, ff, mvec(l, 5), ln_ffn_g[l], ln_ffn_b[l], None, None,
                                alpha=alpha, rows_per_group=N, m_rows=rows)
    return xt[:NX].reshape(B, N, D)
```

```python
import functools
import math

import jax
import jax.numpy as jnp
from jax import lax
from jax.experimental import pallas as pl
from jax.experimental.pallas import tpu as pltpu

N_HEADS = 32
QK_NOPE = 128
QK_ROPE = 64
V_HEAD = 128
ROPE_THETA = 10000.0
GRID_W = 64
S5_GROUP = 16
S5_STATE = 64
TOP_K = 2
LN_EPS = 1e-6
RMS_EPS = 1e-6

HEAD_PAD = 256
S5_L = 16
FLASH_ONES = 16
LANE = 128
S5_SG = LANE // S5_GROUP
VMEM_LIMIT_BYTES = 56 * 2**20

F32 = jnp.float32
BF16 = jnp.bfloat16


def _cparams(sem):
    return pltpu.CompilerParams(dimension_semantics=sem, vmem_limit_bytes=VMEM_LIMIT_BYTES)


def _pick(n, prefs):
    for p in prefs:
        if n % p == 0:
            return p
    raise ValueError(f"no tile in {prefs} divides {n}")


def _mm_body(*refs, n_a, n_pc, n_b, kinds, prologue, epilogue, tm, tn, nt):
    a_refs = refs[:n_a]
    pc_refs = refs[n_a:n_a + n_pc]
    b_refs = refs[n_a + n_pc:n_a + n_pc + n_b]
    ex_refs = refs[n_a + n_pc + n_b:n_a + n_pc + n_b + len(kinds)]
    o_ref = refs[n_a + n_pc + n_b + len(kinds)]
    if prologue is not None:
        a_s = refs[n_a + n_pc + n_b + len(kinds) + 1]

        @pl.when(pl.program_id(1) == 0)
        def _():
            a_s[...] = prologue(*[r[...] for r in a_refs], *[r[...] for r in pc_refs]).astype(BF16)

        a = a_s[...]
    else:
        a = a_refs[0][...]
    if nt:
        accs = [lax.dot_general(a, b[...], (((1,), (1,)), ((), ())), preferred_element_type=F32)
                for b in b_refs]
    else:
        accs = [jnp.dot(a, b[...].astype(BF16), preferred_element_type=F32) for b in b_refs]
    exs = []
    for r, kind in zip(ex_refs, kinds):
        v = r[...]
        if kind == "rowtab_tiled":
            v = jnp.tile(v, (1, tn // v.shape[1]))
        elif kind == "coltab_tiled":
            v = jnp.tile(v, (tm // v.shape[0], 1))
        exs.append(v)
    o_ref[...] = epilogue(*accs, *exs).astype(o_ref.dtype)


def matmul(a_list, b_list, *, out_dtype, tm, tn, m_rows=None, n_cols=None, nt=False, epilogue=None,
           extras=(), prologue=None, pro_consts=(), name="mm"):
    K = a_list[0].shape[1]
    N = b_list[0].shape[0 if nt else 1] if n_cols is None else n_cols
    M = a_list[0].shape[0] if m_rows is None else m_rows
    tn = _pick(N, tuple(t for t in (tn, 512, 256, 128) if t <= tn))
    assert M % tm == 0 and N % tn == 0, (M, tm, N, tn)
    if epilogue is None:
        epilogue = lambda acc: acc
    if prologue is None:
        assert len(a_list) == 1 and a_list[0].dtype == BF16
    in_specs = [pl.BlockSpec((tm, K), lambda i, j: (i, 0)) for _ in a_list]
    in_specs += [pl.BlockSpec(c.shape, lambda i, j: (0, 0)) for c in pro_consts]
    if nt:
        in_specs += [pl.BlockSpec((tn, K), lambda i, j: (j, 0)) for _ in b_list]
    else:
        in_specs += [pl.BlockSpec((K, tn), lambda i, j: (0, j)) for _ in b_list]
    kinds = []
    ex_arrays = []
    for arr, kind in extras:
        kinds.append(kind)
        ex_arrays.append(arr)
        if kind == "tile":
            in_specs.append(pl.BlockSpec((tm, tn), lambda i, j: (i, j)))
        elif kind == "col":
            in_specs.append(pl.BlockSpec((1, tn), lambda i, j: (0, j)))
        elif kind in ("rowtab", "rowtab_tiled"):
            in_specs.append(pl.BlockSpec((tm, arr.shape[1]), lambda i, j: (i, 0)))
        elif kind == "coltab_tiled":
            in_specs.append(pl.BlockSpec((arr.shape[0], tn), lambda i, j: (0, j)))
        else:
            raise ValueError(kind)
    scratch = [pltpu.VMEM((tm, K), BF16)] if prologue is not None else []
    body = functools.partial(_mm_body, n_a=len(a_list), n_pc=len(pro_consts), n_b=len(b_list),
                             kinds=tuple(kinds), prologue=prologue, epilogue=epilogue, tm=tm, tn=tn, nt=nt)
    return pl.pallas_call(
        body,
        grid=(M // tm, N // tn),
        in_specs=in_specs,
        out_specs=pl.BlockSpec((tm, tn), lambda i, j: (i, j)),
        out_shape=jax.ShapeDtypeStruct((M, N), out_dtype),
        scratch_shapes=scratch,
        compiler_params=_cparams(("parallel", "arbitrary")),
        name=name,
    )(*a_list, *pro_consts, *b_list, *ex_arrays)


def _mmk_body(a_ref, b_ref, *rest, n_ex, epilogue, nk):
    ex_refs = rest[:n_ex]
    o_ref = rest[n_ex]
    acc_ref = rest[n_ex + 1]
    k = pl.program_id(2)

    @pl.when(k == 0)
    def _():
        acc_ref[...] = jnp.zeros_like(acc_ref)

    acc_ref[...] += jnp.dot(a_ref[...], b_ref[...], preferred_element_type=F32)

    @pl.when(k == nk - 1)
    def _():
        o_ref[...] = epilogue(acc_ref[...], *[e[...] for e in ex_refs]).astype(o_ref.dtype)


def matmul_ksplit(a, b, *, out_dtype, tm, tn, tk, m_rows=None, epilogue=None, extras=(), name="mmk"):
    K = a.shape[1]
    N = b.shape[1]
    M = a.shape[0] if m_rows is None else m_rows
    tn = _pick(N, tuple(t for t in (tn, 512, 256, 128) if t <= tn))
    assert M % tm == 0 and N % tn == 0 and K % tk == 0, (M, tm, N, tn, K, tk)
    if epilogue is None:
        epilogue = lambda acc: acc
    in_specs = [pl.BlockSpec((tm, tk), lambda i, j, k: (i, k)),
                pl.BlockSpec((tk, tn), lambda i, j, k: (k, j))]
    ex_arrays = []
    for arr, kind in extras:
        ex_arrays.append(arr)
        if kind == "tile":
            in_specs.append(pl.BlockSpec((tm, tn), lambda i, j, k: (i, j)))
        elif kind == "rowtab":
            in_specs.append(pl.BlockSpec((tm, arr.shape[1]), lambda i, j, k: (i, 0)))
        else:
            raise ValueError(kind)
    nk = K // tk
    body = functools.partial(_mmk_body, n_ex=len(ex_arrays), epilogue=epilogue, nk=nk)
    return pl.pallas_call(
        body,
        grid=(M // tm, N // tn, nk),
        in_specs=in_specs,
        out_specs=pl.BlockSpec((tm, tn), lambda i, j, k: (i, j)),
        out_shape=jax.ShapeDtypeStruct((M, N), out_dtype),
        scratch_shapes=[pltpu.VMEM((tm, tn), F32)],
        compiler_params=_cparams(("parallel", "parallel", "arbitrary")),
        name=name,
    )(a, b, *ex_arrays)


def _mod_body(c_ref, w_ref, b_ref, o_ref):
    c = c_ref[...]
    act = (c * jax.nn.sigmoid(c)).astype(BF16)
    o_ref[...] = jnp.dot(act, w_ref[...].astype(BF16), preferred_element_type=F32) + b_ref[...]


def mod_vectors(cond, w_mod, b_mod):
    depth, d, n = w_mod.shape
    r = cond.shape[0]
    tn = _pick(n, (1024, 512, 256, 128))
    return pl.pallas_call(
        _mod_body,
        grid=(depth, n // tn),
        in_specs=[pl.BlockSpec((r, d), lambda l, j: (0, 0)),
                  pl.BlockSpec((None, d, tn), lambda l, j: (l, 0, j)),
                  pl.BlockSpec((None, 1, tn), lambda l, j: (l, 0, j))],
        out_specs=pl.BlockSpec((None, r, tn), lambda l, j: (l, 0, j)),
        out_shape=jax.ShapeDtypeStruct((depth, r, n), F32),
        compiler_params=_cparams(("parallel", "parallel")),
        name="mod_vectors",
    )(cond, w_mod, b_mod.reshape(depth, 1, n))


def _ln_rows(x):
    mu = jnp.mean(x, axis=-1, keepdims=True)
    xc = x - mu
    var = jnp.mean(xc * xc, axis=-1, keepdims=True)
    return xc * lax.rsqrt(var + LN_EPS)


def _modln_body(x_ref, sh_ref, sc_ref, h_ref):
    h_ref[...] = (_ln_rows(x_ref[...]) * (1.0 + sc_ref[...]) + sh_ref[...]).astype(h_ref.dtype)


def modulate_ln(x, shift, scale, *, rows_per_group, m_rows=None, tr=256):
    M = x.shape[0] if m_rows is None else m_rows
    d = x.shape[1]
    assert M % tr == 0 and rows_per_group % tr == 0
    gmap = lambda i: ((i * tr) // rows_per_group, 0, 0)
    return pl.pallas_call(
        _modln_body,
        grid=(M // tr,),
        in_specs=[pl.BlockSpec((tr, d), lambda i: (i, 0)),
                  pl.BlockSpec((None, 1, d), gmap),
                  pl.BlockSpec((None, 1, d), gmap)],
        out_specs=pl.BlockSpec((tr, d), lambda i: (i, 0)),
        out_shape=jax.ShapeDtypeStruct((M, d), BF16),
        compiler_params=_cparams(("parallel",)),
        name="modulate_ln",
    )(x, shift, scale)


def _resln_body(x_ref, y_ref, gate_ref, g_ref, b_ref, sh_ref, sc_ref, xo_ref, h_ref, *, alpha):
    xn = _ln_rows(alpha * x_ref[...] + gate_ref[...] * y_ref[...]) * g_ref[...] + b_ref[...]
    xo_ref[...] = xn
    h_ref[...] = (_ln_rows(xn) * (1.0 + sc_ref[...]) + sh_ref[...]).astype(h_ref.dtype)


def _resln_last_body(x_ref, y_ref, gate_ref, g_ref, b_ref, xo_ref, *, alpha):
    xo_ref[...] = _ln_rows(alpha * x_ref[...] + gate_ref[...] * y_ref[...]) * g_ref[...] + b_ref[...]


def residual_ln(x, y, gate, ln_g, ln_b, shift, scale, *, alpha, rows_per_group, m_rows=None, tr=256,
                h_dtype=None):
    M = x.shape[0] if m_rows is None else m_rows
    d = x.shape[1]
    assert M % tr == 0 and rows_per_group % tr == 0
    gmap = lambda i: ((i * tr) // rows_per_group, 0, 0)
    row = pl.BlockSpec((tr, d), lambda i: (i, 0))
    vec = pl.BlockSpec((1, d), lambda i: (0, 0))
    gvec = pl.BlockSpec((None, 1, d), gmap)
    if shift is None:
        return pl.pallas_call(
            functools.partial(_resln_last_body, alpha=alpha),
            grid=(M // tr,),
            in_specs=[row, row, gvec, vec, vec],
            out_specs=row,
            out_shape=jax.ShapeDtypeStruct((M, d), F32),
            compiler_params=_cparams(("parallel",)),
            name="residual_ln_last",
        )(x, y, gate, ln_g.reshape(1, d), ln_b.reshape(1, d)), None
    return pl.pallas_call(
        functools.partial(_resln_body, alpha=alpha),
        grid=(M // tr,),
        in_specs=[row, row, gvec, vec, vec, gvec, gvec],
        out_specs=[row, row],
        out_shape=[jax.ShapeDtypeStruct((M, d), F32), jax.ShapeDtypeStruct((M, d), h_dtype or BF16)],
        compiler_params=_cparams(("parallel",)),
        name="residual_ln",
    )(x, y, gate, ln_g.reshape(1, d), ln_b.reshape(1, d), shift, scale)


def _flash_body(qt_ref, *refs, seg_lens, tk):
    n_seg = len(seg_lens)
    kv_refs = refs[:2 * n_seg]
    o_ref = refs[2 * n_seg]
    acc_ref, st_a, st_b = refs[2 * n_seg + 1:]
    bufs = (st_a, st_b)
    qt = qt_ref[...]
    tq = qt.shape[1]
    acc_ref[...] = jnp.zeros_like(acc_ref)
    m = jnp.full((1, tq), -jnp.inf, F32)

    def scores(seg, c, tks):
        k = kv_refs[2 * seg][pl.ds(pl.multiple_of(c * tks, tks), tks), :]
        return jnp.dot(k, qt, preferred_element_type=F32)

    def absorb(st, seg, c, tks, m_old):
        vt = kv_refs[2 * seg + 1][:, pl.ds(pl.multiple_of(c * tks, tks), tks)]
        vt1 = jnp.concatenate([vt, jnp.ones((FLASH_ONES, tks), BF16)], axis=0)
        m_new = jnp.maximum(m_old, jnp.max(st, axis=0, keepdims=True))
        p = jnp.exp2(st - m_new).astype(BF16)
        corr = jnp.exp2(m_old - m_new)
        acc_ref[...] = corr * acc_ref[...] + jnp.dot(vt1, p, preferred_element_type=F32)
        return m_new

    def run_static(chunks, cur, m):
        for i, (seg, c, tks) in enumerate(chunks):
            if i + 1 < len(chunks):
                nseg, nc, ntks = chunks[i + 1]
                bufs[1 - cur][:ntks] = scores(nseg, nc, ntks)
            m = absorb(bufs[cur][:tks], seg, c, tks, m)
            cur = 1 - cur
        return m

    chunk_counts = [(s, min(tk, ln), ln // min(tk, ln)) for s, ln in enumerate(seg_lens)]
    s0, tk0, n0 = chunk_counts[0]
    rest = [(s, c, tks) for s, tks, n in chunk_counts[1:] for c in range(n)]
    if n0 >= 4 and n0 % 2 == 0:
        st_a[...] = scores(s0, 0, tk0)

        def pair(j, m):
            c0 = 2 * j
            st_b[...] = scores(s0, c0 + 1, tk0)
            m = absorb(st_a[...], s0, c0, tk0, m)
            st_a[...] = scores(s0, c0 + 2, tk0)
            return absorb(st_b[...], s0, c0 + 1, tk0, m)

        m = lax.fori_loop(0, n0 // 2 - 1, pair, m)
        m = run_static([(s0, n0 - 2, tk0), (s0, n0 - 1, tk0)] + rest, 0, m)
    else:
        chunks = [(s0, c, tk0) for c in range(n0)] + rest
        st_a[:tk0] = scores(s0, 0, tk0)
        m = run_static(chunks, 0, m)
    acc = acc_ref[...]
    o_ref[...] = (acc[:V_HEAD] / acc[V_HEAD:V_HEAD + 1]).T.astype(o_ref.dtype)


def flash_attention(qt, k, vt, *, n_batch, q_row0, q_len, segs, tq, tk, name="flash"):
    h = N_HEADS
    nq = q_len // tq
    assert q_len % tq == 0 and q_row0 % tq == 0
    in_specs = [pl.BlockSpec((HEAD_PAD, tq), lambda b, hh, i: (hh, q_row0 // tq + b * nq + i))]
    args = [qt]
    for row0, ln in segs:
        assert row0 % ln == 0
        in_specs.append(pl.BlockSpec((ln, HEAD_PAD), lambda b, hh, i, r=row0 // ln: (r + b, hh)))
        in_specs.append(pl.BlockSpec((V_HEAD, ln), lambda b, hh, i, r=row0 // ln: (hh, r + b)))
        args += [k, vt]
    body = functools.partial(_flash_body, seg_lens=tuple(ln for _, ln in segs), tk=tk)
    return pl.pallas_call(
        body,
        grid=(n_batch, h, nq),
        in_specs=in_specs,
        out_specs=pl.BlockSpec((tq, V_HEAD), lambda b, hh, i: (b * nq + i, hh)),
        out_shape=jax.ShapeDtypeStruct((n_batch * q_len, h * V_HEAD), BF16),
        scratch_shapes=[pltpu.VMEM((V_HEAD + FLASH_ONES, tq), F32),
                        pltpu.VMEM((tk, tq), F32), pltpu.VMEM((tk, tq), F32)],
        compiler_params=_cparams(("parallel", "parallel", "arbitrary")),
        name=name,
    )(*args)


def _chunk_rows(x_ref):
    return jnp.concatenate([x_ref[:, t, :] for t in range(S5_L)], axis=1).astype(BF16)


def _s5_drive_body(x_ref, w_ref, o_ref):
    res = jnp.dot(_chunk_rows(x_ref), w_ref[...], preferred_element_type=F32)
    for gl in range(S5_SG):
        o_ref[:, gl, :] = res[:, gl * 4 * S5_STATE:(gl + 1) * 4 * S5_STATE]


def _s5_out_body(x_ref, sf_ref, sb_ref, t_ref, q_ref, o_ref):
    s = jnp.concatenate([sf_ref[:, gl, :] for gl in range(S5_SG)]
                        + [sb_ref[:, gl, :] for gl in range(S5_SG)], axis=1).astype(BF16)
    res = (jnp.dot(_chunk_rows(x_ref), t_ref[...], preferred_element_type=F32)
           + jnp.dot(s, q_ref[...], preferred_element_type=F32))
    for t in range(S5_L):
        o_ref[:, t, :] = res[:, t * LANE:(t + 1) * LANE]


def _s5_scan_body(wf_ref, wb_ref, af_ref, bf_ref, ab_ref, bb_ref, sf_ref, sb_ref, st_f, st_b):
    @pl.when(pl.program_id(1) == 0)
    def _():
        st_f[...] = jnp.zeros_like(st_f)
        st_b[...] = jnp.zeros_like(st_b)

    a_f, b_f, a_b, b_b = af_ref[...], bf_ref[...], ab_ref[...], bb_ref[...]
    cb = wf_ref.shape[0]

    def step(c, carry):
        s_f, s_b = carry
        cr = cb - 1 - c
        sf_ref[c] = s_f
        sb_ref[cr] = s_b
        n_f = a_f * s_f + b_f * pltpu.roll(s_f, S5_STATE, 1) + wf_ref[c]
        n_b = a_b * s_b + b_b * pltpu.roll(s_b, S5_STATE, 1) + wb_ref[cr]
        return n_f, n_b

    s_f, s_b = lax.fori_loop(0, cb, step, (st_f[...], st_b[...]))
    st_f[...] = s_f
    st_b[...] = s_b


def s5_scan(w3, lam, *, n_batch, n_xc, n_cc, cb):
    nch, g, _ = w3.shape
    p2 = 2 * S5_STATE
    n_xb, n_cb = n_xc // cb, n_cc // cb
    ctx0 = n_batch * n_xb

    def fwd_blk(b, j):
        return jnp.where(j < n_cb, ctx0 + b * n_cb + j, b * n_xb + j - n_cb)

    def bwd_blk(b, j):
        return jnp.where(j < n_cb, ctx0 + b * n_cb + (n_cb - 1 - j), b * n_xb + (n_xb - 1 - (j - n_cb)))

    coef = pl.BlockSpec((g, p2), lambda b, j: (0, 0))
    return pl.pallas_call(
        _s5_scan_body,
        grid=(n_batch, n_xb + n_cb),
        in_specs=[pl.BlockSpec((cb, g, p2), lambda b, j: (fwd_blk(b, j), 0, 0)),
                  pl.BlockSpec((cb, g, p2), lambda b, j: (bwd_blk(b, j), 0, 1)),
                  coef, coef, coef, coef],
        out_specs=[pl.BlockSpec((cb, g, p2), lambda b, j: (fwd_blk(b, j), 0, 0)),
                   pl.BlockSpec((cb, g, p2), lambda b, j: (bwd_blk(b, j), 0, 0))],
        out_shape=[jax.ShapeDtypeStruct((nch, g, p2), F32)] * 2,
        scratch_shapes=[pltpu.VMEM((g, p2), F32), pltpu.VMEM((g, p2), F32)],
        compiler_params=_cparams(("arbitrary", "arbitrary")),
        name="s5_scan",
    )(w3, w3, *lam)


def _s5_tables(a_re, a_im, log_dt, b_re, b_im, c_re, c_im):
    L, P, Hh = S5_L, S5_STATE, S5_GROUP
    hp = lax.Precision.HIGHEST
    dt = jnp.exp(log_dt.astype(F32))[..., None]
    ar, ai = a_re.astype(F32), a_im.astype(F32)
    j = jnp.arange(L + 1, dtype=F32)[:, None, None, None]
    mag = jnp.exp(j * ar * dt)
    pr, pi = mag * jnp.cos(j * ai * dt), mag * jnp.sin(j * ai * dt)
    lr, li = pr[1], pi[1]
    nr = lr - 1.0
    den = ar * ar + ai * ai
    f_re = ((nr * ar + li * ai) / den)[..., None]
    f_im = ((li * ar - nr * ai) / den)[..., None]
    br, bi = b_re.astype(F32), b_im.astype(F32)
    bb_re = f_re * br - f_im * bi
    bb_im = f_re * bi + f_im * br
    cr, ci = c_re.astype(F32), c_im.astype(F32)

    zr = pr[:L, ..., None] * bb_re - pi[:L, ..., None] * bb_im
    zi = pr[:L, ..., None] * bb_im + pi[:L, ..., None] * bb_re
    kj = (jnp.einsum('dghp,jdgpk->jdghk', cr, zr, precision=hp)
          - jnp.einsum('dghp,jdgpk->jdghk', ci, zi, precision=hp))
    t_idx = jnp.arange(L)
    lag = t_idx[None, :] - t_idx[:, None]
    kf = kj[:, 0][jnp.clip(lag, 0, L - 1)]
    kb = kj[:, 1][jnp.clip(-lag, 0, L - 1)]
    tm4 = (jnp.where((lag >= 0)[:, :, None, None, None], kf, 0.0)
           + jnp.where((lag <= 0)[:, :, None, None, None], kb, 0.0))
    g = tm4.shape[2]
    tmat = jnp.transpose(tm4, (2, 0, 4, 1, 3)).reshape(g, L * Hh, L * Hh)

    kk = jnp.arange(L)
    pf_r, pf_i = pr[L - 1 - kk, 0], pi[L - 1 - kk, 0]
    pb_r, pb_i = pr[kk, 1], pi[kk, 1]
    wf_re = pf_r[..., None] * bb_re[0] - pf_i[..., None] * bb_im[0]
    wf_im = pf_r[..., None] * bb_im[0] + pf_i[..., None] * bb_re[0]
    wb_re = pb_r[..., None] * bb_re[1] - pb_i[..., None] * bb_im[1]
    wb_im = pb_r[..., None] * bb_im[1] + pb_i[..., None] * bb_re[1]
    wcat = jnp.concatenate([wf_re, wf_im, wb_re, wb_im], axis=2)
    wmat = jnp.transpose(wcat, (1, 0, 3, 2)).reshape(g, L * Hh, 4 * P)

    qf_r, qf_i = pr[kk + 1, 0], pi[kk + 1, 0]
    qb_r, qb_i = pr[L - kk, 1], pi[L - kk, 1]

    def qpair(c_r, c_i, q_r, q_i):
        return (c_r[None] * q_r[:, :, None, :] - c_i[None] * q_i[:, :, None, :],
                -c_r[None] * q_i[:, :, None, :] - c_i[None] * q_r[:, :, None, :])

    qf_re, qf_im = qpair(cr[0], ci[0], qf_r, qf_i)
    qb_re, qb_im = qpair(cr[1], ci[1], qb_r, qb_i)
    qf = jnp.transpose(jnp.concatenate([qf_re, qf_im], axis=3), (1, 3, 0, 2))
    qb = jnp.transpose(jnp.concatenate([qb_re, qb_im], axis=3), (1, 3, 0, 2))

    n_in = L * S5_SG * Hh
    lane = jnp.arange(n_in)
    rep_t = (jnp.arange(L * Hh)[:, None] == ((lane // LANE) * Hh + lane % Hh)[None, :]).astype(BF16)
    rep_w = (jnp.arange(4 * P)[:, None] == (lane % (4 * P))[None, :]).astype(BF16)
    col_grp_t = (lane // Hh) % S5_SG
    col_grp_w = lane // (4 * P)
    wfull = s5_spread([wmat.astype(BF16)], rep_w, col_grp_w, chunk_rows=True)
    tfull = s5_spread([tmat.astype(BF16)], rep_t, col_grp_t, chunk_rows=True)
    qfull = s5_spread([qf.reshape(g, 2 * P, L * Hh).astype(BF16), qb.reshape(g, 2 * P, L * Hh).astype(BF16)],
                      rep_t, col_grp_t, chunk_rows=False)

    def lam_tiles(d):
        return (jnp.concatenate([pr[L, d], pr[L, d]], axis=-1), jnp.concatenate([-pi[L, d], pi[L, d]], axis=-1))

    lam = lam_tiles(0) + lam_tiles(1)
    return wfull, tfull, qfull, lam


def _s5_spread_body(*refs, n_src, chunk_rows):
    src_refs = refs[:n_src]
    rep_ref, cgrp_ref, o_ref = refs[n_src:]
    if chunk_rows:
        pieces = [src_refs[0][a, k * S5_GROUP:(k + 1) * S5_GROUP, :] for k in range(S5_L) for a in range(S5_SG)]
        per_grp = S5_GROUP
    else:
        pieces = [r[a] for r in src_refs for a in range(S5_SG)]
        per_grp = src_refs[0].shape[1]
    rows = jnp.concatenate(pieces, axis=0)
    full = jnp.dot(rows, rep_ref[...], preferred_element_type=F32)
    assert per_grp & (per_grp - 1) == 0 and S5_SG & (S5_SG - 1) == 0
    row_grp = (lax.broadcasted_iota(jnp.int32, full.shape, 0) >> (per_grp.bit_length() - 1)) & (S5_SG - 1)
    o_ref[...] = jnp.where(row_grp == cgrp_ref[...], full, 0.0).astype(o_ref.dtype)


def s5_spread(srcs, rep, col_grp, *, chunk_rows):
    g, r, c = srcs[0].shape
    n = rep.shape[1]
    n_rows = len(srcs) * S5_SG * r
    return pl.pallas_call(
        functools.partial(_s5_spread_body, n_src=len(srcs), chunk_rows=chunk_rows),
        grid=(g // S5_SG,),
        in_specs=[pl.BlockSpec((S5_SG, r, c), lambda s: (s, 0, 0)) for _ in srcs]
        + [pl.BlockSpec((c, n), lambda s: (0, 0)), pl.BlockSpec((1, n), lambda s: (0, 0))],
        out_specs=pl.BlockSpec((None, n_rows, n), lambda s: (s, 0, 0)),
        out_shape=jax.ShapeDtypeStruct((g // S5_SG, n_rows, n), BF16),
        compiler_params=_cparams(("parallel",)),
        name="s5_spread",
    )(*srcs, rep, col_grp.astype(jnp.int32).reshape(1, n))


def s5_mix(u, tables, *, n_batch, seq, ctx_len):
    wfull, tfull, qfull, lam = tables
    L, P = S5_L, S5_STATE
    t_rows, w_tot = u.shape
    g = w_tot // S5_GROUP
    nsg = g // S5_SG
    nch = t_rows // L
    n_in = L * LANE
    rb = max(r for r in range(8, 265, 8) if nch % r == 0)
    u3 = u.reshape(nch, L, w_tot)
    xspec = pl.BlockSpec((rb, L, LANE), lambda s, i: (i, 0, s))
    wspec = lambda k, n: pl.BlockSpec((None, k, n), lambda s, i: (s, 0, 0))

    w3 = pl.pallas_call(
        _s5_drive_body,
        grid=(nsg, nch // rb),
        in_specs=[xspec, wspec(n_in, S5_SG * 4 * P)],
        out_specs=pl.BlockSpec((rb, S5_SG, 4 * P), lambda s, i: (i, s, 0)),
        out_shape=jax.ShapeDtypeStruct((nch, g, 4 * P), F32),
        compiler_params=_cparams(("parallel", "parallel")),
        name="s5_drive",
    )(u3, wfull)

    n_xc, n_cc = seq // L, ctx_len // L
    cb = _pick(math.gcd(n_xc, n_cc), (16, 8, 4, 2, 1))
    sf, sb = s5_scan(w3, lam, n_batch=n_batch, n_xc=n_xc, n_cc=n_cc, cb=cb)

    sspec = pl.BlockSpec((rb, S5_SG, 2 * P), lambda s, i: (i, s, 0))
    y3 = pl.pallas_call(
        _s5_out_body,
        grid=(nsg, nch // rb),
        in_specs=[xspec, sspec, sspec, wspec(n_in, n_in), wspec(S5_SG * 4 * P, n_in)],
        out_specs=xspec,
        out_shape=jax.ShapeDtypeStruct((nch, L, w_tot), F32),
        compiler_params=_cparams(("parallel", "parallel")),
        name="s5_out",
    )(u3, sf, sb, tfull, qfull)
    return y3.reshape(t_rows, w_tot)


R_E1, R_E2, R_W1, R_W2, R_RANK1, R_RANK2 = range(6)


def _router_body(lg_ref, b_ref, meta_ref, cnt_ref, carry_ref, *, n_exp):
    @pl.when(pl.program_id(0) == 0)
    def _():
        carry_ref[...] = jnp.zeros_like(carry_ref)

    lg = lg_ref[...] + b_ref[...]
    tr = lg.shape[0]
    lane = lax.broadcasted_iota(jnp.int32, lg.shape, 1).astype(F32)
    neg = jnp.float32(-jnp.inf)
    lg = jnp.where(lane < n_exp, lg, neg)
    m1 = jnp.max(lg, axis=-1, keepdims=True)
    i1 = jnp.min(jnp.where(lg == m1, lane, float(LANE)), axis=-1, keepdims=True)
    lg2 = jnp.where(lane == i1, neg, lg)
    m2 = jnp.max(lg2, axis=-1, keepdims=True)
    i2 = jnp.min(jnp.where(lg2 == m2, lane, float(LANE)), axis=-1, keepdims=True)
    e2 = jnp.exp(m2 - m1)
    den = 1.0 + e2
    sel = jnp.where((lane == i1) | (lane == i2), 1.0, 0.0)
    r_i = lax.broadcasted_iota(jnp.int32, (tr, tr), 0)
    c_i = lax.broadcasted_iota(jnp.int32, (tr, tr), 1)
    tri = jnp.where(r_i > c_i, 1.0, 0.0).astype(BF16)
    before = jnp.dot(tri, sel.astype(BF16), preferred_element_type=F32) + carry_ref[...]
    rank1 = jnp.sum(jnp.where(lane == i1, before, 0.0), axis=-1, keepdims=True)
    rank2 = jnp.sum(jnp.where(lane == i2, before, 0.0), axis=-1, keepdims=True)
    total = carry_ref[...] + jnp.sum(sel, axis=0, keepdims=True)
    carry_ref[...] = total
    cnt_ref[...] = jnp.broadcast_to(total, cnt_ref.shape)
    meta = jnp.zeros_like(lg)
    for idx, val in ((R_E1, i1), (R_E2, i2), (R_W1, 1.0 / den), (R_W2, e2 / den),
                     (R_RANK1, rank1), (R_RANK2, rank2)):
        meta = jnp.where(lane == idx, val, meta)
    meta_ref[...] = meta


def router_top2(logits, b_router_pad, *, n_exp, tr=512):
    m = logits.shape[0]
    tr = _pick(m, (tr, 256, 128, 64, 32, 16, 8))
    return pl.pallas_call(
        functools.partial(_router_body, n_exp=n_exp),
        grid=(m // tr,),
        in_specs=[pl.BlockSpec((tr, LANE), lambda i: (i, 0)), pl.BlockSpec((1, LANE), lambda i: (0, 0))],
        out_specs=[pl.BlockSpec((tr, LANE), lambda i: (i, 0)), pl.BlockSpec((8, LANE), lambda i: (0, 0))],
        out_shape=[jax.ShapeDtypeStruct((m, LANE), F32), jax.ShapeDtypeStruct((8, LANE), F32)],
        scratch_shapes=[pltpu.VMEM((1, LANE), F32)],
        compiler_params=_cparams(("arbitrary",)),
        name="router_top2",
    )(logits, b_router_pad)


def _row_copy_wait(src_ref, dst_ref, sem):
    pltpu.make_async_copy(src_ref.at[pl.ds(0, 1), :], dst_ref.at[pl.ds(0, 1), :], sem).wait()


def _gather_rows_body(idx_ref, src_ref, o_ref, buf, sem, *, tb):
    def issue(t, carry):
        pltpu.make_async_copy(src_ref.at[pl.ds(idx_ref[0, t], 1), :], buf.at[pl.ds(t, 1), :], sem).start()
        return carry

    lax.fori_loop(0, tb, issue, 0)

    def drain(t, carry):
        _row_copy_wait(src_ref, buf, sem)
        return carry

    lax.fori_loop(0, tb, drain, 0)
    o_ref[...] = buf[...]


def gather_rows(src, idx, *, tb=256):
    r, d = idx.shape[0], src.shape[1]
    tb = _pick(r, (tb, 128, 64, 32, 16, 8))
    return pl.pallas_call(
        functools.partial(_gather_rows_body, tb=tb),
        grid=(r // tb,),
        in_specs=[pl.BlockSpec((None, 1, tb), lambda i: (i, 0, 0), memory_space=pltpu.SMEM),
                  pl.BlockSpec(memory_space=pl.ANY)],
        out_specs=pl.BlockSpec((tb, d), lambda i: (i, 0)),
        out_shape=jax.ShapeDtypeStruct((r, d), src.dtype),
        scratch_shapes=[pltpu.VMEM((tb, d), src.dtype), pltpu.SemaphoreType.DMA(())],
        compiler_params=_cparams(("arbitrary",)),
        name="moe_gather_rows",
    )(idx.reshape(r // tb, 1, tb), src)


def _gather_combine_body(pos_ref, meta_ref, y_ref, o_ref, buf, sem, *, tb):
    def issue(t, carry):
        for k in range(TOP_K):
            pltpu.make_async_copy(y_ref.at[pl.ds(pos_ref[0, TOP_K * t + k], 1), :],
                                  buf.at[k, pl.ds(t, 1), :], sem).start()
        return carry

    lax.fori_loop(0, tb, issue, 0)

    def drain(t, carry):
        for k in range(TOP_K):
            _row_copy_wait(y_ref, buf.at[0], sem)
        return carry

    lax.fori_loop(0, tb, drain, 0)
    meta = meta_ref[...]
    o_ref[...] = meta[:, R_W1:R_W1 + 1] * buf[0] + meta[:, R_W2:R_W2 + 1] * buf[1]


def gather_combine(y, pos, meta, *, tb=256):
    m, d = pos.shape[0], y.shape[1]
    tb = _pick(m, (tb, 128, 64, 32, 16, 8))
    pos3 = pos.reshape(m // tb, 1, TOP_K * tb)
    return pl.pallas_call(
        functools.partial(_gather_combine_body, tb=tb),
        grid=(m // tb,),
        in_specs=[pl.BlockSpec((None, 1, TOP_K * tb), lambda i: (i, 0, 0), memory_space=pltpu.SMEM),
                  pl.BlockSpec((tb, LANE), lambda i: (i, 0)),
                  pl.BlockSpec(memory_space=pl.ANY)],
        out_specs=pl.BlockSpec((tb, d), lambda i: (i, 0)),
        out_shape=jax.ShapeDtypeStruct((m, d), F32),
        scratch_shapes=[pltpu.VMEM((TOP_K, tb, d), F32), pltpu.SemaphoreType.DMA(())],
        compiler_params=_cparams(("arbitrary",)),
        name="moe_gather_combine",
    )(pos3, meta, y)


def _gmm_up_body(te_ref, nv_ref, a_ref, b1_ref, b3_ref, o_ref, a_s):
    del te_ref
    live = pl.program_id(0) < nv_ref[0]
    first_col = pl.program_id(1) == 0

    @pl.when(live)
    def _():
        @pl.when(first_col)
        def _():
            a_s[...] = a_ref[...].astype(BF16)

        a = a_s[...]
        g = jnp.dot(a, b1_ref[...], preferred_element_type=F32)
        u = jnp.dot(a, b3_ref[...], preferred_element_type=F32)
        o_ref[...] = (g * jax.nn.sigmoid(g) * u).astype(o_ref.dtype)

    @pl.when(jnp.logical_not(live))
    def _():
        o_ref[...] = jnp.zeros_like(o_ref)


def _gmm_down_body(te_ref, nv_ref, a_ref, b_ref, o_ref):
    del te_ref
    live = pl.program_id(0) < nv_ref[0]

    @pl.when(live)
    def _():
        o_ref[...] = jnp.dot(a_ref[...], b_ref[...], preferred_element_type=F32)

    @pl.when(jnp.logical_not(live))
    def _():
        o_ref[...] = jnp.zeros_like(o_ref)


def grouped_swiglu(xs, w1, w3, w2, tile_expert, n_valid, *, tm):
    r, d = xs.shape
    f = w1.shape[2]
    n_tiles = r // tm
    tn_up = _pick(f, (512, 256, 128))
    tn_dn = _pick(d, (512, 256, 128))
    act = pl.pallas_call(
        _gmm_up_body,
        grid_spec=pltpu.PrefetchScalarGridSpec(
            num_scalar_prefetch=2,
            grid=(n_tiles, f // tn_up),
            in_specs=[pl.BlockSpec((tm, d), lambda i, j, te, nv: (i, 0)),
                      pl.BlockSpec((None, d, tn_up), lambda i, j, te, nv: (te[i], 0, j)),
                      pl.BlockSpec((None, d, tn_up), lambda i, j, te, nv: (te[i], 0, j))],
            out_specs=pl.BlockSpec((tm, tn_up), lambda i, j, te, nv: (i, j)),
            scratch_shapes=[pltpu.VMEM((tm, d), BF16)]),
        out_shape=jax.ShapeDtypeStruct((r, f), BF16),
        compiler_params=_cparams(("arbitrary", "arbitrary")),
        name="moe_up",
    )(tile_expert, n_valid, xs, w1, w3)
    return pl.pallas_call(
        _gmm_down_body,
        grid_spec=pltpu.PrefetchScalarGridSpec(
            num_scalar_prefetch=2,
            grid=(n_tiles, d // tn_dn),
            in_specs=[pl.BlockSpec((tm, f), lambda i, j, te, nv: (i, 0)),
                      pl.BlockSpec((None, f, tn_dn), lambda i, j, te, nv: (te[i], 0, j))],
            out_specs=pl.BlockSpec((tm, tn_dn), lambda i, j, te, nv: (i, j))),
        out_shape=jax.ShapeDtypeStruct((r, d), F32),
        compiler_params=_cparams(("arbitrary", "arbitrary")),
        name="moe_down",
    )(tile_expert, n_valid, act, w2)


def sparse_moe(hf, meta, counts, w1, w3, w2, *, tm=512):
    m = hf.shape[0]
    n_exp = w1.shape[0]
    tm = _pick(m, (tm, 256, 128))
    cnt = counts.astype(jnp.int32)
    padded = (cnt + tm - 1) // tm * tm
    ends = jnp.cumsum(padded)
    offs = ends - padded
    e1 = meta[:, R_E1].astype(jnp.int32)
    e2 = meta[:, R_E2].astype(jnp.int32)
    pos = jnp.stack([offs[e1] + meta[:, R_RANK1].astype(jnp.int32),
                     offs[e2] + meta[:, R_RANK2].astype(jnp.int32)], axis=1)
    n_tiles = TOP_K * m // tm + n_exp
    tile_expert = jnp.minimum(jnp.searchsorted(ends, jnp.arange(n_tiles) * tm, side="right"),
                              n_exp - 1).astype(jnp.int32)
    n_valid = (ends[-1:] // tm).astype(jnp.int32)
    src_tok = jnp.zeros((n_tiles * tm,), jnp.int32).at[pos.reshape(-1)].set(
        jnp.repeat(jnp.arange(m, dtype=jnp.int32), TOP_K))
    xs = gather_rows(hf, src_tok)
    y = grouped_swiglu(xs, w1, w3, w2, tile_expert, n_valid, tm=tm)
    return gather_combine(y, pos, meta)


def _rms_pro(x, gain):
    return x * lax.rsqrt(jnp.mean(x * x, axis=-1, keepdims=True) + RMS_EPS) * gain


def _gelu_tanh(x):
    return 0.5 * x * (1.0 + jnp.tanh(math.sqrt(2.0 / math.pi) * (x + 0.044715 * (x * x * x))))


def _rope_apply(x, cos, sin_up, sin_dn):
    n = x.shape[-1]
    return x * cos + pltpu.roll(x, n - QK_ROPE // 4, 1) * sin_up + pltpu.roll(x, QK_ROPE // 4, 1) * sin_dn


def _rope_apply_t(x, cos, sin_up, sin_dn):
    n = x.shape[0]
    return x * cos + pltpu.roll(x, n - QK_ROPE // 4, 0) * sin_up + pltpu.roll(x, QK_ROPE // 4, 0) * sin_dn


def _rope_tables(n_batch, seq, n_ctx_rows):
    nf = QK_ROPE // 4
    pos = jnp.arange(seq)
    row = (pos // GRID_W).astype(F32)
    col = (pos % GRID_W).astype(F32)
    inv = ROPE_THETA ** (-jnp.arange(nf, dtype=F32) / nf)
    ar, ac = row[:, None] * inv, col[:, None] * inv
    z = jnp.zeros((seq, nf), F32)
    cos64 = jnp.concatenate([jnp.cos(ar), jnp.cos(ar), jnp.cos(ac), jnp.cos(ac)], axis=1)
    up64 = jnp.concatenate([-jnp.sin(ar), z, -jnp.sin(ac), z], axis=1)
    dn64 = jnp.concatenate([z, jnp.sin(ar), z, jnp.sin(ac)], axis=1)

    def place(t64, fill):
        full = jnp.full((seq, HEAD_PAD), fill, F32).at[:, QK_NOPE:QK_NOPE + QK_ROPE].set(t64)
        full = jnp.tile(full, (n_batch, 1))
        return jnp.concatenate([full, jnp.full((n_ctx_rows, HEAD_PAD), fill, F32)], axis=0)

    return place(cos64, 1.0), place(up64, 0.0), place(dn64, 0.0)


def _pad_cols(w, n):
    return jnp.pad(w, ((0, 0), (0, n - w.shape[1])))


def _head_cat_cols(w_a, w_b, da, db):
    k = w_a.shape[0]
    parts = [w_a.reshape(k, N_HEADS, da)]
    if w_b is not None:
        parts.append(w_b.reshape(k, N_HEADS, db))
    used = da + (db if w_b is not None else 0)
    parts.append(jnp.zeros((k, N_HEADS, HEAD_PAD - used), w_a.dtype))
    return jnp.concatenate(parts, axis=2).reshape(k, N_HEADS * HEAD_PAD)


def kernel(x, c, ctx, c_ctx, w_mod, b_mod, w_in, b_gate, q_norm, w_uq, kv_norm, w_ukv, w_branch_mla,
           s5_a_re, s5_a_im, s5_log_dt, s5_b_re, s5_b_im, s5_c_re, s5_c_im, s5_d, w_glu, b_glu,
           w_branch_s5, w_out, ln_mix_g, ln_mix_b, ln_ffn_g, ln_ffn_b, ffn_w1, ffn_w3, ffn_w2,
           moe_w_router, moe_b_router, moe_w1, moe_w3, moe_w2):
    B, N, D = x.shape
    C = ctx.shape[1]
    depth = w_mod.shape[0]
    QL, KL = q_norm.shape[1], kv_norm.shape[1]
    SW = s5_d.shape[1]
    H = N_HEADS
    NX, NC_ROWS = B * N, B * C
    T = NX + NC_ROWS
    alpha = (2 * depth) ** 0.25
    q_scale = (QK_NOPE + QK_ROPE) ** -0.5 * math.log2(math.e)
    o_ckv, o_kr, o_u, o_g = QL, QL + KL, QL + KL + QK_ROPE, QL + KL + QK_ROPE + SW
    assert N % C == 0 and N % 256 == 0 and NC_ROWS % 256 == 0

    tm_all = _pick(T, (1536, 1024, 768, 512, 384, 256, 128))
    tm_x = _pick(NX, (1024, 512, 256, 128))
    tile_n = lambda n: _pick(n, (512, 256, 128))

    n_cond = B + 1
    cond = jnp.concatenate([c, c_ctx[None], jnp.zeros((-n_cond % 8, D), F32)], axis=0)
    mods = mod_vectors(cond, w_mod, b_mod)
    mods = mods.reshape(depth, cond.shape[0], 6, D)

    def mvec(l, k):
        return mods[l, :n_cond, k][:, None, :]

    rope_cos, rope_up, rope_dn = _rope_tables(B, N, NC_ROWS)
    rope_cos_t, rope_up_t, rope_dn_t = rope_cos.T, rope_up.T, rope_dn.T
    xt =jnp.concatenate([x.reshape(NX, D), ctx.reshape(NC_ROWS, D)], axis=0)
    h = modulate_ln(xt, mvec(0, 0), mvec(0, 1), rows_per_group=N)

    for l in range(depth):
        need_ctx = l < depth - 1
        rows = T if need_ctx else NX
        tm_r = tm_all if need_ctx else tm_x

        wi = w_in[l]
        w_cq = wi[:, :o_ckv].astype(BF16)
        w_ckv = wi[:, o_ckv:o_kr].astype(BF16)
        w_kr = jnp.concatenate(
            [jnp.zeros((D, QK_NOPE), F32), wi[:, o_kr:o_u],
             jnp.zeros((D, HEAD_PAD - QK_NOPE - QK_ROPE), F32)], axis=1).astype(BF16)
        w_u = wi[:, o_u:o_g].astype(BF16)
        w_gm = wi[:, o_g:o_g + D].astype(BF16)
        w_gs = wi[:, o_g + D:].astype(BF16)
        wq = w_uq[l].reshape(QL, H, QK_NOPE + QK_ROPE)
        w_q = _head_cat_cols(wq[:, :, :QK_NOPE].reshape(QL, -1), wq[:, :, QK_NOPE:].reshape(QL, -1),
                             QK_NOPE, QK_ROPE).astype(BF16)
        wkv = w_ukv[l].reshape(KL, H, QK_NOPE + V_HEAD)
        w_k = _head_cat_cols(wkv[:, :, :QK_NOPE].reshape(KL, -1), None, QK_NOPE, 0).astype(BF16)
        w_v = wkv[:, :, QK_NOPE:].reshape(KL, H * V_HEAD).astype(BF16)

        cqn = matmul([h], [w_cq], out_dtype=BF16, tm=tm_r // 2, tn=QL, m_rows=rows,
                     epilogue=_rms_pro, extras=((q_norm[l].reshape(1, QL), "col"),), name="in_cq")
        ckvn = matmul([h], [w_ckv], out_dtype=BF16, tm=tm_all, tn=KL,
                      epilogue=_rms_pro, extras=((kv_norm[l].reshape(1, KL), "col"),), name="in_ckv")
        krp = matmul([h], [w_kr], out_dtype=F32, tm=tm_all, tn=HEAD_PAD, name="in_kr")
        u = matmul([h], [w_u], out_dtype=F32, tm=tm_all, tn=tile_n(SW), name="in_s5")

        qt = matmul([w_q.T], [cqn], nt=True, out_dtype=BF16, tm=_pick(H * HEAD_PAD, (1024, 512, 256)), tn=512,
                    n_cols=rows, epilogue=lambda acc, cs, up, dn: _rope_apply_t(acc, cs, up, dn) * q_scale,
                    extras=((rope_cos_t, "coltab_tiled"), (rope_up_t, "coltab_tiled"), (rope_dn_t, "coltab_tiled")),
                    name="mla_qt")
        kh = matmul([ckvn], [w_k], out_dtype=BF16, tm=tm_all, tn=512,
                    epilogue=lambda acc, kr, cs, up, dn: acc + jnp.tile(_rope_apply(kr, cs, up, dn),
                                                                        (1, acc.shape[1] // HEAD_PAD)),
                    extras=((krp, "rowtab"), (rope_cos, "rowtab"), (rope_up, "rowtab"), (rope_dn, "rowtab")),
                    name="mla_k")
        vt = matmul([w_v.T], [ckvn], nt=True, out_dtype=BF16, tm=_pick(H * V_HEAD, (1024, 512, 256, 128)),
                    tn=512, name="mla_vt")
        tq = _pick(N, (512, 256, 128))
        o_x = flash_attention(qt, kh, vt, n_batch=B, q_row0=0, q_len=N, segs=[(0, N), (NX, C)],
                              tq=tq, tk=1024, name="flash_x")
        if need_ctx:
            o_c = flash_attention(qt, kh, vt, n_batch=B, q_row0=NX, q_len=C, segs=[(NX, C)],
                                  tq=_pick(C, (256, 128)), tk=512, name="flash_ctx")
            o_all = jnp.concatenate([o_x, o_c], axis=0)
        else:
            o_all = o_x

        tables = _s5_tables(s5_a_re[l], s5_a_im[l], s5_log_dt[l], s5_b_re[l], s5_b_im[l],
                            s5_c_re[l], s5_c_im[l])
        y = s5_mix(u, tables, n_batch=B, seq=N, ctx_len=C)
        d_row = s5_d[l].reshape(1, SW)
        glu_pro = lambda yv, uv, dv: _gelu_tanh(yv + dv * uv)
        ys = matmul([y, u], [w_glu[l].astype(BF16)], out_dtype=BF16, tm=tm_r // 2, tn=tile_n(SW), m_rows=rows,
                    prologue=glu_pro, pro_consts=(d_row,),
                    epilogue=lambda acc, yv, uv, dv, bv: (lambda gg: gg * jax.nn.sigmoid(acc + bv))(
                        _gelu_tanh(yv + dv * uv)),
                    extras=((y, "tile"), (u, "tile"), (d_row, "col"), (b_glu[l].reshape(1, SW), "col")),
                    name="s5_glu")

        bg = b_gate[l]
        m1 = matmul([o_all], [w_branch_mla[l].astype(BF16)], out_dtype=F32, tm=tm_r, tn=512, m_rows=rows,
                    name="branch_mla")
        gm = matmul([h], [w_gm], out_dtype=F32, tm=tm_r, tn=512, m_rows=rows,
                    epilogue=lambda acc, bv, mv: jax.nn.sigmoid(acc + bv) * mv,
                    extras=((bg[:D].reshape(1, D), "col"), (m1, "tile")), name="gate_mla")
        m2 = matmul([ys], [w_branch_s5[l].astype(BF16)], out_dtype=F32, tm=tm_r, tn=512, m_rows=rows,
                    name="branch_s5")
        merged = matmul([h], [w_gs], out_dtype=BF16, tm=tm_r, tn=512, m_rows=rows,
                        epilogue=lambda acc, bv, mv, pv: jax.nn.sigmoid(acc + bv) * mv + pv,
                        extras=((bg[D:].reshape(1, D), "col"), (m2, "tile"), (gm, "tile")), name="gate_s5")
        mix = matmul([merged], [w_out[l].astype(BF16)], out_dtype=F32, tm=tm_r, tn=512, m_rows=rows,
                     name="out_proj")
        xt, h2 = residual_ln(xt, mix, mvec(l, 2), ln_mix_g[l], ln_mix_b[l], mvec(l, 3), mvec(l, 4),
                             alpha=alpha, rows_per_group=N, m_rows=rows, h_dtype=BF16 if l % 2 == 0 else F32)

        if l % 2 == 0:
            fi = l // 2
            dff = ffn_w1.shape[2]
            dff_p = -(-dff // 512) * 512
            w1 = _pad_cols(ffn_w1[fi], dff_p).astype(BF16)
            w3 = _pad_cols(ffn_w3[fi], dff_p).astype(BF16)
            w2 = jnp.pad(ffn_w2[fi], ((0, dff_p - dff), (0, 0))).astype(BF16)
            act = matmul([h2], [w1, w3], out_dtype=BF16, tm=tm_r, tn=256, m_rows=rows,
                         epilogue=lambda a, b: a * jax.nn.sigmoid(a) * b, name="ffn_up")
            tk = _pick(dff_p, (2816, 2048, 1024, 512))
            ff = matmul_ksplit(act, w2, out_dtype=F32, tm=tm_r, tn=512, tk=tk, m_rows=rows, name="ffn_down")
        else:
            mi = l // 2
            n_exp = moe_w_router.shape[2]
            w_r = _pad_cols(moe_w_router[mi], LANE).astype(BF16)
            b_r = jnp.pad(moe_b_router[mi], (0, LANE - n_exp)).reshape(1, LANE)
            logits = matmul([h2], [w_r], out_dtype=F32, tm=tm_r // 2, tn=LANE, m_rows=rows,
                            prologue=lambda a: a, name="router_logits")
            meta, counts = router_top2(logits, b_r, n_exp=n_exp)
            ff = sparse_moe(h2, meta, counts[0, :n_exp], moe_w1[mi].astype(BF16), moe_w3[mi].astype(BF16),
                            moe_w2[mi].astype(BF16))
        if need_ctx:
            xt, h = residual_ln(xt, ff, mvec(l, 5), ln_ffn_g[l], ln_ffn_b[l], mvec(l + 1, 0), mvec(l + 1, 1),
                                alpha=alpha, rows_per_group=N, m_rows=rows)
        else:
            xt, _ = residual_ln(xt, ff, mvec(l, 5), ln_ffn_g[l], ln_ffn_b[l], None, None,
                                alpha=alpha, rows_per_group=N, m_rows=rows)
    return xt[:NX].reshape(B, N, D)
```

```python
import functools
import math

import jax
import jax.numpy as jnp
from jax import lax
from jax.experimental import pallas as pl
from jax.experimental.pallas import tpu as pltpu

N_HEADS = 32
QK_NOPE = 128
QK_ROPE = 64
V_HEAD = 128
ROPE_THETA = 10000.0
GRID_W = 64
S5_GROUP = 16
S5_STATE = 64
TOP_K = 2
LN_EPS = 1e-6
RMS_EPS = 1e-6

HEAD_PAD = 256
S5_L = 16
FLASH_ONES = 16
LANE = 128
S5_SG = LANE // S5_GROUP
VMEM_LIMIT_BYTES = 56 * 2**20

F32 = jnp.float32
BF16 = jnp.bfloat16


def _cparams(sem):
    return pltpu.CompilerParams(dimension_semantics=sem, vmem_limit_bytes=VMEM_LIMIT_BYTES)


def _pick(n, prefs):
    for p in prefs:
        if n % p == 0:
            return p
    raise ValueError(f"no tile in {prefs} divides {n}")


def _mm_body(*refs, n_a, n_pc, n_b, kinds, prologue, epilogue, tm, tn, nt):
    a_refs = refs[:n_a]
    pc_refs = refs[n_a:n_a + n_pc]
    b_refs = refs[n_a + n_pc:n_a + n_pc + n_b]
    ex_refs = refs[n_a + n_pc + n_b:n_a + n_pc + n_b + len(kinds)]
    o_ref = refs[n_a + n_pc + n_b + len(kinds)]
    if prologue is not None:
        a_s = refs[n_a + n_pc + n_b + len(kinds) + 1]

        @pl.when(pl.program_id(1) == 0)
        def _():
            a_s[...] = prologue(*[r[...] for r in a_refs], *[r[...] for r in pc_refs]).astype(BF16)

        a = a_s[...]
    else:
        a = a_refs[0][...]
    if nt:
        accs = [lax.dot_general(a, b[...], (((1,), (1,)), ((), ())), preferred_element_type=F32)
                for b in b_refs]
    else:
        accs = [jnp.dot(a, b[...].astype(BF16), preferred_element_type=F32) for b in b_refs]
    exs = []
    for r, kind in zip(ex_refs, kinds):
        v = r[...]
        if kind == "rowtab_tiled":
            v = jnp.tile(v, (1, tn // v.shape[1]))
        exs.append(v)
    o_ref[...] = epilogue(*accs, *exs).astype(o_ref.dtype)


def matmul(a_list, b_list, *, out_dtype, tm, tn, m_rows=None, n_cols=None, nt=False, epilogue=None,
           extras=(), prologue=None, pro_consts=(), name="mm"):
    K = a_list[0].shape[1]
    N = b_list[0].shape[0 if nt else 1] if n_cols is None else n_cols
    M = a_list[0].shape[0] if m_rows is None else m_rows
    tn = _pick(N, tuple(t for t in (tn, 512, 256, 128) if t <= tn))
    assert M % tm == 0 and N % tn == 0, (M, tm, N, tn)
    if epilogue is None:
        epilogue = lambda acc: acc
    if prologue is None:
        assert len(a_list) == 1 and a_list[0].dtype == BF16
    in_specs = [pl.BlockSpec((tm, K), lambda i, j: (i, 0)) for _ in a_list]
    in_specs += [pl.BlockSpec(c.shape, lambda i, j: (0, 0)) for c in pro_consts]
    if nt:
        in_specs += [pl.BlockSpec((tn, K), lambda i, j: (j, 0)) for _ in b_list]
    else:
        in_specs += [pl.BlockSpec((K, tn), lambda i, j: (0, j)) for _ in b_list]
    kinds = []
    ex_arrays = []
    for arr, kind in extras:
        kinds.append(kind)
        ex_arrays.append(arr)
        if kind == "tile":
            in_specs.append(pl.BlockSpec((tm, tn), lambda i, j: (i, j)))
        elif kind == "col":
            in_specs.append(pl.BlockSpec((1, tn), lambda i, j: (0, j)))
        elif kind in ("rowtab", "rowtab_tiled"):
            in_specs.append(pl.BlockSpec((tm, arr.shape[1]), lambda i, j: (i, 0)))
        elif kind == "coltab":
            in_specs.append(pl.BlockSpec((arr.shape[0], tn), lambda i, j: (0, j)))
        else:
            raise ValueError(kind)
    scratch = [pltpu.VMEM((tm, K), BF16)] if prologue is not None else []
    body = functools.partial(_mm_body, n_a=len(a_list), n_pc=len(pro_consts), n_b=len(b_list),
                             kinds=tuple(kinds), prologue=prologue, epilogue=epilogue, tm=tm, tn=tn, nt=nt)
    return pl.pallas_call(
        body,
        grid=(M // tm, N // tn),
        in_specs=in_specs,
        out_specs=pl.BlockSpec((tm, tn), lambda i, j: (i, j)),
        out_shape=jax.ShapeDtypeStruct((M, N), out_dtype),
        scratch_shapes=scratch,
        compiler_params=_cparams(("parallel", "arbitrary")),
        name=name,
    )(*a_list, *pro_consts, *b_list, *ex_arrays)


def _mmk_body(a_ref, b_ref, *rest, n_ex, epilogue, nk):
    ex_refs = rest[:n_ex]
    o_ref = rest[n_ex]
    acc_ref = rest[n_ex + 1]
    k = pl.program_id(2)

    @pl.when(k == 0)
    def _():
        acc_ref[...] = jnp.zeros_like(acc_ref)

    acc_ref[...] += jnp.dot(a_ref[...], b_ref[...], preferred_element_type=F32)

    @pl.when(k == nk - 1)
    def _():
        o_ref[...] = epilogue(acc_ref[...], *[e[...] for e in ex_refs]).astype(o_ref.dtype)


def matmul_ksplit(a, b, *, out_dtype, tm, tn, tk, m_rows=None, epilogue=None, extras=(), name="mmk"):
    K = a.shape[1]
    N = b.shape[1]
    M = a.shape[0] if m_rows is None else m_rows
    tn = _pick(N, tuple(t for t in (tn, 512, 256, 128) if t <= tn))
    assert M % tm == 0 and N % tn == 0 and K % tk == 0, (M, tm, N, tn, K, tk)
    if epilogue is None:
        epilogue = lambda acc: acc
    in_specs = [pl.BlockSpec((tm, tk), lambda i, j, k: (i, k)),
                pl.BlockSpec((tk, tn), lambda i, j, k: (k, j))]
    ex_arrays = []
    for arr, kind in extras:
        ex_arrays.append(arr)
        if kind == "tile":
            in_specs.append(pl.BlockSpec((tm, tn), lambda i, j, k: (i, j)))
        elif kind == "rowtab":
            in_specs.append(pl.BlockSpec((tm, arr.shape[1]), lambda i, j, k: (i, 0)))
        else:
            raise ValueError(kind)
    nk = K // tk
    body = functools.partial(_mmk_body, n_ex=len(ex_arrays), epilogue=epilogue, nk=nk)
    return pl.pallas_call(
        body,
        grid=(M // tm, N // tn, nk),
        in_specs=in_specs,
        out_specs=pl.BlockSpec((tm, tn), lambda i, j, k: (i, j)),
        out_shape=jax.ShapeDtypeStruct((M, N), out_dtype),
        scratch_shapes=[pltpu.VMEM((tm, tn), F32)],
        compiler_params=_cparams(("parallel", "parallel", "arbitrary")),
        name=name,
    )(a, b, *ex_arrays)


def _mod_body(c_ref, w_ref, b_ref, o_ref):
    c = c_ref[...]
    act = (c * jax.nn.sigmoid(c)).astype(BF16)
    o_ref[...] = jnp.dot(act, w_ref[...].astype(BF16), preferred_element_type=F32) + b_ref[...]


def mod_vectors(cond, w_mod, b_mod):
    depth, d, n = w_mod.shape
    r = cond.shape[0]
    tn = _pick(n, (1024, 512, 256, 128))
    return pl.pallas_call(
        _mod_body,
        grid=(depth, n // tn),
        in_specs=[pl.BlockSpec((r, d), lambda l, j: (0, 0)),
                  pl.BlockSpec((None, d, tn), lambda l, j: (l, 0, j)),
                  pl.BlockSpec((None, 1, tn), lambda l, j: (l, 0, j))],
        out_specs=pl.BlockSpec((None, r, tn), lambda l, j: (l, 0, j)),
        out_shape=jax.ShapeDtypeStruct((depth, r, n), F32),
        compiler_params=_cparams(("parallel", "parallel")),
        name="mod_vectors",
    )(cond, w_mod, b_mod.reshape(depth, 1, n))


def _ln_rows(x):
    mu = jnp.mean(x, axis=-1, keepdims=True)
    xc = x - mu
    var = jnp.mean(xc * xc, axis=-1, keepdims=True)
    return xc * lax.rsqrt(var + LN_EPS)


def _modln_body(x_ref, sh_ref, sc_ref, h_ref):
    h_ref[...] = (_ln_rows(x_ref[...]) * (1.0 + sc_ref[...]) + sh_ref[...]).astype(h_ref.dtype)


def modulate_ln(x, shift, scale, *, rows_per_group, m_rows=None, tr=256):
    M = x.shape[0] if m_rows is None else m_rows
    d = x.shape[1]
    assert M % tr == 0 and rows_per_group % tr == 0
    gmap = lambda i: ((i * tr) // rows_per_group, 0, 0)
    return pl.pallas_call(
        _modln_body,
        grid=(M // tr,),
        in_specs=[pl.BlockSpec((tr, d), lambda i: (i, 0)),
                  pl.BlockSpec((None, 1, d), gmap),
                  pl.BlockSpec((None, 1, d), gmap)],
        out_specs=pl.BlockSpec((tr, d), lambda i: (i, 0)),
        out_shape=jax.ShapeDtypeStruct((M, d), BF16),
        compiler_params=_cparams(("parallel",)),
        name="modulate_ln",
    )(x, shift, scale)


def _resln_body(x_ref, y_ref, gate_ref, g_ref, b_ref, sh_ref, sc_ref, xo_ref, h_ref, *, alpha):
    xn = _ln_rows(alpha * x_ref[...] + gate_ref[...] * y_ref[...]) * g_ref[...] + b_ref[...]
    xo_ref[...] = xn
    h_ref[...] = (_ln_rows(xn) * (1.0 + sc_ref[...]) + sh_ref[...]).astype(h_ref.dtype)


def _resln_last_body(x_ref, y_ref, gate_ref, g_ref, b_ref, xo_ref, *, alpha):
    xo_ref[...] = _ln_rows(alpha * x_ref[...] + gate_ref[...] * y_ref[...]) * g_ref[...] + b_ref[...]


def residual_ln(x, y, gate, ln_g, ln_b, shift, scale, *, alpha, rows_per_group, m_rows=None, tr=256,
                h_dtype=None):
    M = x.shape[0] if m_rows is None else m_rows
    d = x.shape[1]
    assert M % tr == 0 and rows_per_group % tr == 0
    gmap = lambda i: ((i * tr) // rows_per_group, 0, 0)
    row = pl.BlockSpec((tr, d), lambda i: (i, 0))
    vec = pl.BlockSpec((1, d), lambda i: (0, 0))
    gvec = pl.BlockSpec((None, 1, d), gmap)
    if shift is None:
        return pl.pallas_call(
            functools.partial(_resln_last_body, alpha=alpha),
            grid=(M // tr,),
            in_specs=[row, row, gvec, vec, vec],
            out_specs=row,
            out_shape=jax.ShapeDtypeStruct((M, d), F32),
            compiler_params=_cparams(("parallel",)),
            name="residual_ln_last",
        )(x, y, gate, ln_g.reshape(1, d), ln_b.reshape(1, d)), None
    return pl.pallas_call(
        functools.partial(_resln_body, alpha=alpha),
        grid=(M // tr,),
        in_specs=[row, row, gvec, vec, vec, gvec, gvec],
        out_specs=[row, row],
        out_shape=[jax.ShapeDtypeStruct((M, d), F32), jax.ShapeDtypeStruct((M, d), h_dtype or BF16)],
        compiler_params=_cparams(("parallel",)),
        name="residual_ln",
    )(x, y, gate, ln_g.reshape(1, d), ln_b.reshape(1, d), shift, scale)


def _flash_body(qt_ref, *refs, seg_lens, tk):
    n_seg = len(seg_lens)
    kv_refs = refs[:2 * n_seg]
    o_ref = refs[2 * n_seg]
    acc_ref, st_a, st_b = refs[2 * n_seg + 1:]
    bufs = (st_a, st_b)
    qt = qt_ref[...]
    tq = qt.shape[1]
    acc_ref[...] = jnp.zeros_like(acc_ref)
    m = jnp.full((1, tq), -jnp.inf, F32)

    def scores(seg, c, tks):
        k = kv_refs[2 * seg][pl.ds(pl.multiple_of(c * tks, tks), tks), :]
        return jnp.dot(k, qt, preferred_element_type=F32)

    def absorb(st, seg, c, tks, m_old):
        vt = kv_refs[2 * seg + 1][:, pl.ds(pl.multiple_of(c * tks, tks), tks)]
        vt1 = jnp.concatenate([vt, jnp.ones((FLASH_ONES, tks), BF16)], axis=0)
        m_new = jnp.maximum(m_old, jnp.max(st, axis=0, keepdims=True))
        p = jnp.exp2(st - m_new).astype(BF16)
        corr = jnp.exp2(m_old - m_new)
        acc_ref[...] = corr * acc_ref[...] + jnp.dot(vt1, p, preferred_element_type=F32)
        return m_new

    def run_static(chunks, cur, m):
        for i, (seg, c, tks) in enumerate(chunks):
            if i + 1 < len(chunks):
                nseg, nc, ntks = chunks[i + 1]
                bufs[1 - cur][:ntks] = scores(nseg, nc, ntks)
            m = absorb(bufs[cur][:tks], seg, c, tks, m)
            cur = 1 - cur
        return m

    chunk_counts = [(s, min(tk, ln), ln // min(tk, ln)) for s, ln in enumerate(seg_lens)]
    s0, tk0, n0 = chunk_counts[0]
    rest = [(s, c, tks) for s, tks, n in chunk_counts[1:] for c in range(n)]
    if n0 >= 4 and n0 % 2 == 0:
        st_a[...] = scores(s0, 0, tk0)

        def pair(j, m):
            c0 = 2 * j
            st_b[...] = scores(s0, c0 + 1, tk0)
            m = absorb(st_a[...], s0, c0, tk0, m)
            st_a[...] = scores(s0, c0 + 2, tk0)
            return absorb(st_b[...], s0, c0 + 1, tk0, m)

        m = lax.fori_loop(0, n0 // 2 - 1, pair, m)
        m = run_static([(s0, n0 - 2, tk0), (s0, n0 - 1, tk0)] + rest, 0, m)
    else:
        chunks = [(s0, c, tk0) for c in range(n0)] + rest
        st_a[:tk0] = scores(s0, 0, tk0)
        m = run_static(chunks, 0, m)
    acc = acc_ref[...]
    o_ref[...] = (acc[:V_HEAD] / acc[V_HEAD:V_HEAD + 1]).T.astype(o_ref.dtype)


def flash_attention(qt, k, vt, *, n_batch, q_row0, q_len, segs, tq, tk, name="flash"):
    h = N_HEADS
    nq = q_len // tq
    assert q_len % tq == 0 and q_row0 % tq == 0
    in_specs = [pl.BlockSpec((HEAD_PAD, tq), lambda b, hh, i: (hh, q_row0 // tq + b * nq + i))]
    args = [qt]
    for row0, ln in segs:
        assert row0 % ln == 0
        in_specs.append(pl.BlockSpec((ln, HEAD_PAD), lambda b, hh, i, r=row0 // ln: (r + b, hh)))
        in_specs.append(pl.BlockSpec((V_HEAD, ln), lambda b, hh, i, r=row0 // ln: (hh, r + b)))
        args += [k, vt]
    body = functools.partial(_flash_body, seg_lens=tuple(ln for _, ln in segs), tk=tk)
    return pl.pallas_call(
        body,
        grid=(n_batch, h, nq),
        in_specs=in_specs,
        out_specs=pl.BlockSpec((tq, V_HEAD), lambda b, hh, i: (b * nq + i, hh)),
        out_shape=jax.ShapeDtypeStruct((n_batch * q_len, h * V_HEAD), BF16),
        scratch_shapes=[pltpu.VMEM((V_HEAD + FLASH_ONES, tq), F32),
                        pltpu.VMEM((tk, tq), F32), pltpu.VMEM((tk, tq), F32)],
        compiler_params=_cparams(("parallel", "parallel", "arbitrary")),
        name=name,
    )(*args)


def _chunk_rows(x_ref):
    return jnp.concatenate([x_ref[:, t, :] for t in range(S5_L)], axis=1).astype(BF16)


def _s5_drive_body(x_ref, w_ref, o_ref):
    res = jnp.dot(_chunk_rows(x_ref), w_ref[...], preferred_element_type=F32)
    for gl in range(S5_SG):
        o_ref[:, gl, :] = res[:, gl * 4 * S5_STATE:(gl + 1) * 4 * S5_STATE]


def _s5_out_body(x_ref, sf_ref, sb_ref, t_ref, q_ref, o_ref):
    s = jnp.concatenate([sf_ref[:, gl, :] for gl in range(S5_SG)]
                        + [sb_ref[:, gl, :] for gl in range(S5_SG)], axis=1).astype(BF16)
    res = (jnp.dot(_chunk_rows(x_ref), t_ref[...], preferred_element_type=F32)
           + jnp.dot(s, q_ref[...], preferred_element_type=F32))
    for t in range(S5_L):
        o_ref[:, t, :] = res[:, t * LANE:(t + 1) * LANE]


def _s5_scan_body(wf_ref, wb_ref, af_ref, bf_ref, ab_ref, bb_ref, sf_ref, sb_ref, st_f, st_b):
    @pl.when(pl.program_id(1) == 0)
    def _():
        st_f[...] = jnp.zeros_like(st_f)
        st_b[...] = jnp.zeros_like(st_b)

    a_f, b_f, a_b, b_b = af_ref[...], bf_ref[...], ab_ref[...], bb_ref[...]
    cb = wf_ref.shape[0]

    def step(c, carry):
        s_f, s_b = carry
        cr = cb - 1 - c
        sf_ref[c] = s_f
        sb_ref[cr] = s_b
        n_f = a_f * s_f + b_f * pltpu.roll(s_f, S5_STATE, 1) + wf_ref[c]
        n_b = a_b * s_b + b_b * pltpu.roll(s_b, S5_STATE, 1) + wb_ref[cr]
        return n_f, n_b

    s_f, s_b = lax.fori_loop(0, cb, step, (st_f[...], st_b[...]))
    st_f[...] = s_f
    st_b[...] = s_b


def s5_scan(w3, lam, *, n_batch, n_xc, n_cc, cb):
    nch, g, _ = w3.shape
    p2 = 2 * S5_STATE
    n_xb, n_cb = n_xc // cb, n_cc // cb
    ctx0 = n_batch * n_xb

    def fwd_blk(b, j):
        return jnp.where(j < n_cb, ctx0 + b * n_cb + j, b * n_xb + j - n_cb)

    def bwd_blk(b, j):
        return jnp.where(j < n_cb, ctx0 + b * n_cb + (n_cb - 1 - j), b * n_xb + (n_xb - 1 - (j - n_cb)))

    coef = pl.BlockSpec((g, p2), lambda b, j: (0, 0))
    return pl.pallas_call(
        _s5_scan_body,
        grid=(n_batch, n_xb + n_cb),
        in_specs=[pl.BlockSpec((cb, g, p2), lambda b, j: (fwd_blk(b, j), 0, 0)),
                  pl.BlockSpec((cb, g, p2), lambda b, j: (bwd_blk(b, j), 0, 1)),
                  coef, coef, coef, coef],
        out_specs=[pl.BlockSpec((cb, g, p2), lambda b, j: (fwd_blk(b, j), 0, 0)),
                   pl.BlockSpec((cb, g, p2), lambda b, j: (bwd_blk(b, j), 0, 0))],
        out_shape=[jax.ShapeDtypeStruct((nch, g, p2), F32)] * 2,
        scratch_shapes=[pltpu.VMEM((g, p2), F32), pltpu.VMEM((g, p2), F32)],
        compiler_params=_cparams(("arbitrary", "arbitrary")),
        name="s5_scan",
    )(w3, w3, *lam)


def _s5_tables(a_re, a_im, log_dt, b_re, b_im, c_re, c_im):
    L, P, Hh = S5_L, S5_STATE, S5_GROUP
    hp = lax.Precision.HIGHEST
    dt = jnp.exp(log_dt.astype(F32))[..., None]
    ar, ai = a_re.astype(F32), a_im.astype(F32)
    j = jnp.arange(L + 1, dtype=F32)[:, None, None, None]
    mag = jnp.exp(j * ar * dt)
    pr, pi = mag * jnp.cos(j * ai * dt), mag * jnp.sin(j * ai * dt)
    lr, li = pr[1], pi[1]
    nr = lr - 1.0
    den = ar * ar + ai * ai
    f_re = ((nr * ar + li * ai) / den)[..., None]
    f_im = ((li * ar - nr * ai) / den)[..., None]
    br, bi = b_re.astype(F32), b_im.astype(F32)
    bb_re = f_re * br - f_im * bi
    bb_im = f_re * bi + f_im * br
    cr, ci = c_re.astype(F32), c_im.astype(F32)

    zr = pr[:L, ..., None] * bb_re - pi[:L, ..., None] * bb_im
    zi = pr[:L, ..., None] * bb_im + pi[:L, ..., None] * bb_re
    kj = (jnp.einsum('dghp,jdgpk->jdghk', cr, zr, precision=hp)
          - jnp.einsum('dghp,jdgpk->jdghk', ci, zi, precision=hp))
    t_idx = jnp.arange(L)
    lag = t_idx[None, :] - t_idx[:, None]
    kf = kj[:, 0][jnp.clip(lag, 0, L - 1)]
    kb = kj[:, 1][jnp.clip(-lag, 0, L - 1)]
    tm4 = (jnp.where((lag >= 0)[:, :, None, None, None], kf, 0.0)
           + jnp.where((lag <= 0)[:, :, None, None, None], kb, 0.0))
    g = tm4.shape[2]
    tmat = jnp.transpose(tm4, (2, 0, 4, 1, 3)).reshape(g, L * Hh, L * Hh)

    kk = jnp.arange(L)
    pf_r, pf_i = pr[L - 1 - kk, 0], pi[L - 1 - kk, 0]
    pb_r, pb_i = pr[kk, 1], pi[kk, 1]
    wf_re = pf_r[..., None] * bb_re[0] - pf_i[..., None] * bb_im[0]
    wf_im = pf_r[..., None] * bb_im[0] + pf_i[..., None] * bb_re[0]
    wb_re = pb_r[..., None] * bb_re[1] - pb_i[..., None] * bb_im[1]
    wb_im = pb_r[..., None] * bb_im[1] + pb_i[..., None] * bb_re[1]
    wcat = jnp.concatenate([wf_re, wf_im, wb_re, wb_im], axis=2)
    wmat = jnp.transpose(wcat, (1, 0, 3, 2)).reshape(g, L * Hh, 4 * P)

    qf_r, qf_i = pr[kk + 1, 0], pi[kk + 1, 0]
    qb_r, qb_i = pr[L - kk, 1], pi[L - kk, 1]

    def qpair(c_r, c_i, q_r, q_i):
        return (c_r[None] * q_r[:, :, None, :] - c_i[None] * q_i[:, :, None, :],
                -c_r[None] * q_i[:, :, None, :] - c_i[None] * q_r[:, :, None, :])

    qf_re, qf_im = qpair(cr[0], ci[0], qf_r, qf_i)
    qb_re, qb_im = qpair(cr[1], ci[1], qb_r, qb_i)
    qf = jnp.transpose(jnp.concatenate([qf_re, qf_im], axis=3), (1, 3, 0, 2))
    qb = jnp.transpose(jnp.concatenate([qb_re, qb_im], axis=3), (1, 3, 0, 2))

    n_in = L * S5_SG * Hh
    lane = jnp.arange(n_in)
    rep_t = (jnp.arange(L * Hh)[:, None] == ((lane // LANE) * Hh + lane % Hh)[None, :]).astype(BF16)
    rep_w = (jnp.arange(4 * P)[:, None] == (lane % (4 * P))[None, :]).astype(BF16)
    col_grp_t = (lane // Hh) % S5_SG
    col_grp_w = lane // (4 * P)
    wfull = s5_spread([wmat.astype(BF16)], rep_w, col_grp_w, chunk_rows=True)
    tfull = s5_spread([tmat.astype(BF16)], rep_t, col_grp_t, chunk_rows=True)
    qfull = s5_spread([qf.reshape(g, 2 * P, L * Hh).astype(BF16), qb.reshape(g, 2 * P, L * Hh).astype(BF16)],
                      rep_t, col_grp_t, chunk_rows=False)

    def lam_tiles(d):
        return (jnp.concatenate([pr[L, d], pr[L, d]], axis=-1), jnp.concatenate([-pi[L, d], pi[L, d]], axis=-1))

    lam = lam_tiles(0) + lam_tiles(1)
    return wfull, tfull, qfull, lam


def _s5_spread_body(*refs, n_src, chunk_rows):
    src_refs = refs[:n_src]
    rep_ref, cgrp_ref, o_ref = refs[n_src:]
    if chunk_rows:
        pieces = [src_refs[0][a, k * S5_GROUP:(k + 1) * S5_GROUP, :] for k in range(S5_L) for a in range(S5_SG)]
        per_grp = S5_GROUP
    else:
        pieces = [r[a] for r in src_refs for a in range(S5_SG)]
        per_grp = src_refs[0].shape[1]
    rows = jnp.concatenate(pieces, axis=0)
    full = jnp.dot(rows, rep_ref[...], preferred_element_type=F32)
    assert per_grp & (per_grp - 1) == 0 and S5_SG & (S5_SG - 1) == 0
    row_grp = (lax.broadcasted_iota(jnp.int32, full.shape, 0) >> (per_grp.bit_length() - 1)) & (S5_SG - 1)
    o_ref[...] = jnp.where(row_grp == cgrp_ref[...], full, 0.0).astype(o_ref.dtype)


def s5_spread(srcs, rep, col_grp, *, chunk_rows):
    g, r, c = srcs[0].shape
    n = rep.shape[1]
    n_rows = len(srcs) * S5_SG * r
    return pl.pallas_call(
        functools.partial(_s5_spread_body, n_src=len(srcs), chunk_rows=chunk_rows),
        grid=(g // S5_SG,),
        in_specs=[pl.BlockSpec((S5_SG, r, c), lambda s: (s, 0, 0)) for _ in srcs]
        + [pl.BlockSpec((c, n), lambda s: (0, 0)), pl.BlockSpec((1, n), lambda s: (0, 0))],
        out_specs=pl.BlockSpec((None, n_rows, n), lambda s: (s, 0, 0)),
        out_shape=jax.ShapeDtypeStruct((g // S5_SG, n_rows, n), BF16),
        compiler_params=_cparams(("parallel",)),
        name="s5_spread",
    )(*srcs, rep, col_grp.astype(jnp.int32).reshape(1, n))


def s5_mix(u, tables, *, n_batch, seq, ctx_len):
    wfull, tfull, qfull, lam = tables
    L, P = S5_L, S5_STATE
    t_rows, w_tot = u.shape
    g = w_tot // S5_GROUP
    nsg = g // S5_SG
    nch = t_rows // L
    n_in = L * LANE
    rb = max(r for r in range(8, 265, 8) if nch % r == 0)
    u3 = u.reshape(nch, L, w_tot)
    xspec = pl.BlockSpec((rb, L, LANE), lambda s, i: (i, 0, s))
    wspec = lambda k, n: pl.BlockSpec((None, k, n), lambda s, i: (s, 0, 0))

    w3 = pl.pallas_call(
        _s5_drive_body,
        grid=(nsg, nch // rb),
        in_specs=[xspec, wspec(n_in, S5_SG * 4 * P)],
        out_specs=pl.BlockSpec((rb, S5_SG, 4 * P), lambda s, i: (i, s, 0)),
        out_shape=jax.ShapeDtypeStruct((nch, g, 4 * P), F32),
        compiler_params=_cparams(("parallel", "parallel")),
        name="s5_drive",
    )(u3, wfull)

    n_xc, n_cc = seq // L, ctx_len // L
    cb = _pick(math.gcd(n_xc, n_cc), (16, 8, 4, 2, 1))
    sf, sb = s5_scan(w3, lam, n_batch=n_batch, n_xc=n_xc, n_cc=n_cc, cb=cb)

    sspec = pl.BlockSpec((rb, S5_SG, 2 * P), lambda s, i: (i, s, 0))
    y3 = pl.pallas_call(
        _s5_out_body,
        grid=(nsg, nch // rb),
        in_specs=[xspec, sspec, sspec, wspec(n_in, n_in), wspec(S5_SG * 4 * P, n_in)],
        out_specs=xspec,
        out_shape=jax.ShapeDtypeStruct((nch, L, w_tot), F32),
        compiler_params=_cparams(("parallel", "parallel")),
        name="s5_out",
    )(u3, sf, sb, tfull, qfull)
    return y3.reshape(t_rows, w_tot)


R_E1, R_E2, R_W1, R_W2, R_RANK1, R_RANK2 = range(6)


def _router_body(lg_ref, b_ref, meta_ref, cnt_ref, carry_ref, *, n_exp):
    @pl.when(pl.program_id(0) == 0)
    def _():
        carry_ref[...] = jnp.zeros_like(carry_ref)

    lg = lg_ref[...] + b_ref[...]
    tr = lg.shape[0]
    lane = lax.broadcasted_iota(jnp.int32, lg.shape, 1).astype(F32)
    neg = jnp.float32(-jnp.inf)
    lg = jnp.where(lane < n_exp, lg, neg)
    m1 = jnp.max(lg, axis=-1, keepdims=True)
    i1 = jnp.min(jnp.where(lg == m1, lane, float(LANE)), axis=-1, keepdims=True)
    lg2 = jnp.where(lane == i1, neg, lg)
    m2 = jnp.max(lg2, axis=-1, keepdims=True)
    i2 = jnp.min(jnp.where(lg2 == m2, lane, float(LANE)), axis=-1, keepdims=True)
    e2 = jnp.exp(m2 - m1)
    den = 1.0 + e2
    sel = jnp.where((lane == i1) | (lane == i2), 1.0, 0.0)
    r_i = lax.broadcasted_iota(jnp.int32, (tr, tr), 0)
    c_i = lax.broadcasted_iota(jnp.int32, (tr, tr), 1)
    tri = jnp.where(r_i > c_i, 1.0, 0.0).astype(BF16)
    before = jnp.dot(tri, sel.astype(BF16), preferred_element_type=F32) + carry_ref[...]
    rank1 = jnp.sum(jnp.where(lane == i1, before, 0.0), axis=-1, keepdims=True)
    rank2 = jnp.sum(jnp.where(lane == i2, before, 0.0), axis=-1, keepdims=True)
    total = carry_ref[...] + jnp.sum(sel, axis=0, keepdims=True)
    carry_ref[...] = total
    cnt_ref[...] = jnp.broadcast_to(total, cnt_ref.shape)
    meta = jnp.zeros_like(lg)
    for idx, val in ((R_E1, i1), (R_E2, i2), (R_W1, 1.0 / den), (R_W2, e2 / den),
                     (R_RANK1, rank1), (R_RANK2, rank2)):
        meta = jnp.where(lane == idx, val, meta)
    meta_ref[...] = meta


def router_top2(logits, b_router_pad, *, n_exp, tr=512):
    m = logits.shape[0]
    tr = _pick(m, (tr, 256, 128, 64, 32, 16, 8))
    return pl.pallas_call(
        functools.partial(_router_body, n_exp=n_exp),
        grid=(m // tr,),
        in_specs=[pl.BlockSpec((tr, LANE), lambda i: (i, 0)), pl.BlockSpec((1, LANE), lambda i: (0, 0))],
        out_specs=[pl.BlockSpec((tr, LANE), lambda i: (i, 0)), pl.BlockSpec((8, LANE), lambda i: (0, 0))],
        out_shape=[jax.ShapeDtypeStruct((m, LANE), F32), jax.ShapeDtypeStruct((8, LANE), F32)],
        scratch_shapes=[pltpu.VMEM((1, LANE), F32)],
        compiler_params=_cparams(("arbitrary",)),
        name="router_top2",
    )(logits, b_router_pad)


def _row_copy_wait(src_ref, dst_ref, sem):
    pltpu.make_async_copy(src_ref.at[pl.ds(0, 1), :], dst_ref.at[pl.ds(0, 1), :], sem).wait()


def _gather_rows_body(idx_ref, src_ref, o_ref, buf, sem, *, tb):
    def issue(t, carry):
        pltpu.make_async_copy(src_ref.at[pl.ds(idx_ref[0, t], 1), :], buf.at[pl.ds(t, 1), :], sem).start()
        return carry

    lax.fori_loop(0, tb, issue, 0)

    def drain(t, carry):
        _row_copy_wait(src_ref, buf, sem)
        return carry

    lax.fori_loop(0, tb, drain, 0)
    o_ref[...] = buf[...].astype(o_ref.dtype)


def gather_rows(src, idx, *, out_dtype, tb=256):
    r, d = idx.shape[0], src.shape[1]
    tb = _pick(r, (tb, 128, 64, 32, 16, 8))
    return pl.pallas_call(
        functools.partial(_gather_rows_body, tb=tb),
        grid=(r // tb,),
        in_specs=[pl.BlockSpec((None, 1, tb), lambda i: (i, 0, 0), memory_space=pltpu.SMEM),
                  pl.BlockSpec(memory_space=pl.ANY)],
        out_specs=pl.BlockSpec((tb, d), lambda i: (i, 0)),
        out_shape=jax.ShapeDtypeStruct((r, d), out_dtype),
        scratch_shapes=[pltpu.VMEM((tb, d), src.dtype), pltpu.SemaphoreType.DMA(())],
        compiler_params=_cparams(("arbitrary",)),
        name="moe_gather_rows",
    )(idx.reshape(r // tb, 1, tb), src)


def _gather_combine_body(pos_ref, meta_ref, y_ref, o_ref, buf, sem, *, tb):
    def issue(t, carry):
        for k in range(TOP_K):
            pltpu.make_async_copy(y_ref.at[pl.ds(pos_ref[0, TOP_K * t + k], 1), :],
                                  buf.at[k, pl.ds(t, 1), :], sem).start()
        return carry

    lax.fori_loop(0, tb, issue, 0)

    def drain(t, carry):
        for k in range(TOP_K):
            _row_copy_wait(y_ref, buf.at[0], sem)
        return carry

    lax.fori_loop(0, tb, drain, 0)
    meta = meta_ref[...]
    o_ref[...] = meta[:, R_W1:R_W1 + 1] * buf[0] + meta[:, R_W2:R_W2 + 1] * buf[1]


def gather_combine(y, pos, meta, *, tb=256):
    m, d = pos.shape[0], y.shape[1]
    tb = _pick(m, (tb, 128, 64, 32, 16, 8))
    pos3 = pos.reshape(m // tb, 1, TOP_K * tb)
    return pl.pallas_call(
        functools.partial(_gather_combine_body, tb=tb),
        grid=(m // tb,),
        in_specs=[pl.BlockSpec((None, 1, TOP_K * tb), lambda i: (i, 0, 0), memory_space=pltpu.SMEM),
                  pl.BlockSpec((tb, LANE), lambda i: (i, 0)),
                  pl.BlockSpec(memory_space=pl.ANY)],
        out_specs=pl.BlockSpec((tb, d), lambda i: (i, 0)),
        out_shape=jax.ShapeDtypeStruct((m, d), F32),
        scratch_shapes=[pltpu.VMEM((TOP_K, tb, d), F32), pltpu.SemaphoreType.DMA(())],
        compiler_params=_cparams(("arbitrary",)),
        name="moe_gather_combine",
    )(pos3, meta, y)


def _gmm_up_body(te_ref, nv_ref, a_ref, b1_ref, b3_ref, o_ref):
    del te_ref
    live = pl.program_id(0) < nv_ref[0]

    @pl.when(live)
    def _():
        a = a_ref[...]
        g = jnp.dot(a, b1_ref[...].astype(BF16), preferred_element_type=F32)
        u = jnp.dot(a, b3_ref[...].astype(BF16), preferred_element_type=F32)
        o_ref[...] = (g * jax.nn.sigmoid(g) * u).astype(o_ref.dtype)

    @pl.when(jnp.logical_not(live))
    def _():
        o_ref[...] = jnp.zeros_like(o_ref)


def _gmm_down_body(te_ref, nv_ref, a_ref, b_ref, o_ref):
    del te_ref
    live = pl.program_id(0) < nv_ref[0]

    @pl.when(live)
    def _():
        o_ref[...] = jnp.dot(a_ref[...], b_ref[...].astype(BF16), preferred_element_type=F32)

    @pl.when(jnp.logical_not(live))
    def _():
        o_ref[...] = jnp.zeros_like(o_ref)


def grouped_swiglu(xs, w1, w3, w2, tile_expert, n_valid, *, tm):
    r, d = xs.shape
    f = w1.shape[2]
    n_tiles = r // tm
    tn_up = _pick(f, (512, 256, 128))
    tn_dn = _pick(d, (1024, 512, 256, 128))
    act = pl.pallas_call(
        _gmm_up_body,
        grid_spec=pltpu.PrefetchScalarGridSpec(
            num_scalar_prefetch=2,
            grid=(n_tiles, f // tn_up),
            in_specs=[pl.BlockSpec((tm, d), lambda i, j, te, nv: (i, 0)),
                      pl.BlockSpec((None, d, tn_up), lambda i, j, te, nv: (te[i], 0, j)),
                      pl.BlockSpec((None, d, tn_up), lambda i, j, te, nv: (te[i], 0, j))],
            out_specs=pl.BlockSpec((tm, tn_up), lambda i, j, te, nv: (i, j))),
        out_shape=jax.ShapeDtypeStruct((r, f), BF16),
        compiler_params=_cparams(("arbitrary", "arbitrary")),
        name="moe_up",
    )(tile_expert, n_valid, xs, w1, w3)
    return pl.pallas_call(
        _gmm_down_body,
        grid_spec=pltpu.PrefetchScalarGridSpec(
            num_scalar_prefetch=2,
            grid=(n_tiles, d // tn_dn),
            in_specs=[pl.BlockSpec((tm, f), lambda i, j, te, nv: (i, 0)),
                      pl.BlockSpec((None, f, tn_dn), lambda i, j, te, nv: (te[i], 0, j))],
            out_specs=pl.BlockSpec((tm, tn_dn), lambda i, j, te, nv: (i, j))),
        out_shape=jax.ShapeDtypeStruct((r, d), F32),
        compiler_params=_cparams(("arbitrary", "arbitrary")),
        name="moe_down",
    )(tile_expert, n_valid, act, w2)


def sparse_moe(hf, meta, counts, w1, w3, w2, *, tm=512):
    m = hf.shape[0]
    n_exp = w1.shape[0]
    tm = _pick(m, (tm, 256, 128))
    cnt = counts.astype(jnp.int32)
    padded = (cnt + tm - 1) // tm * tm
    ends = jnp.cumsum(padded)
    offs = ends - padded
    e1 = meta[:, R_E1].astype(jnp.int32)
    e2 = meta[:, R_E2].astype(jnp.int32)
    pos = jnp.stack([offs[e1] + meta[:, R_RANK1].astype(jnp.int32),
                     offs[e2] + meta[:, R_RANK2].astype(jnp.int32)], axis=1)
    n_tiles = TOP_K * m // tm + n_exp
    tile_expert = jnp.minimum(jnp.searchsorted(ends, jnp.arange(n_tiles) * tm, side="right"),
                              n_exp - 1).astype(jnp.int32)
    n_valid = (ends[-1:] // tm).astype(jnp.int32)
    src_tok = jnp.zeros((n_tiles * tm,), jnp.int32).at[pos.reshape(-1)].set(
        jnp.repeat(jnp.arange(m, dtype=jnp.int32), TOP_K))
    xs = gather_rows(hf, src_tok, out_dtype=BF16)
    y = grouped_swiglu(xs, w1, w3, w2, tile_expert, n_valid, tm=tm)
    return gather_combine(y, pos, meta)


def _rms_pro(x, gain):
    return x * lax.rsqrt(jnp.mean(x * x, axis=-1, keepdims=True) + RMS_EPS) * gain


def _gelu_tanh(x):
    return 0.5 * x * (1.0 + jnp.tanh(math.sqrt(2.0 / math.pi) * (x + 0.044715 * (x * x * x))))


def _rope_apply(x, cos, sin_up, sin_dn):
    n = x.shape[-1]
    return x * cos + pltpu.roll(x, n - QK_ROPE // 4, 1) * sin_up + pltpu.roll(x, QK_ROPE // 4, 1) * sin_dn


def _rope_heads_t(x, cos, sin_up, sin_dn):
    q4 = QK_ROPE // 4
    parts = []
    for base in range(0, x.shape[0], HEAD_PAD):
        seg = x[base + QK_NOPE:base + QK_NOPE + QK_ROPE]
        rot = seg * cos + pltpu.roll(seg, QK_ROPE - q4, 0) * sin_up + pltpu.roll(seg, q4, 0) * sin_dn
        parts += [x[base:base + QK_NOPE], rot, x[base + QK_NOPE + QK_ROPE:base + HEAD_PAD]]
    return jnp.concatenate(parts, axis=0)


def _rope_tables(n_batch, seq, n_ctx_rows):
    nf = QK_ROPE // 4
    pos = jnp.arange(seq)
    row = (pos // GRID_W).astype(F32)
    col = (pos % GRID_W).astype(F32)
    inv = ROPE_THETA ** (-jnp.arange(nf, dtype=F32) / nf)
    ar, ac = row[:, None] * inv, col[:, None] * inv
    z = jnp.zeros((seq, nf), F32)
    cos64 = jnp.concatenate([jnp.cos(ar), jnp.cos(ar), jnp.cos(ac), jnp.cos(ac)], axis=1)
    up64 = jnp.concatenate([-jnp.sin(ar), z, -jnp.sin(ac), z], axis=1)
    dn64 = jnp.concatenate([z, jnp.sin(ar), z, jnp.sin(ac)], axis=1)

    def place(t64, fill):
        full = jnp.full((seq, HEAD_PAD), fill, F32).at[:, QK_NOPE:QK_NOPE + QK_ROPE].set(t64)
        full = jnp.tile(full, (n_batch, 1))
        return jnp.concatenate([full, jnp.full((n_ctx_rows, HEAD_PAD), fill, F32)], axis=0)

    return place(cos64, 1.0), place(up64, 0.0), place(dn64, 0.0)


def _pad_cols(w, n):
    return jnp.pad(w, ((0, 0), (0, n - w.shape[1])))


def _head_cat_cols(w_a, w_b, da, db):
    k = w_a.shape[0]
    parts = [w_a.reshape(k, N_HEADS, da)]
    if w_b is not None:
        parts.append(w_b.reshape(k, N_HEADS, db))
    used = da + (db if w_b is not None else 0)
    parts.append(jnp.zeros((k, N_HEADS, HEAD_PAD - used), w_a.dtype))
    return jnp.concatenate(parts, axis=2).reshape(k, N_HEADS * HEAD_PAD)


def kernel(x, c, ctx, c_ctx, w_mod, b_mod, w_in, b_gate, q_norm, w_uq, kv_norm, w_ukv, w_branch_mla,
           s5_a_re, s5_a_im, s5_log_dt, s5_b_re, s5_b_im, s5_c_re, s5_c_im, s5_d, w_glu, b_glu,
           w_branch_s5, w_out, ln_mix_g, ln_mix_b, ln_ffn_g, ln_ffn_b, ffn_w1, ffn_w3, ffn_w2,
           moe_w_router, moe_b_router, moe_w1, moe_w3, moe_w2):
    B, N, D = x.shape
    C = ctx.shape[1]
    depth = w_mod.shape[0]
    QL, KL = q_norm.shape[1], kv_norm.shape[1]
    SW = s5_d.shape[1]
    H = N_HEADS
    NX, NC_ROWS = B * N, B * C
    T = NX + NC_ROWS
    alpha = (2 * depth) ** 0.25
    q_scale = (QK_NOPE + QK_ROPE) ** -0.5 * math.log2(math.e)
    o_ckv, o_kr, o_u, o_g = QL, QL + KL, QL + KL + QK_ROPE, QL + KL + QK_ROPE + SW
    assert N % C == 0 and N % 256 == 0 and NC_ROWS % 256 == 0

    tm_all = _pick(T, (1536, 1024, 768, 512, 384, 256, 128))
    tm_x = _pick(NX, (1024, 512, 256, 128))
    tile_n = lambda n: _pick(n, (512, 256, 128))

    n_cond = B + 1
    cond = jnp.concatenate([c, c_ctx[None], jnp.zeros((-n_cond % 8, D), F32)], axis=0)
    mods = mod_vectors(cond, w_mod, b_mod)
    mods = mods.reshape(depth, cond.shape[0], 6, D)

    def mvec(l, k):
        return mods[l, :n_cond, k][:, None, :]

    rope_cos, rope_up, rope_dn = _rope_tables(B, N, NC_ROWS)
    rope_cos_t, rope_up_t, rope_dn_t = (t[:, QK_NOPE:QK_NOPE + QK_ROPE].T for t in (rope_cos, rope_up, rope_dn))
    xt =jnp.concatenate([x.reshape(NX, D), ctx.reshape(NC_ROWS, D)], axis=0)
    h = modulate_ln(xt, mvec(0, 0), mvec(0, 1), rows_per_group=N)

    for l in range(depth):
        need_ctx = l < depth - 1
        rows = T if need_ctx else NX
        tm_r = tm_all if need_ctx else tm_x

        wi = w_in[l]
        w_cq = wi[:, :o_ckv].astype(BF16)
        w_ckv = wi[:, o_ckv:o_kr].astype(BF16)
        w_kr = jnp.concatenate(
            [jnp.zeros((D, QK_NOPE), F32), wi[:, o_kr:o_u],
             jnp.zeros((D, HEAD_PAD - QK_NOPE - QK_ROPE), F32)], axis=1).astype(BF16)
        w_u = wi[:, o_u:o_g].astype(BF16)
        w_gm = wi[:, o_g:o_g + D].astype(BF16)
        w_gs = wi[:, o_g + D:].astype(BF16)
        wq = w_uq[l].reshape(QL, H, QK_NOPE + QK_ROPE)
        w_q = _head_cat_cols(wq[:, :, :QK_NOPE].reshape(QL, -1), wq[:, :, QK_NOPE:].reshape(QL, -1),
                             QK_NOPE, QK_ROPE).astype(BF16)
        wkv = w_ukv[l].reshape(KL, H, QK_NOPE + V_HEAD)
        w_k = _head_cat_cols(wkv[:, :, :QK_NOPE].reshape(KL, -1), None, QK_NOPE, 0).astype(BF16)
        w_v = wkv[:, :, QK_NOPE:].reshape(KL, H * V_HEAD).astype(BF16)

        cqn = matmul([h], [w_cq], out_dtype=BF16, tm=tm_r // 2, tn=QL, m_rows=rows,
                     epilogue=_rms_pro, extras=((q_norm[l].reshape(1, QL), "col"),), name="in_cq")
        ckvn = matmul([h], [w_ckv], out_dtype=BF16, tm=tm_all, tn=KL,
                      epilogue=_rms_pro, extras=((kv_norm[l].reshape(1, KL), "col"),), name="in_ckv")
        krp = matmul([h], [w_kr], out_dtype=F32, tm=tm_all, tn=HEAD_PAD, epilogue=_rope_apply,
                     extras=((rope_cos, "rowtab"), (rope_up, "rowtab"), (rope_dn, "rowtab")), name="in_kr")
        u = matmul([h], [w_u], out_dtype=F32, tm=tm_all, tn=tile_n(SW), name="in_s5")

        qt = matmul([w_q.T], [cqn], nt=True, out_dtype=BF16, tm=_pick(H * HEAD_PAD, (1024, 512, 256)), tn=512,
                    n_cols=rows, epilogue=lambda acc, cs, up, dn: _rope_heads_t(acc, cs, up, dn) * q_scale,
                    extras=((rope_cos_t, "coltab"), (rope_up_t, "coltab"), (rope_dn_t, "coltab")),
                    name="mla_qt")
        kh = matmul([ckvn], [w_k], out_dtype=BF16, tm=tm_all, tn=512,
                    epilogue=lambda acc, kr: acc + jnp.tile(kr, (1, acc.shape[1] // HEAD_PAD)),
                    extras=((krp, "rowtab"),), name="mla_k")
        vt = matmul([w_v.T], [ckvn], nt=True, out_dtype=BF16, tm=_pick(H * V_HEAD, (1024, 512, 256, 128)),
                    tn=512, name="mla_vt")
        tq = _pick(N, (512, 256, 128))
        o_x = flash_attention(qt, kh, vt, n_batch=B, q_row0=0, q_len=N, segs=[(0, N), (NX, C)],
                              tq=tq, tk=1024, name="flash_x")
        if need_ctx:
            o_c = flash_attention(qt, kh, vt, n_batch=B, q_row0=NX, q_len=C, segs=[(NX, C)],
                                  tq=_pick(C, (256, 128)), tk=512, name="flash_ctx")
            o_all = jnp.concatenate([o_x, o_c], axis=0)
        else:
            o_all = o_x

        tables = _s5_tables(s5_a_re[l], s5_a_im[l], s5_log_dt[l], s5_b_re[l], s5_b_im[l],
                            s5_c_re[l], s5_c_im[l])
        y = s5_mix(u, tables, n_batch=B, seq=N, ctx_len=C)
        d_row = s5_d[l].reshape(1, SW)
        glu_pro = lambda yv, uv, dv: _gelu_tanh(yv + dv * uv)
        ys = matmul([y, u], [w_glu[l].astype(BF16)], out_dtype=BF16, tm=tm_r // 2, tn=tile_n(SW), m_rows=rows,
                    prologue=glu_pro, pro_consts=(d_row,),
                    epilogue=lambda acc, yv, uv, dv, bv: (lambda gg: gg * jax.nn.sigmoid(acc + bv))(
                        _gelu_tanh(yv + dv * uv)),
                    extras=((y, "tile"), (u, "tile"), (d_row, "col"), (b_glu[l].reshape(1, SW), "col")),
                    name="s5_glu")

        bg = b_gate[l]
        m1 = matmul([o_all], [w_branch_mla[l].astype(BF16)], out_dtype=F32, tm=tm_r, tn=512, m_rows=rows,
                    name="branch_mla")
        gm = matmul([h], [w_gm], out_dtype=F32, tm=tm_r, tn=512, m_rows=rows,
                    epilogue=lambda acc, bv, mv: jax.nn.sigmoid(acc + bv) * mv,
                    extras=((bg[:D].reshape(1, D), "col"), (m1, "tile")), name="gate_mla")
        m2 = matmul([ys], [w_branch_s5[l].astype(BF16)], out_dtype=F32, tm=tm_r, tn=512, m_rows=rows,
                    name="branch_s5")
        merged = matmul([h], [w_gs], out_dtype=BF16, tm=tm_r, tn=512, m_rows=rows,
                        epilogue=lambda acc, bv, mv, pv: jax.nn.sigmoid(acc + bv) * mv + pv,
                        extras=((bg[D:].reshape(1, D), "col"), (m2, "tile"), (gm, "tile")), name="gate_s5")
        mix = matmul([merged], [w_out[l].astype(BF16)], out_dtype=F32, tm=tm_r, tn=512, m_rows=rows,
                     name="out_proj")
        xt, h2 = residual_ln(xt, mix, mvec(l, 2), ln_mix_g[l], ln_mix_b[l], mvec(l, 3), mvec(l, 4),
                             alpha=alpha, rows_per_group=N, m_rows=rows, h_dtype=BF16 if l % 2 == 0 else F32)

        if l % 2 == 0:
            fi = l // 2
            dff = ffn_w1.shape[2]
            dff_p = -(-dff // 512) * 512
            w1 = _pad_cols(ffn_w1[fi], dff_p).astype(BF16)
            w3 = _pad_cols(ffn_w3[fi], dff_p).astype(BF16)
            w2 = jnp.pad(ffn_w2[fi], ((0, dff_p - dff), (0, 0))).astype(BF16)
            act = matmul([h2], [w1, w3], out_dtype=BF16, tm=tm_r, tn=256, m_rows=rows,
                         epilogue=lambda a, b: a * jax.nn.sigmoid(a) * b, name="ffn_up")
            tk = _pick(dff_p, (2816, 2048, 1024, 512))
            ff = matmul_ksplit(act, w2, out_dtype=F32, tm=tm_r, tn=512, tk=tk, m_rows=rows, name="ffn_down")
        else:
            mi = l // 2
            n_exp = moe_w_router.shape[2]
            w_r = _pad_cols(moe_w_router[mi], LANE).astype(BF16)
            b_r = jnp.pad(moe_b_router[mi], (0, LANE - n_exp)).reshape(1, LANE)
            logits = matmul([h2], [w_r], out_dtype=F32, tm=tm_r // 2, tn=LANE, m_rows=rows,
                            prologue=lambda a: a, name="router_logits")
            meta, counts = router_top2(logits, b_r, n_exp=n_exp)
            ff = sparse_moe(h2, meta, counts[0, :n_exp], moe_w1[mi], moe_w3[mi], moe_w2[mi])
        if need_ctx:
            xt, h = residual_ln(xt, ff, mvec(l, 5), ln_ffn_g[l], ln_ffn_b[l], mvec(l + 1, 0), mvec(l + 1, 1),
                                alpha=alpha, rows_per_group=N, m_rows=rows)
        else:
            xt, _ = residual_ln(xt, ff, mvec(l, 5), ln_ffn_g[l], ln_ffn_b[l], None, None,
                                alpha=alpha, rows_per_group=N, m_rows=rows)
    return xt[:NX].reshape(B, N, D)
```

```python
import functools
import math

import jax
import jax.numpy as jnp
from jax import lax
from jax.experimental import pallas as pl
from jax.experimental.pallas import tpu as pltpu

N_HEADS = 32
QK_NOPE = 128
QK_ROPE = 64
V_HEAD = 128
ROPE_THETA = 10000.0
GRID_W = 64
S5_GROUP = 16
S5_STATE = 64
TOP_K = 2
LN_EPS = 1e-6
RMS_EPS = 1e-6

HEAD_PAD = 256
S5_L = 16
FLASH_ONES = 16
LANE = 128
S5_SG = LANE // S5_GROUP
VMEM_LIMIT_BYTES = 56 * 2**20

F32 = jnp.float32
BF16 = jnp.bfloat16


def _cparams(sem):
    return pltpu.CompilerParams(dimension_semantics=sem, vmem_limit_bytes=VMEM_LIMIT_BYTES)


def _pick(n, prefs):
    for p in prefs:
        if n % p == 0:
            return p
    raise ValueError(f"no tile in {prefs} divides {n}")


def _mm_body(*refs, n_a, n_pc, n_b, kinds, prologue, epilogue, tn, nt, keep_pro):
    a_refs = refs[:n_a]
    pc_refs = refs[n_a:n_a + n_pc]
    b_refs = refs[n_a + n_pc:n_a + n_pc + n_b]
    ex_refs = refs[n_a + n_pc + n_b:n_a + n_pc + n_b + len(kinds)]
    o_ref = refs[n_a + n_pc + n_b + len(kinds)]
    pro_tile = []
    if prologue is not None:
        a_s = refs[n_a + n_pc + n_b + len(kinds) + 1]
        a_f = refs[n_a + n_pc + n_b + len(kinds) + 2] if keep_pro else None
        j = pl.program_id(1)

        @pl.when(j == 0)
        def _():
            val = prologue(*[r[...] for r in a_refs], *[r[...] for r in pc_refs])
            a_s[...] = val.astype(BF16)
            if keep_pro:
                a_f[...] = val

        a = a_s[...]
        if keep_pro:
            pro_tile = [a_f[:, pl.ds(pl.multiple_of(j * tn, tn), tn)]]
    else:
        a = a_refs[0][...]
    if nt:
        accs = [lax.dot_general(a, b[...], (((1,), (1,)), ((), ())), preferred_element_type=F32)
                for b in b_refs]
    else:
        accs = [jnp.dot(a, b[...].astype(BF16), preferred_element_type=F32) for b in b_refs]
    exs = []
    for r, kind in zip(ex_refs, kinds):
        v = r[...]
        if kind == "rowtab_tiled":
            v = jnp.tile(v, (1, tn // v.shape[1]))
        exs.append(v)
    o_ref[...] = epilogue(*accs, *exs, *pro_tile).astype(o_ref.dtype)


def matmul(a_list, b_list, *, out_dtype, tm, tn, m_rows=None, n_cols=None, nt=False, epilogue=None,
           extras=(), prologue=None, pro_consts=(), keep_pro=False, name="mm"):
    K = a_list[0].shape[1]
    N = b_list[0].shape[0 if nt else 1] if n_cols is None else n_cols
    M = a_list[0].shape[0] if m_rows is None else m_rows
    tn = _pick(N, tuple(t for t in (tn, 512, 256, 128) if t <= tn))
    assert M % tm == 0 and N % tn == 0, (M, tm, N, tn)
    if epilogue is None:
        epilogue = lambda acc: acc
    if prologue is None:
        assert len(a_list) == 1 and a_list[0].dtype == BF16
    in_specs = [pl.BlockSpec((tm, K), lambda i, j: (i, 0)) for _ in a_list]
    in_specs += [pl.BlockSpec(c.shape, lambda i, j: (0, 0)) for c in pro_consts]
    if nt:
        in_specs += [pl.BlockSpec((tn, K), lambda i, j: (j, 0)) for _ in b_list]
    else:
        in_specs += [pl.BlockSpec((K, tn), lambda i, j: (0, j)) for _ in b_list]
    kinds = []
    ex_arrays = []
    for arr, kind in extras:
        kinds.append(kind)
        ex_arrays.append(arr)
        if kind == "tile":
            in_specs.append(pl.BlockSpec((tm, tn), lambda i, j: (i, j)))
        elif kind == "col":
            in_specs.append(pl.BlockSpec((1, tn), lambda i, j: (0, j)))
        elif kind in ("rowtab", "rowtab_tiled"):
            in_specs.append(pl.BlockSpec((tm, arr.shape[1]), lambda i, j: (i, 0)))
        elif kind == "coltab":
            in_specs.append(pl.BlockSpec((arr.shape[0], tn), lambda i, j: (0, j)))
        else:
            raise ValueError(kind)
    scratch = [pltpu.VMEM((tm, K), BF16)] if prologue is not None else []
    if keep_pro:
        assert prologue is not None and N == K
        scratch.append(pltpu.VMEM((tm, K), F32))
    body = functools.partial(_mm_body, n_a=len(a_list), n_pc=len(pro_consts), n_b=len(b_list), kinds=tuple(kinds),
                             prologue=prologue, epilogue=epilogue, tn=tn, nt=nt, keep_pro=keep_pro)
    return pl.pallas_call(
        body,
        grid=(M // tm, N // tn),
        in_specs=in_specs,
        out_specs=pl.BlockSpec((tm, tn), lambda i, j: (i, j)),
        out_shape=jax.ShapeDtypeStruct((M, N), out_dtype),
        scratch_shapes=scratch,
        compiler_params=_cparams(("parallel", "arbitrary")),
        name=name,
    )(*a_list, *pro_consts, *b_list, *ex_arrays)


def _mmk_body(a_ref, b_ref, *rest, n_ex, epilogue, nk):
    ex_refs = rest[:n_ex]
    o_ref = rest[n_ex]
    acc_ref = rest[n_ex + 1]
    k = pl.program_id(2)

    @pl.when(k == 0)
    def _():
        acc_ref[...] = jnp.zeros_like(acc_ref)

    acc_ref[...] += jnp.dot(a_ref[...], b_ref[...], preferred_element_type=F32)

    @pl.when(k == nk - 1)
    def _():
        o_ref[...] = epilogue(acc_ref[...], *[e[...] for e in ex_refs]).astype(o_ref.dtype)


def matmul_ksplit(a, b, *, out_dtype, tm, tn, tk, m_rows=None, epilogue=None, extras=(), name="mmk"):
    K = a.shape[1]
    N = b.shape[1]
    M = a.shape[0] if m_rows is None else m_rows
    tn = _pick(N, tuple(t for t in (tn, 512, 256, 128) if t <= tn))
    assert M % tm == 0 and N % tn == 0 and K % tk == 0, (M, tm, N, tn, K, tk)
    if epilogue is None:
        epilogue = lambda acc: acc
    in_specs = [pl.BlockSpec((tm, tk), lambda i, j, k: (i, k)),
                pl.BlockSpec((tk, tn), lambda i, j, k: (k, j))]
    ex_arrays = []
    for arr, kind in extras:
        ex_arrays.append(arr)
        if kind == "tile":
            in_specs.append(pl.BlockSpec((tm, tn), lambda i, j, k: (i, j)))
        elif kind == "rowtab":
            in_specs.append(pl.BlockSpec((tm, arr.shape[1]), lambda i, j, k: (i, 0)))
        else:
            raise ValueError(kind)
    nk = K // tk
    body = functools.partial(_mmk_body, n_ex=len(ex_arrays), epilogue=epilogue, nk=nk)
    return pl.pallas_call(
        body,
        grid=(M // tm, N // tn, nk),
        in_specs=in_specs,
        out_specs=pl.BlockSpec((tm, tn), lambda i, j, k: (i, j)),
        out_shape=jax.ShapeDtypeStruct((M, N), out_dtype),
        scratch_shapes=[pltpu.VMEM((tm, tn), F32)],
        compiler_params=_cparams(("parallel", "parallel", "arbitrary")),
        name=name,
    )(a, b, *ex_arrays)


def _mod_body(c_ref, w_ref, b_ref, o_ref):
    c = c_ref[...]
    act = (c * jax.nn.sigmoid(c)).astype(BF16)
    o_ref[...] = jnp.dot(act, w_ref[...].astype(BF16), preferred_element_type=F32) + b_ref[...]


def mod_vectors(cond, w_mod, b_mod):
    depth, d, n = w_mod.shape
    r = cond.shape[0]
    tn = _pick(n, (1024, 512, 256, 128))
    return pl.pallas_call(
        _mod_body,
        grid=(depth, n // tn),
        in_specs=[pl.BlockSpec((r, d), lambda l, j: (0, 0)),
                  pl.BlockSpec((None, d, tn), lambda l, j: (l, 0, j)),
                  pl.BlockSpec((None, 1, tn), lambda l, j: (l, 0, j))],
        out_specs=pl.BlockSpec((None, r, tn), lambda l, j: (l, 0, j)),
        out_shape=jax.ShapeDtypeStruct((depth, r, n), F32),
        compiler_params=_cparams(("parallel", "parallel")),
        name="mod_vectors",
    )(cond, w_mod, b_mod.reshape(depth, 1, n))


def _ln_rows(x):
    mu = jnp.mean(x, axis=-1, keepdims=True)
    xc = x - mu
    var = jnp.mean(xc * xc, axis=-1, keepdims=True)
    return xc * lax.rsqrt(var + LN_EPS)


def _modln_body(x_ref, sh_ref, sc_ref, h_ref):
    h_ref[...] = (_ln_rows(x_ref[...]) * (1.0 + sc_ref[...]) + sh_ref[...]).astype(h_ref.dtype)


def modulate_ln(x, shift, scale, *, rows_per_group, m_rows=None, tr=256):
    M = x.shape[0] if m_rows is None else m_rows
    d = x.shape[1]
    assert M % tr == 0 and rows_per_group % tr == 0
    gmap = lambda i: ((i * tr) // rows_per_group, 0, 0)
    return pl.pallas_call(
        _modln_body,
        grid=(M // tr,),
        in_specs=[pl.BlockSpec((tr, d), lambda i: (i, 0)),
                  pl.BlockSpec((None, 1, d), gmap),
                  pl.BlockSpec((None, 1, d), gmap)],
        out_specs=pl.BlockSpec((tr, d), lambda i: (i, 0)),
        out_shape=jax.ShapeDtypeStruct((M, d), BF16),
        compiler_params=_cparams(("parallel",)),
        name="modulate_ln",
    )(x, shift, scale)


def _resln_body(x_ref, y_ref, gate_ref, g_ref, b_ref, sh_ref, sc_ref, xo_ref, h_ref, *, alpha):
    xn = _ln_rows(alpha * x_ref[...] + gate_ref[...] * y_ref[...]) * g_ref[...] + b_ref[...]
    xo_ref[...] = xn
    h_ref[...] = (_ln_rows(xn) * (1.0 + sc_ref[...]) + sh_ref[...]).astype(h_ref.dtype)


def _resln_last_body(x_ref, y_ref, gate_ref, g_ref, b_ref, xo_ref, *, alpha):
    xo_ref[...] = _ln_rows(alpha * x_ref[...] + gate_ref[...] * y_ref[...]) * g_ref[...] + b_ref[...]


def residual_ln(x, y, gate, ln_g, ln_b, shift, scale, *, alpha, rows_per_group, m_rows=None, tr=256,
                h_dtype=None):
    M = x.shape[0] if m_rows is None else m_rows
    d = x.shape[1]
    assert M % tr == 0 and rows_per_group % tr == 0
    gmap = lambda i: ((i * tr) // rows_per_group, 0, 0)
    row = pl.BlockSpec((tr, d), lambda i: (i, 0))
    vec = pl.BlockSpec((1, d), lambda i: (0, 0))
    gvec = pl.BlockSpec((None, 1, d), gmap)
    if shift is None:
        return pl.pallas_call(
            functools.partial(_resln_last_body, alpha=alpha),
            grid=(M // tr,),
            in_specs=[row, row, gvec, vec, vec],
            out_specs=row,
            out_shape=jax.ShapeDtypeStruct((M, d), F32),
            compiler_params=_cparams(("parallel",)),
            name="residual_ln_last",
        )(x, y, gate, ln_g.reshape(1, d), ln_b.reshape(1, d)), None
    return pl.pallas_call(
        functools.partial(_resln_body, alpha=alpha),
        grid=(M // tr,),
        in_specs=[row, row, gvec, vec, vec, gvec, gvec],
        out_specs=[row, row],
        out_shape=[jax.ShapeDtypeStruct((M, d), F32), jax.ShapeDtypeStruct((M, d), h_dtype or BF16)],
        compiler_params=_cparams(("parallel",)),
        name="residual_ln",
    )(x, y, gate, ln_g.reshape(1, d), ln_b.reshape(1, d), shift, scale)


def _flash_body(qt_ref, *refs, seg_lens, tk):
    n_seg = len(seg_lens)
    kv_refs = refs[:2 * n_seg]
    o_ref = refs[2 * n_seg]
    acc_ref, st_a, st_b = refs[2 * n_seg + 1:]
    bufs = (st_a, st_b)
    qt = qt_ref[...]
    tq = qt.shape[1]
    acc_ref[...] = jnp.zeros_like(acc_ref)
    m = jnp.full((1, tq), -jnp.inf, F32)

    def scores(seg, c, tks):
        k = kv_refs[2 * seg][pl.ds(pl.multiple_of(c * tks, tks), tks), :]
        return jnp.dot(k, qt, preferred_element_type=F32)

    def absorb(st, seg, c, tks, m_old):
        vt = kv_refs[2 * seg + 1][:, pl.ds(pl.multiple_of(c * tks, tks), tks)]
        vt1 = jnp.concatenate([vt, jnp.ones((FLASH_ONES, tks), BF16)], axis=0)
        m_new = jnp.maximum(m_old, jnp.max(st, axis=0, keepdims=True))
        p = jnp.exp2(st - m_new).astype(BF16)
        corr = jnp.exp2(m_old - m_new)
        acc_ref[...] = corr * acc_ref[...] + jnp.dot(vt1, p, preferred_element_type=F32)
        return m_new

    def run_static(chunks, cur, m):
        for i, (seg, c, tks) in enumerate(chunks):
            if i + 1 < len(chunks):
                nseg, nc, ntks = chunks[i + 1]
                bufs[1 - cur][:ntks] = scores(nseg, nc, ntks)
            m = absorb(bufs[cur][:tks], seg, c, tks, m)
            cur = 1 - cur
        return m

    chunk_counts = [(s, min(tk, ln), ln // min(tk, ln)) for s, ln in enumerate(seg_lens)]
    s0, tk0, n0 = chunk_counts[0]
    rest = [(s, c, tks) for s, tks, n in chunk_counts[1:] for c in range(n)]
    if n0 >= 4 and n0 % 2 == 0:
        st_a[...] = scores(s0, 0, tk0)

        def pair(j, m):
            c0 = 2 * j
            st_b[...] = scores(s0, c0 + 1, tk0)
            m = absorb(st_a[...], s0, c0, tk0, m)
            st_a[...] = scores(s0, c0 + 2, tk0)
            return absorb(st_b[...], s0, c0 + 1, tk0, m)

        m = lax.fori_loop(0, n0 // 2 - 1, pair, m)
        m = run_static([(s0, n0 - 2, tk0), (s0, n0 - 1, tk0)] + rest, 0, m)
    else:
        chunks = [(s0, c, tk0) for c in range(n0)] + rest
        st_a[:tk0] = scores(s0, 0, tk0)
        m = run_static(chunks, 0, m)
    acc = acc_ref[...]
    o_ref[...] = (acc[:V_HEAD] / acc[V_HEAD:V_HEAD + 1]).T.astype(o_ref.dtype)


def flash_attention(qt, k, vt, *, n_batch, q_row0, q_len, segs, tq, tk, name="flash"):
    h = N_HEADS
    nq = q_len // tq
    assert q_len % tq == 0 and q_row0 % tq == 0
    in_specs = [pl.BlockSpec((HEAD_PAD, tq), lambda b, hh, i: (hh, q_row0 // tq + b * nq + i))]
    args = [qt]
    for row0, ln in segs:
        assert row0 % ln == 0
        in_specs.append(pl.BlockSpec((ln, HEAD_PAD), lambda b, hh, i, r=row0 // ln: (r + b, hh)))
        in_specs.append(pl.BlockSpec((V_HEAD, ln), lambda b, hh, i, r=row0 // ln: (hh, r + b)))
        args += [k, vt]
    body = functools.partial(_flash_body, seg_lens=tuple(ln for _, ln in segs), tk=tk)
    return pl.pallas_call(
        body,
        grid=(n_batch, h, nq),
        in_specs=in_specs,
        out_specs=pl.BlockSpec((tq, V_HEAD), lambda b, hh, i: (b * nq + i, hh)),
        out_shape=jax.ShapeDtypeStruct((n_batch * q_len, h * V_HEAD), BF16),
        scratch_shapes=[pltpu.VMEM((V_HEAD + FLASH_ONES, tq), F32),
                        pltpu.VMEM((tk, tq), F32), pltpu.VMEM((tk, tq), F32)],
        compiler_params=_cparams(("parallel", "parallel", "arbitrary")),
        name=name,
    )(*args)


def _chunk_rows(x_ref):
    return jnp.concatenate([x_ref[:, t, :] for t in range(S5_L)], axis=1).astype(BF16)


def _s5_drive_body(x_ref, w_ref, o_ref):
    res = jnp.dot(_chunk_rows(x_ref), w_ref[...], preferred_element_type=F32)
    for gl in range(S5_SG):
        o_ref[:, gl, :] = res[:, gl * 4 * S5_STATE:(gl + 1) * 4 * S5_STATE]


def _s5_out_body(x_ref, sf_ref, sb_ref, t_ref, q_ref, o_ref):
    s = jnp.concatenate([sf_ref[:, gl, :] for gl in range(S5_SG)]
                        + [sb_ref[:, gl, :] for gl in range(S5_SG)], axis=1).astype(BF16)
    res = (jnp.dot(_chunk_rows(x_ref), t_ref[...], preferred_element_type=F32)
           + jnp.dot(s, q_ref[...], preferred_element_type=F32))
    for t in range(S5_L):
        o_ref[:, t, :] = res[:, t * LANE:(t + 1) * LANE]


def _s5_scan_body(wf_ref, wb_ref, af_ref, bf_ref, ab_ref, bb_ref, sf_ref, sb_ref, st_f, st_b):
    @pl.when(pl.program_id(1) == 0)
    def _():
        st_f[...] = jnp.zeros_like(st_f)
        st_b[...] = jnp.zeros_like(st_b)

    a_f, b_f, a_b, b_b = af_ref[...], bf_ref[...], ab_ref[...], bb_ref[...]
    cb = wf_ref.shape[0]

    def step(c, carry):
        s_f, s_b = carry
        cr = cb - 1 - c
        sf_ref[c] = s_f
        sb_ref[cr] = s_b
        n_f = a_f * s_f + b_f * pltpu.roll(s_f, S5_STATE, 1) + wf_ref[c]
        n_b = a_b * s_b + b_b * pltpu.roll(s_b, S5_STATE, 1) + wb_ref[cr]
        return n_f, n_b

    s_f, s_b = lax.fori_loop(0, cb, step, (st_f[...], st_b[...]))
    st_f[...] = s_f
    st_b[...] = s_b


def s5_scan(w3, lam, *, n_batch, n_xc, n_cc, cb):
    nch, g, _ = w3.shape
    p2 = 2 * S5_STATE
    n_xb, n_cb = n_xc // cb, n_cc // cb
    ctx0 = n_batch * n_xb

    def fwd_blk(b, j):
        return jnp.where(j < n_cb, ctx0 + b * n_cb + j, b * n_xb + j - n_cb)

    def bwd_blk(b, j):
        return jnp.where(j < n_cb, ctx0 + b * n_cb + (n_cb - 1 - j), b * n_xb + (n_xb - 1 - (j - n_cb)))

    coef = pl.BlockSpec((g, p2), lambda b, j: (0, 0))
    return pl.pallas_call(
        _s5_scan_body,
        grid=(n_batch, n_xb + n_cb),
        in_specs=[pl.BlockSpec((cb, g, p2), lambda b, j: (fwd_blk(b, j), 0, 0)),
                  pl.BlockSpec((cb, g, p2), lambda b, j: (bwd_blk(b, j), 0, 1)),
                  coef, coef, coef, coef],
        out_specs=[pl.BlockSpec((cb, g, p2), lambda b, j: (fwd_blk(b, j), 0, 0)),
                   pl.BlockSpec((cb, g, p2), lambda b, j: (bwd_blk(b, j), 0, 0))],
        out_shape=[jax.ShapeDtypeStruct((nch, g, p2), F32)] * 2,
        scratch_shapes=[pltpu.VMEM((g, p2), F32), pltpu.VMEM((g, p2), F32)],
        compiler_params=_cparams(("arbitrary", "arbitrary")),
        name="s5_scan",
    )(w3, w3, *lam)


def _s5_tables(a_re, a_im, log_dt, b_re, b_im, c_re, c_im):
    L, P, Hh = S5_L, S5_STATE, S5_GROUP
    hp = lax.Precision.HIGHEST
    dt = jnp.exp(log_dt.astype(F32))[..., None]
    ar, ai = a_re.astype(F32), a_im.astype(F32)
    j = jnp.arange(L + 1, dtype=F32)[:, None, None, None]
    mag = jnp.exp(j * ar * dt)
    pr, pi = mag * jnp.cos(j * ai * dt), mag * jnp.sin(j * ai * dt)
    lr, li = pr[1], pi[1]
    nr = lr - 1.0
    den = ar * ar + ai * ai
    f_re = ((nr * ar + li * ai) / den)[..., None]
    f_im = ((li * ar - nr * ai) / den)[..., None]
    br, bi = b_re.astype(F32), b_im.astype(F32)
    bb_re = f_re * br - f_im * bi
    bb_im = f_re * bi + f_im * br
    cr, ci = c_re.astype(F32), c_im.astype(F32)

    zr = pr[:L, ..., None] * bb_re - pi[:L, ..., None] * bb_im
    zi = pr[:L, ..., None] * bb_im + pi[:L, ..., None] * bb_re
    kj = (jnp.einsum('dghp,jdgpk->jdghk', cr, zr, precision=hp)
          - jnp.einsum('dghp,jdgpk->jdghk', ci, zi, precision=hp))
    t_idx = jnp.arange(L)
    lag = t_idx[None, :] - t_idx[:, None]
    kf = kj[:, 0][jnp.clip(lag, 0, L - 1)]
    kb = kj[:, 1][jnp.clip(-lag, 0, L - 1)]
    tm4 = (jnp.where((lag >= 0)[:, :, None, None, None], kf, 0.0)
           + jnp.where((lag <= 0)[:, :, None, None, None], kb, 0.0))
    g = tm4.shape[2]
    tmat = jnp.transpose(tm4, (2, 0, 4, 1, 3)).reshape(g, L * Hh, L * Hh)

    kk = jnp.arange(L)
    pf_r, pf_i = pr[L - 1 - kk, 0], pi[L - 1 - kk, 0]
    pb_r, pb_i = pr[kk, 1], pi[kk, 1]
    wf_re = pf_r[..., None] * bb_re[0] - pf_i[..., None] * bb_im[0]
    wf_im = pf_r[..., None] * bb_im[0] + pf_i[..., None] * bb_re[0]
    wb_re = pb_r[..., None] * bb_re[1] - pb_i[..., None] * bb_im[1]
    wb_im = pb_r[..., None] * bb_im[1] + pb_i[..., None] * bb_re[1]
    wcat = jnp.concatenate([wf_re, wf_im, wb_re, wb_im], axis=2)
    wmat = jnp.transpose(wcat, (1, 0, 3, 2)).reshape(g, L * Hh, 4 * P)

    qf_r, qf_i = pr[kk + 1, 0], pi[kk + 1, 0]
    qb_r, qb_i = pr[L - kk, 1], pi[L - kk, 1]

    def qpair(c_r, c_i, q_r, q_i):
        return (c_r[None] * q_r[:, :, None, :] - c_i[None] * q_i[:, :, None, :],
                -c_r[None] * q_i[:, :, None, :] - c_i[None] * q_r[:, :, None, :])

    qf_re, qf_im = qpair(cr[0], ci[0], qf_r, qf_i)
    qb_re, qb_im = qpair(cr[1], ci[1], qb_r, qb_i)
    qf = jnp.transpose(jnp.concatenate([qf_re, qf_im], axis=3), (1, 3, 0, 2))
    qb = jnp.transpose(jnp.concatenate([qb_re, qb_im], axis=3), (1, 3, 0, 2))

    n_in = L * S5_SG * Hh
    lane = jnp.arange(n_in)
    rep_t = (jnp.arange(L * Hh)[:, None] == ((lane // LANE) * Hh + lane % Hh)[None, :]).astype(BF16)
    rep_w = (jnp.arange(4 * P)[:, None] == (lane % (4 * P))[None, :]).astype(BF16)
    col_grp_t = (lane // Hh) % S5_SG
    col_grp_w = lane // (4 * P)
    wfull = s5_spread([wmat.astype(BF16)], rep_w, col_grp_w, chunk_rows=True)
    tfull = s5_spread([tmat.astype(BF16)], rep_t, col_grp_t, chunk_rows=True)
    qfull = s5_spread([qf.reshape(g, 2 * P, L * Hh).astype(BF16), qb.reshape(g, 2 * P, L * Hh).astype(BF16)],
                      rep_t, col_grp_t, chunk_rows=False)

    def lam_tiles(d):
        return (jnp.concatenate([pr[L, d], pr[L, d]], axis=-1), jnp.concatenate([-pi[L, d], pi[L, d]], axis=-1))

    lam = lam_tiles(0) + lam_tiles(1)
    return wfull, tfull, qfull, lam


def _s5_spread_body(*refs, n_src, chunk_rows):
    src_refs = refs[:n_src]
    rep_ref, cgrp_ref, o_ref = refs[n_src:]
    if chunk_rows:
        pieces = [src_refs[0][a, k * S5_GROUP:(k + 1) * S5_GROUP, :] for k in range(S5_L) for a in range(S5_SG)]
        per_grp = S5_GROUP
    else:
        pieces = [r[a] for r in src_refs for a in range(S5_SG)]
        per_grp = src_refs[0].shape[1]
    rows = jnp.concatenate(pieces, axis=0)
    full = jnp.dot(rows, rep_ref[...], preferred_element_type=F32)
    assert per_grp & (per_grp - 1) == 0 and S5_SG & (S5_SG - 1) == 0
    row_grp = (lax.broadcasted_iota(jnp.int32, full.shape, 0) >> (per_grp.bit_length() - 1)) & (S5_SG - 1)
    o_ref[...] = jnp.where(row_grp == cgrp_ref[...], full, 0.0).astype(o_ref.dtype)


def s5_spread(srcs, rep, col_grp, *, chunk_rows):
    g, r, c = srcs[0].shape
    n = rep.shape[1]
    n_rows = len(srcs) * S5_SG * r
    return pl.pallas_call(
        functools.partial(_s5_spread_body, n_src=len(srcs), chunk_rows=chunk_rows),
        grid=(g // S5_SG,),
        in_specs=[pl.BlockSpec((S5_SG, r, c), lambda s: (s, 0, 0)) for _ in srcs]
        + [pl.BlockSpec((c, n), lambda s: (0, 0)), pl.BlockSpec((1, n), lambda s: (0, 0))],
        out_specs=pl.BlockSpec((None, n_rows, n), lambda s: (s, 0, 0)),
        out_shape=jax.ShapeDtypeStruct((g // S5_SG, n_rows, n), BF16),
        compiler_params=_cparams(("parallel",)),
        name="s5_spread",
    )(*srcs, rep, col_grp.astype(jnp.int32).reshape(1, n))


def s5_mix(u, tables, *, n_batch, seq, ctx_len):
    wfull, tfull, qfull, lam = tables
    L, P = S5_L, S5_STATE
    t_rows, w_tot = u.shape
    g = w_tot // S5_GROUP
    nsg = g // S5_SG
    nch = t_rows // L
    n_in = L * LANE
    rb = max(r for r in range(8, 265, 8) if nch % r == 0)
    u3 = u.reshape(nch, L, w_tot)
    xspec = pl.BlockSpec((rb, L, LANE), lambda s, i: (i, 0, s))
    wspec = lambda k, n: pl.BlockSpec((None, k, n), lambda s, i: (s, 0, 0))

    w3 = pl.pallas_call(
        _s5_drive_body,
        grid=(nsg, nch // rb),
        in_specs=[xspec, wspec(n_in, S5_SG * 4 * P)],
        out_specs=pl.BlockSpec((rb, S5_SG, 4 * P), lambda s, i: (i, s, 0)),
        out_shape=jax.ShapeDtypeStruct((nch, g, 4 * P), F32),
        compiler_params=_cparams(("parallel", "parallel")),
        name="s5_drive",
    )(u3, wfull)

    n_xc, n_cc = seq // L, ctx_len // L
    cb = _pick(math.gcd(n_xc, n_cc), (16, 8, 4, 2, 1))
    sf, sb = s5_scan(w3, lam, n_batch=n_batch, n_xc=n_xc, n_cc=n_cc, cb=cb)

    sspec = pl.BlockSpec((rb, S5_SG, 2 * P), lambda s, i: (i, s, 0))
    y3 = pl.pallas_call(
        _s5_out_body,
        grid=(nsg, nch // rb),
        in_specs=[xspec, sspec, sspec, wspec(n_in, n_in), wspec(S5_SG * 4 * P, n_in)],
        out_specs=xspec,
        out_shape=jax.ShapeDtypeStruct((nch, L, w_tot), F32),
        compiler_params=_cparams(("parallel", "parallel")),
        name="s5_out",
    )(u3, sf, sb, tfull, qfull)
    return y3.reshape(t_rows, w_tot)


R_E1, R_E2, R_W1, R_W2, R_RANK1, R_RANK2 = range(6)


def _router_body(lg_ref, b_ref, meta_ref, cnt_ref, carry_ref, *, n_exp):
    @pl.when(pl.program_id(0) == 0)
    def _():
        carry_ref[...] = jnp.zeros_like(carry_ref)

    lg = lg_ref[...] + b_ref[...]
    tr = lg.shape[0]
    lane = lax.broadcasted_iota(jnp.int32, lg.shape, 1).astype(F32)
    neg = jnp.float32(-jnp.inf)
    lg = jnp.where(lane < n_exp, lg, neg)
    m1 = jnp.max(lg, axis=-1, keepdims=True)
    i1 = jnp.min(jnp.where(lg == m1, lane, float(LANE)), axis=-1, keepdims=True)
    lg2 = jnp.where(lane == i1, neg, lg)
    m2 = jnp.max(lg2, axis=-1, keepdims=True)
    i2 = jnp.min(jnp.where(lg2 == m2, lane, float(LANE)), axis=-1, keepdims=True)
    e2 = jnp.exp(m2 - m1)
    den = 1.0 + e2
    sel = jnp.where((lane == i1) | (lane == i2), 1.0, 0.0)
    r_i = lax.broadcasted_iota(jnp.int32, (tr, tr), 0)
    c_i = lax.broadcasted_iota(jnp.int32, (tr, tr), 1)
    tri = jnp.where(r_i > c_i, 1.0, 0.0).astype(BF16)
    before = jnp.dot(tri, sel.astype(BF16), preferred_element_type=F32) + carry_ref[...]
    rank1 = jnp.sum(jnp.where(lane == i1, before, 0.0), axis=-1, keepdims=True)
    rank2 = jnp.sum(jnp.where(lane == i2, before, 0.0), axis=-1, keepdims=True)
    total = carry_ref[...] + jnp.sum(sel, axis=0, keepdims=True)
    carry_ref[...] = total
    cnt_ref[...] = jnp.broadcast_to(total, cnt_ref.shape)
    meta = jnp.zeros_like(lg)
    for idx, val in ((R_E1, i1), (R_E2, i2), (R_W1, 1.0 / den), (R_W2, e2 / den),
                     (R_RANK1, rank1), (R_RANK2, rank2)):
        meta = jnp.where(lane == idx, val, meta)
    meta_ref[...] = meta


def router_top2(logits, b_router_pad, *, n_exp, tr=512):
    m = logits.shape[0]
    tr = _pick(m, (tr, 256, 128, 64, 32, 16, 8))
    return pl.pallas_call(
        functools.partial(_router_body, n_exp=n_exp),
        grid=(m // tr,),
        in_specs=[pl.BlockSpec((tr, LANE), lambda i: (i, 0)), pl.BlockSpec((1, LANE), lambda i: (0, 0))],
        out_specs=[pl.BlockSpec((tr, LANE), lambda i: (i, 0)), pl.BlockSpec((8, LANE), lambda i: (0, 0))],
        out_shape=[jax.ShapeDtypeStruct((m, LANE), F32), jax.ShapeDtypeStruct((8, LANE), F32)],
        scratch_shapes=[pltpu.VMEM((1, LANE), F32)],
        compiler_params=_cparams(("arbitrary",)),
        name="router_top2",
    )(logits, b_router_pad)


DMA_ISSUE_UNROLL = 8


def _rows_wait(src_ref, dst_ref, sem):
    pltpu.make_async_copy(src_ref.at[pl.ds(0, dst_ref.shape[0]), :], dst_ref, sem).wait()


def _gather_rows_body(idx_ref, src_ref, o_ref, buf, sem, *, tb):
    def issue(t, carry):
        pltpu.make_async_copy(src_ref.at[pl.ds(idx_ref[0, t], 1), :], buf.at[pl.ds(t, 1), :], sem).start()
        return carry

    lax.fori_loop(0, tb, issue, 0, unroll=DMA_ISSUE_UNROLL)
    _rows_wait(src_ref, buf, sem)
    o_ref[...] = buf[...].astype(o_ref.dtype)


def gather_rows(src, idx, *, out_dtype, tb=256):
    r, d = idx.shape[0], src.shape[1]
    tb = _pick(r, (tb, 128, 64, 32, 16, 8))
    return pl.pallas_call(
        functools.partial(_gather_rows_body, tb=tb),
        grid=(r // tb,),
        in_specs=[pl.BlockSpec((None, 1, tb), lambda i: (i, 0, 0), memory_space=pltpu.SMEM),
                  pl.BlockSpec(memory_space=pl.ANY)],
        out_specs=pl.BlockSpec((tb, d), lambda i: (i, 0)),
        out_shape=jax.ShapeDtypeStruct((r, d), out_dtype),
        scratch_shapes=[pltpu.VMEM((tb, d), src.dtype), pltpu.SemaphoreType.DMA(())],
        compiler_params=_cparams(("arbitrary",)),
        name="moe_gather_rows",
    )(idx.reshape(r // tb, 1, tb), src)


def _gather_combine_body(pos_ref, meta_ref, y_ref, o_ref, buf, sem, *, tb):
    def issue(t, carry):
        for k in range(TOP_K):
            pltpu.make_async_copy(y_ref.at[pl.ds(pos_ref[0, TOP_K * t + k], 1), :],
                                  buf.at[k, pl.ds(t, 1), :], sem).start()
        return carry

    lax.fori_loop(0, tb, issue, 0, unroll=DMA_ISSUE_UNROLL // TOP_K)
    for k in range(TOP_K):
        _rows_wait(y_ref, buf.at[k], sem)
    meta = meta_ref[...]
    o_ref[...] = meta[:, R_W1:R_W1 + 1] * buf[0] + meta[:, R_W2:R_W2 + 1] * buf[1]


def gather_combine(y, pos, meta, *, tb=256):
    m, d = pos.shape[0], y.shape[1]
    tb = _pick(m, (tb, 128, 64, 32, 16, 8))
    pos3 = pos.reshape(m // tb, 1, TOP_K * tb)
    return pl.pallas_call(
        functools.partial(_gather_combine_body, tb=tb),
        grid=(m // tb,),
        in_specs=[pl.BlockSpec((None, 1, TOP_K * tb), lambda i: (i, 0, 0), memory_space=pltpu.SMEM),
                  pl.BlockSpec((tb, LANE), lambda i: (i, 0)),
                  pl.BlockSpec(memory_space=pl.ANY)],
        out_specs=pl.BlockSpec((tb, d), lambda i: (i, 0)),
        out_shape=jax.ShapeDtypeStruct((m, d), F32),
        scratch_shapes=[pltpu.VMEM((TOP_K, tb, d), F32), pltpu.SemaphoreType.DMA(())],
        compiler_params=_cparams(("arbitrary",)),
        name="moe_gather_combine",
    )(pos3, meta, y)


def _gmm_up_body(te_ref, nv_ref, a_ref, b1_ref, b3_ref, o_ref):
    del te_ref
    live = pl.program_id(1) < nv_ref[0]

    @pl.when(live)
    def _():
        a = a_ref[...]
        g = jnp.dot(a, b1_ref[...].astype(BF16), preferred_element_type=F32)
        u = jnp.dot(a, b3_ref[...].astype(BF16), preferred_element_type=F32)
        o_ref[...] = (g * jax.nn.sigmoid(g) * u).astype(o_ref.dtype)

    @pl.when(jnp.logical_not(live))
    def _():
        o_ref[...] = jnp.zeros_like(o_ref)


def _gmm_down_body(te_ref, nv_ref, a_ref, b_ref, o_ref):
    del te_ref
    live = pl.program_id(1) < nv_ref[0]

    @pl.when(live)
    def _():
        o_ref[...] = jnp.dot(a_ref[...], b_ref[...].astype(BF16), preferred_element_type=F32)

    @pl.when(jnp.logical_not(live))
    def _():
        o_ref[...] = jnp.zeros_like(o_ref)


def grouped_swiglu(xs, w1, w3, w2, tile_expert, n_valid, *, tm):
    r, d = xs.shape
    f = w1.shape[2]
    n_tiles = r // tm
    tn_up = _pick(f, (512, 256, 128))
    tn_dn = _pick(d, (1024, 512, 256, 128))
    act = pl.pallas_call(
        _gmm_up_body,
        grid_spec=pltpu.PrefetchScalarGridSpec(
            num_scalar_prefetch=2,
            grid=(f // tn_up, n_tiles),
            in_specs=[pl.BlockSpec((tm, d), lambda j, i, te, nv: (i, 0)),
                      pl.BlockSpec((None, d, tn_up), lambda j, i, te, nv: (te[i], 0, j)),
                      pl.BlockSpec((None, d, tn_up), lambda j, i, te, nv: (te[i], 0, j))],
            out_specs=pl.BlockSpec((tm, tn_up), lambda j, i, te, nv: (i, j))),
        out_shape=jax.ShapeDtypeStruct((r, f), BF16),
        compiler_params=_cparams(("arbitrary", "arbitrary")),
        name="moe_up",
    )(tile_expert, n_valid, xs, w1, w3)
    return pl.pallas_call(
        _gmm_down_body,
        grid_spec=pltpu.PrefetchScalarGridSpec(
            num_scalar_prefetch=2,
            grid=(d // tn_dn, n_tiles),
            in_specs=[pl.BlockSpec((tm, f), lambda j, i, te, nv: (i, 0)),
                      pl.BlockSpec((None, f, tn_dn), lambda j, i, te, nv: (te[i], 0, j))],
            out_specs=pl.BlockSpec((tm, tn_dn), lambda j, i, te, nv: (i, j))),
        out_shape=jax.ShapeDtypeStruct((r, d), F32),
        compiler_params=_cparams(("arbitrary", "arbitrary")),
        name="moe_down",
    )(tile_expert, n_valid, act, w2)


def sparse_moe(hf, meta, counts, w1, w3, w2, *, tm=512):
    m = hf.shape[0]
    n_exp = w1.shape[0]
    tm = _pick(m, (tm, 256, 128))
    cnt = counts.astype(jnp.int32)
    padded = (cnt + tm - 1) // tm * tm
    ends = jnp.cumsum(padded)
    offs = ends - padded
    e1 = meta[:, R_E1].astype(jnp.int32)
    e2 = meta[:, R_E2].astype(jnp.int32)
    pos = jnp.stack([offs[e1] + meta[:, R_RANK1].astype(jnp.int32),
                     offs[e2] + meta[:, R_RANK2].astype(jnp.int32)], axis=1)
    n_tiles = TOP_K * m // tm + n_exp
    tile_expert = jnp.minimum(jnp.searchsorted(ends, jnp.arange(n_tiles) * tm, side="right"),
                              n_exp - 1).astype(jnp.int32)
    n_valid = (ends[-1:] // tm).astype(jnp.int32)
    src_tok = jnp.zeros((n_tiles * tm,), jnp.int32).at[pos.reshape(-1)].set(
        jnp.repeat(jnp.arange(m, dtype=jnp.int32), TOP_K))
    xs = gather_rows(hf, src_tok, out_dtype=BF16)
    y = grouped_swiglu(xs, w1, w3, w2, tile_expert, n_valid, tm=tm)
    return gather_combine(y, pos, meta)


def _rms_pro(x, gain):
    return x * lax.rsqrt(jnp.mean(x * x, axis=-1, keepdims=True) + RMS_EPS) * gain


def _gelu_tanh(x):
    return 0.5 * x * (1.0 + jnp.tanh(math.sqrt(2.0 / math.pi) * (x + 0.044715 * (x * x * x))))


def _rope_apply(x, cos, sin_up, sin_dn):
    n = x.shape[-1]
    return x * cos + pltpu.roll(x, n - QK_ROPE // 4, 1) * sin_up + pltpu.roll(x, QK_ROPE // 4, 1) * sin_dn


def _rope_heads_t(x, cos, sin_up, sin_dn):
    q4 = QK_ROPE // 4
    parts = []
    for base in range(0, x.shape[0], HEAD_PAD):
        seg = x[base + QK_NOPE:base + QK_NOPE + QK_ROPE]
        rot = seg * cos + pltpu.roll(seg, QK_ROPE - q4, 0) * sin_up + pltpu.roll(seg, q4, 0) * sin_dn
        parts += [x[base:base + QK_NOPE], rot, x[base + QK_NOPE + QK_ROPE:base + HEAD_PAD]]
    return jnp.concatenate(parts, axis=0)


def _rope_tables(n_batch, seq, n_ctx_rows):
    nf = QK_ROPE // 4
    pos = jnp.arange(seq)
    row = (pos // GRID_W).astype(F32)
    col = (pos % GRID_W).astype(F32)
    inv = ROPE_THETA ** (-jnp.arange(nf, dtype=F32) / nf)
    ar, ac = row[:, None] * inv, col[:, None] * inv
    z = jnp.zeros((seq, nf), F32)
    cos64 = jnp.concatenate([jnp.cos(ar), jnp.cos(ar), jnp.cos(ac), jnp.cos(ac)], axis=1)
    up64 = jnp.concatenate([-jnp.sin(ar), z, -jnp.sin(ac), z], axis=1)
    dn64 = jnp.concatenate([z, jnp.sin(ar), z, jnp.sin(ac)], axis=1)

    def place(t64, fill):
        full = jnp.full((seq, HEAD_PAD), fill, F32).at[:, QK_NOPE:QK_NOPE + QK_ROPE].set(t64)
        full = jnp.tile(full, (n_batch, 1))
        return jnp.concatenate([full, jnp.full((n_ctx_rows, HEAD_PAD), fill, F32)], axis=0)

    return place(cos64, 1.0), place(up64, 0.0), place(dn64, 0.0)


def _pad_cols(w, n):
    return jnp.pad(w, ((0, 0), (0, n - w.shape[1])))


def _head_cat_cols(w_a, w_b, da, db):
    k = w_a.shape[0]
    parts = [w_a.reshape(k, N_HEADS, da)]
    if w_b is not None:
        parts.append(w_b.reshape(k, N_HEADS, db))
    used = da + (db if w_b is not None else 0)
    parts.append(jnp.zeros((k, N_HEADS, HEAD_PAD - used), w_a.dtype))
    return jnp.concatenate(parts, axis=2).reshape(k, N_HEADS * HEAD_PAD)


def kernel(x, c, ctx, c_ctx, w_mod, b_mod, w_in, b_gate, q_norm, w_uq, kv_norm, w_ukv, w_branch_mla,
           s5_a_re, s5_a_im, s5_log_dt, s5_b_re, s5_b_im, s5_c_re, s5_c_im, s5_d, w_glu, b_glu,
           w_branch_s5, w_out, ln_mix_g, ln_mix_b, ln_ffn_g, ln_ffn_b, ffn_w1, ffn_w3, ffn_w2,
           moe_w_router, moe_b_router, moe_w1, moe_w3, moe_w2):
    B, N, D = x.shape
    C = ctx.shape[1]
    depth = w_mod.shape[0]
    QL, KL = q_norm.shape[1], kv_norm.shape[1]
    SW = s5_d.shape[1]
    H = N_HEADS
    NX, NC_ROWS = B * N, B * C
    T = NX + NC_ROWS
    alpha = (2 * depth) ** 0.25
    q_scale = (QK_NOPE + QK_ROPE) ** -0.5 * math.log2(math.e)
    o_ckv, o_kr, o_u, o_g = QL, QL + KL, QL + KL + QK_ROPE, QL + KL + QK_ROPE + SW
    assert N % C == 0 and N % 256 == 0 and NC_ROWS % 256 == 0

    tm_all = _pick(T, (1536, 1024, 768, 512, 384, 256, 128))
    tm_x = _pick(NX, (1024, 512, 256, 128))
    tile_n = lambda n: _pick(n, (512, 256, 128))

    n_cond = B + 1
    cond = jnp.concatenate([c, c_ctx[None], jnp.zeros((-n_cond % 8, D), F32)], axis=0)
    mods = mod_vectors(cond, w_mod, b_mod)
    mods = mods.reshape(depth, cond.shape[0], 6, D)

    def mvec(l, k):
        return mods[l, :n_cond, k][:, None, :]

    rope_cos, rope_up, rope_dn = _rope_tables(B, N, NC_ROWS)
    rope_cos_t, rope_up_t, rope_dn_t = (t[:, QK_NOPE:QK_NOPE + QK_ROPE].T for t in (rope_cos, rope_up, rope_dn))
    xt =jnp.concatenate([x.reshape(NX, D), ctx.reshape(NC_ROWS, D)], axis=0)
    h = modulate_ln(xt, mvec(0, 0), mvec(0, 1), rows_per_group=N)

    for l in range(depth):
        need_ctx = l < depth - 1
        rows = T if need_ctx else NX
        tm_r = tm_all if need_ctx else tm_x

        wi = w_in[l]
        w_cq = wi[:, :o_ckv].astype(BF16)
        w_ckv = wi[:, o_ckv:o_kr].astype(BF16)
        w_kr = jnp.concatenate(
            [jnp.zeros((D, QK_NOPE), F32), wi[:, o_kr:o_u],
             jnp.zeros((D, HEAD_PAD - QK_NOPE - QK_ROPE), F32)], axis=1).astype(BF16)
        w_u = wi[:, o_u:o_g].astype(BF16)
        w_gm = wi[:, o_g:o_g + D].astype(BF16)
        w_gs = wi[:, o_g + D:].astype(BF16)
        wq = w_uq[l].reshape(QL, H, QK_NOPE + QK_ROPE)
        w_q = _head_cat_cols(wq[:, :, :QK_NOPE].reshape(QL, -1), wq[:, :, QK_NOPE:].reshape(QL, -1),
                             QK_NOPE, QK_ROPE).astype(BF16)
        wkv = w_ukv[l].reshape(KL, H, QK_NOPE + V_HEAD)
        w_k = _head_cat_cols(wkv[:, :, :QK_NOPE].reshape(KL, -1), None, QK_NOPE, 0).astype(BF16)
        w_v = wkv[:, :, QK_NOPE:].reshape(KL, H * V_HEAD).astype(BF16)

        cqn = matmul([h], [w_cq], out_dtype=BF16, tm=tm_r // 2, tn=QL, m_rows=rows,
                     epilogue=_rms_pro, extras=((q_norm[l].reshape(1, QL), "col"),), name="in_cq")
        ckvn = matmul([h], [w_ckv], out_dtype=BF16, tm=tm_all, tn=KL,
                      epilogue=_rms_pro, extras=((kv_norm[l].reshape(1, KL), "col"),), name="in_ckv")
        krp = matmul([h], [w_kr], out_dtype=F32, tm=tm_all, tn=HEAD_PAD, epilogue=_rope_apply,
                     extras=((rope_cos, "rowtab"), (rope_up, "rowtab"), (rope_dn, "rowtab")), name="in_kr")
        u = matmul([h], [w_u], out_dtype=F32, tm=tm_all, tn=tile_n(SW), name="in_s5")

        qt = matmul([w_q.T], [cqn], nt=True, out_dtype=BF16, tm=_pick(H * HEAD_PAD, (1024, 512, 256)), tn=512,
                    n_cols=rows, epilogue=lambda acc, cs, up, dn: _rope_heads_t(acc, cs, up, dn) * q_scale,
                    extras=((rope_cos_t, "coltab"), (rope_up_t, "coltab"), (rope_dn_t, "coltab")),
                    name="mla_qt")
        kh = matmul([ckvn], [w_k], out_dtype=BF16, tm=tm_all, tn=512,
                    epilogue=lambda acc, kr: acc + jnp.tile(kr, (1, acc.shape[1] // HEAD_PAD)),
                    extras=((krp, "rowtab"),), name="mla_k")
        vt = matmul([w_v.T], [ckvn], nt=True, out_dtype=BF16, tm=_pick(H * V_HEAD, (1024, 512, 256, 128)),
                    tn=512, name="mla_vt")
        tq = _pick(N, (512, 256, 128))
        o_x = flash_attention(qt, kh, vt, n_batch=B, q_row0=0, q_len=N, segs=[(0, N), (NX, C)],
                              tq=tq, tk=1024, name="flash_x")
        if need_ctx:
            o_c = flash_attention(qt, kh, vt, n_batch=B, q_row0=NX, q_len=C, segs=[(NX, C)],
                                  tq=_pick(C, (256, 128)), tk=512, name="flash_ctx")
            o_all = jnp.concatenate([o_x, o_c], axis=0)
        else:
            o_all = o_x

        tables = _s5_tables(s5_a_re[l], s5_a_im[l], s5_log_dt[l], s5_b_re[l], s5_b_im[l],
                            s5_c_re[l], s5_c_im[l])
        y = s5_mix(u, tables, n_batch=B, seq=N, ctx_len=C)
        d_row = s5_d[l].reshape(1, SW)
        glu_pro = lambda yv, uv, dv: _gelu_tanh(yv + dv * uv)
        ys = matmul([y, u], [w_glu[l].astype(BF16)], out_dtype=BF16, tm=tm_r // 2, tn=tile_n(SW), m_rows=rows,
                    prologue=glu_pro, pro_consts=(d_row,), keep_pro=True,
                    epilogue=lambda acc, bv, gg: gg * jax.nn.sigmoid(acc + bv),
                    extras=((b_glu[l].reshape(1, SW), "col"),), name="s5_glu")

        bg = b_gate[l]
        m1 = matmul([o_all], [w_branch_mla[l].astype(BF16)], out_dtype=F32, tm=tm_r, tn=512, m_rows=rows,
                    name="branch_mla")
        gm = matmul([h], [w_gm], out_dtype=F32, tm=tm_r, tn=512, m_rows=rows,
                    epilogue=lambda acc, bv, mv: jax.nn.sigmoid(acc + bv) * mv,
                    extras=((bg[:D].reshape(1, D), "col"), (m1, "tile")), name="gate_mla")
        m2 = matmul([ys], [w_branch_s5[l].astype(BF16)], out_dtype=F32, tm=tm_r, tn=512, m_rows=rows,
                    name="branch_s5")
        merged = matmul([h], [w_gs], out_dtype=BF16, tm=tm_r, tn=512, m_rows=rows,
                        epilogue=lambda acc, bv, mv, pv: jax.nn.sigmoid(acc + bv) * mv + pv,
                        extras=((bg[D:].reshape(1, D), "col"), (m2, "tile"), (gm, "tile")), name="gate_s5")
        mix = matmul([merged], [w_out[l].astype(BF16)], out_dtype=F32, tm=tm_r, tn=512, m_rows=rows,
                     name="out_proj")
        xt, h2 = residual_ln(xt, mix, mvec(l, 2), ln_mix_g[l], ln_mix_b[l], mvec(l, 3), mvec(l, 4),
                             alpha=alpha, rows_per_group=N, m_rows=rows, h_dtype=BF16 if l % 2 == 0 else F32)

        if l % 2 == 0:
            fi = l // 2
            dff = ffn_w1.shape[2]
            dff_p = -(-dff // 512) * 512
            w1 = _pad_cols(ffn_w1[fi], dff_p).astype(BF16)
            w3 = _pad_cols(ffn_w3[fi], dff_p).astype(BF16)
            w2 = jnp.pad(ffn_w2[fi], ((0, dff_p - dff), (0, 0))).astype(BF16)
            act = matmul([h2], [w1, w3], out_dtype=BF16, tm=tm_r, tn=256, m_rows=rows,
                         epilogue=lambda a, b: a * jax.nn.sigmoid(a) * b, name="ffn_up")
            tk = _pick(dff_p, (2816, 2048, 1024, 512))
            ff = matmul_ksplit(act, w2, out_dtype=F32, tm=tm_r, tn=512, tk=tk, m_rows=rows, name="ffn_down")
        else:
            mi = l // 2
            n_exp = moe_w_router.shape[2]
            w_r = _pad_cols(moe_w_router[mi], LANE).astype(BF16)
            b_r = jnp.pad(moe_b_router[mi], (0, LANE - n_exp)).reshape(1, LANE)
            logits = matmul([h2], [w_r], out_dtype=F32, tm=tm_r // 2, tn=LANE, m_rows=rows,
                            prologue=lambda a: a, name="router_logits")
            meta, counts = router_top2(logits, b_r, n_exp=n_exp)
            ff = sparse_moe(h2, meta, counts[0, :n_exp], moe_w1[mi], moe_w3[mi], moe_w2[mi])
        if need_ctx:
            xt, h = residual_ln(xt, ff, mvec(l, 5), ln_ffn_g[l], ln_ffn_b[l], mvec(l + 1, 0), mvec(l + 1, 1),
                                alpha=alpha, rows_per_group=N, m_rows=rows)
        else:
            xt, _ = residual_ln(xt, ff, mvec(l, 5), ln_ffn_g[l], ln_ffn_b[l], None, None,
                                alpha=alpha, rows_per_group=N, m_rows=rows)
    return xt[:NX].reshape(B, N, D)
```

```python
import functools
import math

import jax
import jax.numpy as jnp
from jax import lax
from jax.experimental import pallas as pl
from jax.experimental.pallas import tpu as pltpu

N_HEADS = 32
QK_NOPE = 128
QK_ROPE = 64
V_HEAD = 128
ROPE_THETA = 10000.0
GRID_W = 64
S5_GROUP = 16
S5_STATE = 64
TOP_K = 2
LN_EPS = 1e-6
RMS_EPS = 1e-6

HEAD_PAD = 256
S5_L = 16
FLASH_ONES = 16
LANE = 128
S5_SG = LANE // S5_GROUP
VMEM_LIMIT_BYTES = 56 * 2**20

F32 = jnp.float32
BF16 = jnp.bfloat16


def _cparams(sem):
    return pltpu.CompilerParams(dimension_semantics=sem, vmem_limit_bytes=VMEM_LIMIT_BYTES)


def _pick(n, prefs):
    for p in prefs:
        if n % p == 0:
            return p
    raise ValueError(f"no tile in {prefs} divides {n}")


def _mm_body(*refs, n_a, n_pc, n_b, kinds, prologue, epilogue, tn, nt, keep_pro):
    a_refs = refs[:n_a]
    pc_refs = refs[n_a:n_a + n_pc]
    b_refs = refs[n_a + n_pc:n_a + n_pc + n_b]
    ex_refs = refs[n_a + n_pc + n_b:n_a + n_pc + n_b + len(kinds)]
    o_ref = refs[n_a + n_pc + n_b + len(kinds)]
    pro_tile = []
    if prologue is not None:
        a_s = refs[n_a + n_pc + n_b + len(kinds) + 1]
        a_f = refs[n_a + n_pc + n_b + len(kinds) + 2] if keep_pro else None
        j = pl.program_id(1)

        @pl.when(j == 0)
        def _():
            val = prologue(*[r[...] for r in a_refs], *[r[...] for r in pc_refs])
            a_s[...] = val.astype(BF16)
            if keep_pro:
                a_f[...] = val

        a = a_s[...]
        if keep_pro:
            pro_tile = [a_f[:, pl.ds(pl.multiple_of(j * tn, tn), tn)]]
    else:
        a = a_refs[0][...]
    if nt:
        accs = [lax.dot_general(a, b[...], (((1,), (1,)), ((), ())), preferred_element_type=F32)
                for b in b_refs]
    else:
        accs = [jnp.dot(a, b[...].astype(BF16), preferred_element_type=F32) for b in b_refs]
    exs = []
    for r, kind in zip(ex_refs, kinds):
        v = r[...]
        if kind == "rowtab_tiled":
            v = jnp.tile(v, (1, tn // v.shape[1]))
        exs.append(v)
    o_ref[...] = epilogue(*accs, *exs, *pro_tile).astype(o_ref.dtype)


def matmul(a_list, b_list, *, out_dtype, tm, tn, m_rows=None, n_cols=None, nt=False, epilogue=None,
           extras=(), prologue=None, pro_consts=(), keep_pro=False, name="mm"):
    K = a_list[0].shape[1]
    N = b_list[0].shape[0 if nt else 1] if n_cols is None else n_cols
    M = a_list[0].shape[0] if m_rows is None else m_rows
    tn = _pick(N, tuple(t for t in (tn, 512, 256, 128) if t <= tn))
    assert M % tm == 0 and N % tn == 0, (M, tm, N, tn)
    if epilogue is None:
        epilogue = lambda acc: acc
    if prologue is None:
        assert len(a_list) == 1 and a_list[0].dtype == BF16
    in_specs = [pl.BlockSpec((tm, K), lambda i, j: (i, 0)) for _ in a_list]
    in_specs += [pl.BlockSpec(c.shape, lambda i, j: (0, 0)) for c in pro_consts]
    if nt:
        in_specs += [pl.BlockSpec((tn, K), lambda i, j: (j, 0)) for _ in b_list]
    else:
        in_specs += [pl.BlockSpec((K, tn), lambda i, j: (0, j)) for _ in b_list]
    kinds = []
    ex_arrays = []
    for arr, kind in extras:
        kinds.append(kind)
        ex_arrays.append(arr)
        if kind == "tile":
            in_specs.append(pl.BlockSpec((tm, tn), lambda i, j: (i, j)))
        elif kind == "col":
            in_specs.append(pl.BlockSpec((1, tn), lambda i, j: (0, j)))
        elif kind in ("rowtab", "rowtab_tiled"):
            in_specs.append(pl.BlockSpec((tm, arr.shape[1]), lambda i, j: (i, 0)))
        elif kind == "coltab":
            in_specs.append(pl.BlockSpec((arr.shape[0], tn), lambda i, j: (0, j)))
        else:
            raise ValueError(kind)
    scratch = [pltpu.VMEM((tm, K), BF16)] if prologue is not None else []
    if keep_pro:
        assert prologue is not None and N == K
        scratch.append(pltpu.VMEM((tm, K), F32))
    body = functools.partial(_mm_body, n_a=len(a_list), n_pc=len(pro_consts), n_b=len(b_list), kinds=tuple(kinds),
                             prologue=prologue, epilogue=epilogue, tn=tn, nt=nt, keep_pro=keep_pro)
    return pl.pallas_call(
        body,
        grid=(M // tm, N // tn),
        in_specs=in_specs,
        out_specs=pl.BlockSpec((tm, tn), lambda i, j: (i, j)),
        out_shape=jax.ShapeDtypeStruct((M, N), out_dtype),
        scratch_shapes=scratch,
        compiler_params=_cparams(("parallel", "arbitrary")),
        name=name,
    )(*a_list, *pro_consts, *b_list, *ex_arrays)


def _mmk_body(a_ref, b_ref, *rest, n_ex, epilogue, nk):
    ex_refs = rest[:n_ex]
    o_ref = rest[n_ex]
    acc_ref = rest[n_ex + 1]
    k = pl.program_id(2)

    @pl.when(k == 0)
    def _():
        acc_ref[...] = jnp.zeros_like(acc_ref)

    acc_ref[...] += jnp.dot(a_ref[...], b_ref[...], preferred_element_type=F32)

    @pl.when(k == nk - 1)
    def _():
        o_ref[...] = epilogue(acc_ref[...], *[e[...] for e in ex_refs]).astype(o_ref.dtype)


def matmul_ksplit(a, b, *, out_dtype, tm, tn, tk, m_rows=None, epilogue=None, extras=(), name="mmk"):
    K = a.shape[1]
    N = b.shape[1]
    M = a.shape[0] if m_rows is None else m_rows
    tn = _pick(N, tuple(t for t in (tn, 512, 256, 128) if t <= tn))
    assert M % tm == 0 and N % tn == 0 and K % tk == 0, (M, tm, N, tn, K, tk)
    if epilogue is None:
        epilogue = lambda acc: acc
    in_specs = [pl.BlockSpec((tm, tk), lambda i, j, k: (i, k)),
                pl.BlockSpec((tk, tn), lambda i, j, k: (k, j))]
    ex_arrays = []
    for arr, kind in extras:
        ex_arrays.append(arr)
        if kind == "tile":
            in_specs.append(pl.BlockSpec((tm, tn), lambda i, j, k: (i, j)))
        elif kind == "rowtab":
            in_specs.append(pl.BlockSpec((tm, arr.shape[1]), lambda i, j, k: (i, 0)))
        else:
            raise ValueError(kind)
    nk = K // tk
    body = functools.partial(_mmk_body, n_ex=len(ex_arrays), epilogue=epilogue, nk=nk)
    return pl.pallas_call(
        body,
        grid=(M // tm, N // tn, nk),
        in_specs=in_specs,
        out_specs=pl.BlockSpec((tm, tn), lambda i, j, k: (i, j)),
        out_shape=jax.ShapeDtypeStruct((M, N), out_dtype),
        scratch_shapes=[pltpu.VMEM((tm, tn), F32)],
        compiler_params=_cparams(("parallel", "parallel", "arbitrary")),
        name=name,
    )(a, b, *ex_arrays)


def _mod_body(c_ref, w_ref, b_ref, o_ref):
    c = c_ref[...]
    act = (c * jax.nn.sigmoid(c)).astype(BF16)
    o_ref[...] = jnp.dot(act, w_ref[...].astype(BF16), preferred_element_type=F32) + b_ref[...]


def mod_vectors(cond, w_mod, b_mod):
    depth, d, n = w_mod.shape
    r = cond.shape[0]
    tn = _pick(n, (1024, 512, 256, 128))
    return pl.pallas_call(
        _mod_body,
        grid=(depth, n // tn),
        in_specs=[pl.BlockSpec((r, d), lambda l, j: (0, 0)),
                  pl.BlockSpec((None, d, tn), lambda l, j: (l, 0, j)),
                  pl.BlockSpec((None, 1, tn), lambda l, j: (l, 0, j))],
        out_specs=pl.BlockSpec((None, r, tn), lambda l, j: (l, 0, j)),
        out_shape=jax.ShapeDtypeStruct((depth, r, n), F32),
        compiler_params=_cparams(("parallel", "parallel")),
        name="mod_vectors",
    )(cond, w_mod, b_mod.reshape(depth, 1, n))


def _ln_rows(x):
    mu = jnp.mean(x, axis=-1, keepdims=True)
    xc = x - mu
    var = jnp.mean(xc * xc, axis=-1, keepdims=True)
    return xc * lax.rsqrt(var + LN_EPS)


def _modln_body(x_ref, sh_ref, sc_ref, h_ref):
    h_ref[...] = (_ln_rows(x_ref[...]) * (1.0 + sc_ref[...]) + sh_ref[...]).astype(h_ref.dtype)


def modulate_ln(x, shift, scale, *, rows_per_group, m_rows=None, tr=256):
    M = x.shape[0] if m_rows is None else m_rows
    d = x.shape[1]
    assert M % tr == 0 and rows_per_group % tr == 0
    gmap = lambda i: ((i * tr) // rows_per_group, 0, 0)
    return pl.pallas_call(
        _modln_body,
        grid=(M // tr,),
        in_specs=[pl.BlockSpec((tr, d), lambda i: (i, 0)),
                  pl.BlockSpec((None, 1, d), gmap),
                  pl.BlockSpec((None, 1, d), gmap)],
        out_specs=pl.BlockSpec((tr, d), lambda i: (i, 0)),
        out_shape=jax.ShapeDtypeStruct((M, d), BF16),
        compiler_params=_cparams(("parallel",)),
        name="modulate_ln",
    )(x, shift, scale)


def _resln_body(x_ref, y_ref, gate_ref, g_ref, b_ref, sh_ref, sc_ref, xo_ref, h_ref, *, alpha):
    xn = _ln_rows(alpha * x_ref[...] + gate_ref[...] * y_ref[...]) * g_ref[...] + b_ref[...]
    xo_ref[...] = xn
    h_ref[...] = (_ln_rows(xn) * (1.0 + sc_ref[...]) + sh_ref[...]).astype(h_ref.dtype)


def _resln_last_body(x_ref, y_ref, gate_ref, g_ref, b_ref, xo_ref, *, alpha):
    xo_ref[...] = _ln_rows(alpha * x_ref[...] + gate_ref[...] * y_ref[...]) * g_ref[...] + b_ref[...]


def residual_ln(x, y, gate, ln_g, ln_b, shift, scale, *, alpha, rows_per_group, m_rows=None, tr=256,
                h_dtype=None):
    M = x.shape[0] if m_rows is None else m_rows
    d = x.shape[1]
    assert M % tr == 0 and rows_per_group % tr == 0
    gmap = lambda i: ((i * tr) // rows_per_group, 0, 0)
    row = pl.BlockSpec((tr, d), lambda i: (i, 0))
    vec = pl.BlockSpec((1, d), lambda i: (0, 0))
    gvec = pl.BlockSpec((None, 1, d), gmap)
    if shift is None:
        return pl.pallas_call(
            functools.partial(_resln_last_body, alpha=alpha),
            grid=(M // tr,),
            in_specs=[row, row, gvec, vec, vec],
            out_specs=row,
            out_shape=jax.ShapeDtypeStruct((M, d), F32),
            compiler_params=_cparams(("parallel",)),
            name="residual_ln_last",
        )(x, y, gate, ln_g.reshape(1, d), ln_b.reshape(1, d)), None
    return pl.pallas_call(
        functools.partial(_resln_body, alpha=alpha),
        grid=(M // tr,),
        in_specs=[row, row, gvec, vec, vec, gvec, gvec],
        out_specs=[row, row],
        out_shape=[jax.ShapeDtypeStruct((M, d), F32), jax.ShapeDtypeStruct((M, d), h_dtype or BF16)],
        compiler_params=_cparams(("parallel",)),
        name="residual_ln",
    )(x, y, gate, ln_g.reshape(1, d), ln_b.reshape(1, d), shift, scale)


def _flash_body(qt_ref, *refs, seg_lens, tk):
    n_seg = len(seg_lens)
    kv_refs = refs[:2 * n_seg]
    o_ref = refs[2 * n_seg]
    acc_ref, st_a, st_b = refs[2 * n_seg + 1:]
    bufs = (st_a, st_b)
    qt = qt_ref[...]
    tq = qt.shape[1]
    acc_ref[...] = jnp.zeros_like(acc_ref)
    m = jnp.full((1, tq), -jnp.inf, F32)

    def scores(buf, seg, c, tks):
        k = kv_refs[2 * seg][pl.ds(pl.multiple_of(c * tks, tks), tks), :]
        st = jnp.dot(k, qt, preferred_element_type=F32)
        buf[:tks] = st
        return jnp.max(st, axis=0, keepdims=True)

    def absorb(buf, cmax, seg, c, tks, m_old):
        vt = kv_refs[2 * seg + 1][:, pl.ds(pl.multiple_of(c * tks, tks), tks)]
        vt1 = jnp.concatenate([vt, jnp.ones((FLASH_ONES, tks), BF16)], axis=0)
        m_new = jnp.maximum(m_old, cmax)
        p = jnp.exp2(buf[:tks] - m_new).astype(BF16)
        corr = jnp.exp2(m_old - m_new)
        acc_ref[...] = corr * acc_ref[...] + jnp.dot(vt1, p, preferred_element_type=F32)
        return m_new

    def run_static(chunks, cur, cmax, m):
        for i, (seg, c, tks) in enumerate(chunks):
            nxt = scores(bufs[1 - cur], *chunks[i + 1]) if i + 1 < len(chunks) else None
            m = absorb(bufs[cur], cmax, seg, c, tks, m)
            cur, cmax = 1 - cur, nxt
        return m

    chunk_counts = [(s, min(tk, ln), ln // min(tk, ln)) for s, ln in enumerate(seg_lens)]
    s0, tk0, n0 = chunk_counts[0]
    rest = [(s, c, tks) for s, tks, n in chunk_counts[1:] for c in range(n)]
    cmax_a = scores(st_a, s0, 0, tk0)
    if n0 >= 4 and n0 % 2 == 0:
        def pair(j, carry):
            m, cmax_a = carry
            c0 = 2 * j
            cmax_b = scores(st_b, s0, c0 + 1, tk0)
            m = absorb(st_a, cmax_a, s0, c0, tk0, m)
            cmax_a = scores(st_a, s0, c0 + 2, tk0)
            return absorb(st_b, cmax_b, s0, c0 + 1, tk0, m), cmax_a

        m, cmax_a = lax.fori_loop(0, n0 // 2 - 1, pair, (m, cmax_a))
        m = run_static([(s0, n0 - 2, tk0), (s0, n0 - 1, tk0)] + rest, 0, cmax_a, m)
    else:
        m = run_static([(s0, c, tk0) for c in range(n0)] + rest, 0, cmax_a, m)
    acc = acc_ref[...]
    o_ref[...] = (acc[:V_HEAD] / acc[V_HEAD:V_HEAD + 1]).T.astype(o_ref.dtype)


def flash_attention(qt, k, vt, *, n_batch, q_row0, q_len, segs, tq, tk, name="flash"):
    h = N_HEADS
    nq = q_len // tq
    assert q_len % tq == 0 and q_row0 % tq == 0
    in_specs = [pl.BlockSpec((HEAD_PAD, tq), lambda b, hh, i: (hh, q_row0 // tq + b * nq + i))]
    args = [qt]
    for row0, ln in segs:
        assert row0 % ln == 0
        in_specs.append(pl.BlockSpec((ln, HEAD_PAD), lambda b, hh, i, r=row0 // ln: (r + b, hh)))
        in_specs.append(pl.BlockSpec((V_HEAD, ln), lambda b, hh, i, r=row0 // ln: (hh, r + b)))
        args += [k, vt]
    body = functools.partial(_flash_body, seg_lens=tuple(ln for _, ln in segs), tk=tk)
    return pl.pallas_call(
        body,
        grid=(n_batch, h, nq),
        in_specs=in_specs,
        out_specs=pl.BlockSpec((tq, V_HEAD), lambda b, hh, i: (b * nq + i, hh)),
        out_shape=jax.ShapeDtypeStruct((n_batch * q_len, h * V_HEAD), BF16),
        scratch_shapes=[pltpu.VMEM((V_HEAD + FLASH_ONES, tq), F32),
                        pltpu.VMEM((tk, tq), F32), pltpu.VMEM((tk, tq), F32)],
        compiler_params=_cparams(("parallel", "parallel", "arbitrary")),
        name=name,
    )(*args)


def _chunk_rows(x_ref):
    return jnp.concatenate([x_ref[:, t, :] for t in range(S5_L)], axis=1).astype(BF16)


def _s5_drive_body(x_ref, w_ref, o_ref):
    res = jnp.dot(_chunk_rows(x_ref), w_ref[...], preferred_element_type=F32)
    for gl in range(S5_SG):
        o_ref[:, gl, :] = res[:, gl * 4 * S5_STATE:(gl + 1) * 4 * S5_STATE]


def _s5_out_body(x_ref, sf_ref, sb_ref, t_ref, q_ref, o_ref):
    s = jnp.concatenate([sf_ref[:, gl, :] for gl in range(S5_SG)]
                        + [sb_ref[:, gl, :] for gl in range(S5_SG)], axis=1).astype(BF16)
    res = (jnp.dot(_chunk_rows(x_ref), t_ref[...], preferred_element_type=F32)
           + jnp.dot(s, q_ref[...], preferred_element_type=F32))
    for t in range(S5_L):
        o_ref[:, t, :] = res[:, t * LANE:(t + 1) * LANE]


def _s5_scan_body(wf_ref, wb_ref, af_ref, bf_ref, ab_ref, bb_ref, sf_ref, sb_ref, st_f, st_b):
    @pl.when(pl.program_id(1) == 0)
    def _():
        st_f[...] = jnp.zeros_like(st_f)
        st_b[...] = jnp.zeros_like(st_b)

    a_f, b_f, a_b, b_b = af_ref[...], bf_ref[...], ab_ref[...], bb_ref[...]
    cb = wf_ref.shape[0]

    def step(c, carry):
        s_f, s_b = carry
        cr = cb - 1 - c
        sf_ref[c] = s_f
        sb_ref[cr] = s_b
        n_f = a_f * s_f + b_f * pltpu.roll(s_f, S5_STATE, 1) + wf_ref[c]
        n_b = a_b * s_b + b_b * pltpu.roll(s_b, S5_STATE, 1) + wb_ref[cr]
        return n_f, n_b

    s_f, s_b = lax.fori_loop(0, cb, step, (st_f[...], st_b[...]))
    st_f[...] = s_f
    st_b[...] = s_b


def s5_scan(w3, lam, *, n_batch, n_xc, n_cc, cb):
    nch, g, _ = w3.shape
    p2 = 2 * S5_STATE
    n_xb, n_cb = n_xc // cb, n_cc // cb
    ctx0 = n_batch * n_xb

    def fwd_blk(b, j):
        return jnp.where(j < n_cb, ctx0 + b * n_cb + j, b * n_xb + j - n_cb)

    def bwd_blk(b, j):
        return jnp.where(j < n_cb, ctx0 + b * n_cb + (n_cb - 1 - j), b * n_xb + (n_xb - 1 - (j - n_cb)))

    coef = pl.BlockSpec((g, p2), lambda b, j: (0, 0))
    return pl.pallas_call(
        _s5_scan_body,
        grid=(n_batch, n_xb + n_cb),
        in_specs=[pl.BlockSpec((cb, g, p2), lambda b, j: (fwd_blk(b, j), 0, 0)),
                  pl.BlockSpec((cb, g, p2), lambda b, j: (bwd_blk(b, j), 0, 1)),
                  coef, coef, coef, coef],
        out_specs=[pl.BlockSpec((cb, g, p2), lambda b, j: (fwd_blk(b, j), 0, 0)),
                   pl.BlockSpec((cb, g, p2), lambda b, j: (bwd_blk(b, j), 0, 0))],
        out_shape=[jax.ShapeDtypeStruct((nch, g, p2), F32)] * 2,
        scratch_shapes=[pltpu.VMEM((g, p2), F32), pltpu.VMEM((g, p2), F32)],
        compiler_params=_cparams(("arbitrary", "arbitrary")),
        name="s5_scan",
    )(w3, w3, *lam)


def _s5_tables(a_re, a_im, log_dt, b_re, b_im, c_re, c_im):
    L, P, Hh = S5_L, S5_STATE, S5_GROUP
    hp = lax.Precision.HIGHEST
    dt = jnp.exp(log_dt.astype(F32))[..., None]
    ar, ai = a_re.astype(F32), a_im.astype(F32)
    j = jnp.arange(L + 1, dtype=F32)[:, None, None, None]
    mag = jnp.exp(j * ar * dt)
    pr, pi = mag * jnp.cos(j * ai * dt), mag * jnp.sin(j * ai * dt)
    lr, li = pr[1], pi[1]
    nr = lr - 1.0
    den = ar * ar + ai * ai
    f_re = ((nr * ar + li * ai) / den)[..., None]
    f_im = ((li * ar - nr * ai) / den)[..., None]
    br, bi = b_re.astype(F32), b_im.astype(F32)
    bb_re = f_re * br - f_im * bi
    bb_im = f_re * bi + f_im * br
    cr, ci = c_re.astype(F32), c_im.astype(F32)

    zr = pr[:L, ..., None] * bb_re - pi[:L, ..., None] * bb_im
    zi = pr[:L, ..., None] * bb_im + pi[:L, ..., None] * bb_re
    kj = (jnp.einsum('dghp,jdgpk->jdghk', cr, zr, precision=hp)
          - jnp.einsum('dghp,jdgpk->jdghk', ci, zi, precision=hp))
    t_idx = jnp.arange(L)
    lag = t_idx[None, :] - t_idx[:, None]
    kf = kj[:, 0][jnp.clip(lag, 0, L - 1)]
    kb = kj[:, 1][jnp.clip(-lag, 0, L - 1)]
    tm4 = (jnp.where((lag >= 0)[:, :, None, None, None], kf, 0.0)
           + jnp.where((lag <= 0)[:, :, None, None, None], kb, 0.0))
    g = tm4.shape[2]
    tmat = jnp.transpose(tm4, (2, 0, 4, 1, 3)).reshape(g, L * Hh, L * Hh)

    kk = jnp.arange(L)
    pf_r, pf_i = pr[L - 1 - kk, 0], pi[L - 1 - kk, 0]
    pb_r, pb_i = pr[kk, 1], pi[kk, 1]
    wf_re = pf_r[..., None] * bb_re[0] - pf_i[..., None] * bb_im[0]
    wf_im = pf_r[..., None] * bb_im[0] + pf_i[..., None] * bb_re[0]
    wb_re = pb_r[..., None] * bb_re[1] - pb_i[..., None] * bb_im[1]
    wb_im = pb_r[..., None] * bb_im[1] + pb_i[..., None] * bb_re[1]
    wcat = jnp.concatenate([wf_re, wf_im, wb_re, wb_im], axis=2)
    wmat = jnp.transpose(wcat, (1, 0, 3, 2)).reshape(g, L * Hh, 4 * P)

    qf_r, qf_i = pr[kk + 1, 0], pi[kk + 1, 0]
    qb_r, qb_i = pr[L - kk, 1], pi[L - kk, 1]

    def qpair(c_r, c_i, q_r, q_i):
        return (c_r[None] * q_r[:, :, None, :] - c_i[None] * q_i[:, :, None, :],
                -c_r[None] * q_i[:, :, None, :] - c_i[None] * q_r[:, :, None, :])

    qf_re, qf_im = qpair(cr[0], ci[0], qf_r, qf_i)
    qb_re, qb_im = qpair(cr[1], ci[1], qb_r, qb_i)
    qf = jnp.transpose(jnp.concatenate([qf_re, qf_im], axis=3), (1, 3, 0, 2))
    qb = jnp.transpose(jnp.concatenate([qb_re, qb_im], axis=3), (1, 3, 0, 2))

    n_in = L * S5_SG * Hh
    lane = jnp.arange(n_in)
    rep_t = (jnp.arange(L * Hh)[:, None] == ((lane // LANE) * Hh + lane % Hh)[None, :]).astype(BF16)
    rep_w = (jnp.arange(4 * P)[:, None] == (lane % (4 * P))[None, :]).astype(BF16)
    col_grp_t = (lane // Hh) % S5_SG
    col_grp_w = lane // (4 * P)
    wfull = s5_spread([wmat.astype(BF16)], rep_w, col_grp_w, chunk_rows=True)
    tfull = s5_spread([tmat.astype(BF16)], rep_t, col_grp_t, chunk_rows=True)
    qfull = s5_spread([qf.reshape(g, 2 * P, L * Hh).astype(BF16), qb.reshape(g, 2 * P, L * Hh).astype(BF16)],
                      rep_t, col_grp_t, chunk_rows=False)

    def lam_tiles(d):
        return (jnp.concatenate([pr[L, d], pr[L, d]], axis=-1), jnp.concatenate([-pi[L, d], pi[L, d]], axis=-1))

    lam = lam_tiles(0) + lam_tiles(1)
    return wfull, tfull, qfull, lam


def _s5_spread_body(*refs, n_src, chunk_rows):
    src_refs = refs[:n_src]
    rep_ref, cgrp_ref, o_ref = refs[n_src:]
    if chunk_rows:
        pieces = [src_refs[0][a, k * S5_GROUP:(k + 1) * S5_GROUP, :] for k in range(S5_L) for a in range(S5_SG)]
        per_grp = S5_GROUP
    else:
        pieces = [r[a] for r in src_refs for a in range(S5_SG)]
        per_grp = src_refs[0].shape[1]
    rows = jnp.concatenate(pieces, axis=0)
    full = jnp.dot(rows, rep_ref[...], preferred_element_type=F32)
    assert per_grp & (per_grp - 1) == 0 and S5_SG & (S5_SG - 1) == 0
    row_grp = (lax.broadcasted_iota(jnp.int32, full.shape, 0) >> (per_grp.bit_length() - 1)) & (S5_SG - 1)
    o_ref[...] = jnp.where(row_grp == cgrp_ref[...], full, 0.0).astype(o_ref.dtype)


def s5_spread(srcs, rep, col_grp, *, chunk_rows):
    g, r, c = srcs[0].shape
    n = rep.shape[1]
    n_rows = len(srcs) * S5_SG * r
    return pl.pallas_call(
        functools.partial(_s5_spread_body, n_src=len(srcs), chunk_rows=chunk_rows),
        grid=(g // S5_SG,),
        in_specs=[pl.BlockSpec((S5_SG, r, c), lambda s: (s, 0, 0)) for _ in srcs]
        + [pl.BlockSpec((c, n), lambda s: (0, 0)), pl.BlockSpec((1, n), lambda s: (0, 0))],
        out_specs=pl.BlockSpec((None, n_rows, n), lambda s: (s, 0, 0)),
        out_shape=jax.ShapeDtypeStruct((g // S5_SG, n_rows, n), BF16),
        compiler_params=_cparams(("parallel",)),
        name="s5_spread",
    )(*srcs, rep, col_grp.astype(jnp.int32).reshape(1, n))


def s5_mix(u, tables, *, n_batch, seq, ctx_len):
    wfull, tfull, qfull, lam = tables
    L, P = S5_L, S5_STATE
    t_rows, w_tot = u.shape
    g = w_tot // S5_GROUP
    nsg = g // S5_SG
    nch = t_rows // L
    n_in = L * LANE
    rb = max(r for r in range(8, 265, 8) if nch % r == 0)
    u3 = u.reshape(nch, L, w_tot)
    xspec = pl.BlockSpec((rb, L, LANE), lambda s, i: (i, 0, s))
    wspec = lambda k, n: pl.BlockSpec((None, k, n), lambda s, i: (s, 0, 0))

    w3 = pl.pallas_call(
        _s5_drive_body,
        grid=(nsg, nch // rb),
        in_specs=[xspec, wspec(n_in, S5_SG * 4 * P)],
        out_specs=pl.BlockSpec((rb, S5_SG, 4 * P), lambda s, i: (i, s, 0)),
        out_shape=jax.ShapeDtypeStruct((nch, g, 4 * P), F32),
        compiler_params=_cparams(("parallel", "parallel")),
        name="s5_drive",
    )(u3, wfull)

    n_xc, n_cc = seq // L, ctx_len // L
    cb = _pick(math.gcd(n_xc, n_cc), (16, 8, 4, 2, 1))
    sf, sb = s5_scan(w3, lam, n_batch=n_batch, n_xc=n_xc, n_cc=n_cc, cb=cb)

    sspec = pl.BlockSpec((rb, S5_SG, 2 * P), lambda s, i: (i, s, 0))
    y3 = pl.pallas_call(
        _s5_out_body,
        grid=(nsg, nch // rb),
        in_specs=[xspec, sspec, sspec, wspec(n_in, n_in), wspec(S5_SG * 4 * P, n_in)],
        out_specs=xspec,
        out_shape=jax.ShapeDtypeStruct((nch, L, w_tot), F32),
        compiler_params=_cparams(("parallel", "parallel")),
        name="s5_out",
    )(u3, sf, sb, tfull, qfull)
    return y3.reshape(t_rows, w_tot)


R_E1, R_E2, R_W1, R_W2, R_RANK1, R_RANK2 = range(6)


def _router_body(lg_ref, b_ref, meta_ref, cnt_ref, carry_ref, *, n_exp):
    @pl.when(pl.program_id(0) == 0)
    def _():
        carry_ref[...] = jnp.zeros_like(carry_ref)

    lg = lg_ref[...] + b_ref[...]
    tr = lg.shape[0]
    lane = lax.broadcasted_iota(jnp.int32, lg.shape, 1).astype(F32)
    neg = jnp.float32(-jnp.inf)
    lg = jnp.where(lane < n_exp, lg, neg)
    m1 = jnp.max(lg, axis=-1, keepdims=True)
    i1 = jnp.min(jnp.where(lg == m1, lane, float(LANE)), axis=-1, keepdims=True)
    lg2 = jnp.where(lane == i1, neg, lg)
    m2 = jnp.max(lg2, axis=-1, keepdims=True)
    i2 = jnp.min(jnp.where(lg2 == m2, lane, float(LANE)), axis=-1, keepdims=True)
    e2 = jnp.exp(m2 - m1)
    den = 1.0 + e2
    sel = jnp.where((lane == i1) | (lane == i2), 1.0, 0.0)
    r_i = lax.broadcasted_iota(jnp.int32, (tr, tr), 0)
    c_i = lax.broadcasted_iota(jnp.int32, (tr, tr), 1)
    tri = jnp.where(r_i > c_i, 1.0, 0.0).astype(BF16)
    before = jnp.dot(tri, sel.astype(BF16), preferred_element_type=F32) + carry_ref[...]
    rank1 = jnp.sum(jnp.where(lane == i1, before, 0.0), axis=-1, keepdims=True)
    rank2 = jnp.sum(jnp.where(lane == i2, before, 0.0), axis=-1, keepdims=True)
    total = carry_ref[...] + jnp.sum(sel, axis=0, keepdims=True)
    carry_ref[...] = total
    cnt_ref[...] = jnp.broadcast_to(total, cnt_ref.shape)
    meta = jnp.zeros_like(lg)
    for idx, val in ((R_E1, i1), (R_E2, i2), (R_W1, 1.0 / den), (R_W2, e2 / den),
                     (R_RANK1, rank1), (R_RANK2, rank2)):
        meta = jnp.where(lane == idx, val, meta)
    meta_ref[...] = meta


def router_top2(logits, b_router_pad, *, n_exp, tr=512):
    m = logits.shape[0]
    tr = _pick(m, (tr, 256, 128, 64, 32, 16, 8))
    return pl.pallas_call(
        functools.partial(_router_body, n_exp=n_exp),
        grid=(m // tr,),
        in_specs=[pl.BlockSpec((tr, LANE), lambda i: (i, 0)), pl.BlockSpec((1, LANE), lambda i: (0, 0))],
        out_specs=[pl.BlockSpec((tr, LANE), lambda i: (i, 0)), pl.BlockSpec((8, LANE), lambda i: (0, 0))],
        out_shape=[jax.ShapeDtypeStruct((m, LANE), F32), jax.ShapeDtypeStruct((8, LANE), F32)],
        scratch_shapes=[pltpu.VMEM((1, LANE), F32)],
        compiler_params=_cparams(("arbitrary",)),
        name="router_top2",
    )(logits, b_router_pad)


DMA_ISSUE_UNROLL = 8


def _rows_wait(src_ref, dst_ref, sem):
    pltpu.make_async_copy(src_ref.at[pl.ds(0, dst_ref.shape[0]), :], dst_ref, sem).wait()


def _fetch_rows(idx_ref, idx_next_ref, src_ref, buf, sem, *, tb, n_per):
    i = pl.program_id(0)
    slot = lax.rem(i, 2)

    def start_block(ref, s):
        def issue(t, carry):
            for k in range(n_per):
                pltpu.make_async_copy(src_ref.at[pl.ds(ref[0, n_per * t + k], 1), :],
                                      buf.at[s, k, pl.ds(t, 1), :], sem.at[s]).start()
            return carry

        lax.fori_loop(0, tb, issue, 0, unroll=DMA_ISSUE_UNROLL // n_per)

    @pl.when(i == 0)
    def _():
        start_block(idx_ref, slot)

    @pl.when(i + 1 < pl.num_programs(0))
    def _():
        start_block(idx_next_ref, 1 - slot)

    for k in range(n_per):
        _rows_wait(src_ref, buf.at[slot, k], sem.at[slot])
    return slot


def _idx_specs(n_blocks, width):
    return [pl.BlockSpec((None, 1, width), lambda i: (i, 0, 0), memory_space=pltpu.SMEM),
            pl.BlockSpec((None, 1, width), lambda i: (jnp.minimum(i + 1, n_blocks - 1), 0, 0),
                         memory_space=pltpu.SMEM)]


def _gather_rows_body(idx_ref, idx_next_ref, src_ref, o_ref, buf, sem, *, tb):
    slot = _fetch_rows(idx_ref, idx_next_ref, src_ref, buf, sem, tb=tb, n_per=1)
    o_ref[...] = buf[slot, 0].astype(o_ref.dtype)


def gather_rows(src, idx, *, out_dtype, tb=256):
    r, d = idx.shape[0], src.shape[1]
    tb = _pick(r, (tb, 128, 64, 32, 16, 8))
    idx3 = idx.reshape(r // tb, 1, tb)
    return pl.pallas_call(
        functools.partial(_gather_rows_body, tb=tb),
        grid=(r // tb,),
        in_specs=_idx_specs(r // tb, tb) + [pl.BlockSpec(memory_space=pl.ANY)],
        out_specs=pl.BlockSpec((tb, d), lambda i: (i, 0)),
        out_shape=jax.ShapeDtypeStruct((r, d), out_dtype),
        scratch_shapes=[pltpu.VMEM((2, 1, tb, d), src.dtype), pltpu.SemaphoreType.DMA((2,))],
        compiler_params=_cparams(("arbitrary",)),
        name="moe_gather_rows",
    )(idx3, idx3, src)


def _gather_combine_body(pos_ref, pos_next_ref, meta_ref, y_ref, o_ref, buf, sem, *, tb):
    slot = _fetch_rows(pos_ref, pos_next_ref, y_ref, buf, sem, tb=tb, n_per=TOP_K)
    meta = meta_ref[...]
    o_ref[...] = meta[:, R_W1:R_W1 + 1] * buf[slot, 0] + meta[:, R_W2:R_W2 + 1] * buf[slot, 1]


def gather_combine(y, pos, meta, *, tb=256):
    m, d = pos.shape[0], y.shape[1]
    tb = _pick(m, (tb, 128, 64, 32, 16, 8))
    pos3 = pos.reshape(m // tb, 1, TOP_K * tb)
    return pl.pallas_call(
        functools.partial(_gather_combine_body, tb=tb),
        grid=(m // tb,),
        in_specs=_idx_specs(m // tb, TOP_K * tb) + [pl.BlockSpec((tb, LANE), lambda i: (i, 0)),
                                                    pl.BlockSpec(memory_space=pl.ANY)],
        out_specs=pl.BlockSpec((tb, d), lambda i: (i, 0)),
        out_shape=jax.ShapeDtypeStruct((m, d), F32),
        scratch_shapes=[pltpu.VMEM((2, TOP_K, tb, d), F32), pltpu.SemaphoreType.DMA((2,))],
        compiler_params=_cparams(("arbitrary",)),
        name="moe_gather_combine",
    )(pos3, pos3, meta, y)


def _gmm_up_body(te_ref, nv_ref, a_ref, b1_ref, b3_ref, o_ref):
    del te_ref
    live = pl.program_id(1) < nv_ref[0]

    @pl.when(live)
    def _():
        a = a_ref[...]
        g = jnp.dot(a, b1_ref[...].astype(BF16), preferred_element_type=F32)
        u = jnp.dot(a, b3_ref[...].astype(BF16), preferred_element_type=F32)
        o_ref[...] = (g * jax.nn.sigmoid(g) * u).astype(o_ref.dtype)

    @pl.when(jnp.logical_not(live))
    def _():
        o_ref[...] = jnp.zeros_like(o_ref)


def _gmm_down_body(te_ref, nv_ref, a_ref, b_ref, o_ref):
    del te_ref
    live = pl.program_id(1) < nv_ref[0]

    @pl.when(live)
    def _():
        o_ref[...] = jnp.dot(a_ref[...], b_ref[...].astype(BF16), preferred_element_type=F32)

    @pl.when(jnp.logical_not(live))
    def _():
        o_ref[...] = jnp.zeros_like(o_ref)


def grouped_swiglu(xs, w1, w3, w2, tile_expert, n_valid, *, tm):
    r, d = xs.shape
    f = w1.shape[2]
    n_tiles = r // tm
    tn_up = _pick(f, (512, 256, 128))
    tn_dn = _pick(d, (1024, 512, 256, 128))
    act = pl.pallas_call(
        _gmm_up_body,
        grid_spec=pltpu.PrefetchScalarGridSpec(
            num_scalar_prefetch=2,
            grid=(f // tn_up, n_tiles),
            in_specs=[pl.BlockSpec((tm, d), lambda j, i, te, nv: (i, 0)),
                      pl.BlockSpec((None, d, tn_up), lambda j, i, te, nv: (te[i], 0, j)),
                      pl.BlockSpec((None, d, tn_up), lambda j, i, te, nv: (te[i], 0, j))],
            out_specs=pl.BlockSpec((tm, tn_up), lambda j, i, te, nv: (i, j))),
        out_shape=jax.ShapeDtypeStruct((r, f), BF16),
        compiler_params=_cparams(("arbitrary", "arbitrary")),
        name="moe_up",
    )(tile_expert, n_valid, xs, w1, w3)
    return pl.pallas_call(
        _gmm_down_body,
        grid_spec=pltpu.PrefetchScalarGridSpec(
            num_scalar_prefetch=2,
            grid=(d // tn_dn, n_tiles),
            in_specs=[pl.BlockSpec((tm, f), lambda j, i, te, nv: (i, 0)),
                      pl.BlockSpec((None, f, tn_dn), lambda j, i, te, nv: (te[i], 0, j))],
            out_specs=pl.BlockSpec((tm, tn_dn), lambda j, i, te, nv: (i, j))),
        out_shape=jax.ShapeDtypeStruct((r, d), F32),
        compiler_params=_cparams(("arbitrary", "arbitrary")),
        name="moe_down",
    )(tile_expert, n_valid, act, w2)


def sparse_moe(hf, meta, counts, w1, w3, w2, *, tm=512):
    m = hf.shape[0]
    n_exp = w1.shape[0]
    tm = _pick(m, (tm, 256, 128))
    cnt = counts.astype(jnp.int32)
    padded = (cnt + tm - 1) // tm * tm
    ends = jnp.cumsum(padded)
    offs = ends - padded
    e1 = meta[:, R_E1].astype(jnp.int32)
    e2 = meta[:, R_E2].astype(jnp.int32)
    pos = jnp.stack([offs[e1] + meta[:, R_RANK1].astype(jnp.int32),
                     offs[e2] + meta[:, R_RANK2].astype(jnp.int32)], axis=1)
    n_tiles = TOP_K * m // tm + n_exp
    tile_expert = jnp.minimum(jnp.searchsorted(ends, jnp.arange(n_tiles) * tm, side="right"),
                              n_exp - 1).astype(jnp.int32)
    n_valid = (ends[-1:] // tm).astype(jnp.int32)
    src_tok = jnp.zeros((n_tiles * tm,), jnp.int32).at[pos.reshape(-1)].set(
        jnp.repeat(jnp.arange(m, dtype=jnp.int32), TOP_K))
    xs = gather_rows(hf, src_tok, out_dtype=BF16)
    y = grouped_swiglu(xs, w1, w3, w2, tile_expert, n_valid, tm=tm)
    return gather_combine(y, pos, meta)


def _rms_pro(x, gain):
    return x * lax.rsqrt(jnp.mean(x * x, axis=-1, keepdims=True) + RMS_EPS) * gain


def _gelu_tanh(x):
    return 0.5 * x * (1.0 + jnp.tanh(math.sqrt(2.0 / math.pi) * (x + 0.044715 * (x * x * x))))


def _rope_apply(x, cos, sin_up, sin_dn):
    n = x.shape[-1]
    return x * cos + pltpu.roll(x, n - QK_ROPE // 4, 1) * sin_up + pltpu.roll(x, QK_ROPE // 4, 1) * sin_dn


def _rope_heads_t(x, cos, sin_up, sin_dn):
    q4 = QK_ROPE // 4
    parts = []
    for base in range(0, x.shape[0], HEAD_PAD):
        seg = x[base + QK_NOPE:base + QK_NOPE + QK_ROPE]
        rot = seg * cos + pltpu.roll(seg, QK_ROPE - q4, 0) * sin_up + pltpu.roll(seg, q4, 0) * sin_dn
        parts += [x[base:base + QK_NOPE], rot, x[base + QK_NOPE + QK_ROPE:base + HEAD_PAD]]
    return jnp.concatenate(parts, axis=0)


def _rope_tables(n_batch, seq, n_ctx_rows):
    nf = QK_ROPE // 4
    pos = jnp.arange(seq)
    row = (pos // GRID_W).astype(F32)
    col = (pos % GRID_W).astype(F32)
    inv = ROPE_THETA ** (-jnp.arange(nf, dtype=F32) / nf)
    ar, ac = row[:, None] * inv, col[:, None] * inv
    z = jnp.zeros((seq, nf), F32)
    cos64 = jnp.concatenate([jnp.cos(ar), jnp.cos(ar), jnp.cos(ac), jnp.cos(ac)], axis=1)
    up64 = jnp.concatenate([-jnp.sin(ar), z, -jnp.sin(ac), z], axis=1)
    dn64 = jnp.concatenate([z, jnp.sin(ar), z, jnp.sin(ac)], axis=1)

    def place(t64, fill):
        full = jnp.full((seq, HEAD_PAD), fill, F32).at[:, QK_NOPE:QK_NOPE + QK_ROPE].set(t64)
        full = jnp.tile(full, (n_batch, 1))
        return jnp.concatenate([full, jnp.full((n_ctx_rows, HEAD_PAD), fill, F32)], axis=0)

    return place(cos64, 1.0), place(up64, 0.0), place(dn64, 0.0)


def _pad_cols(w, n):
    return jnp.pad(w, ((0, 0), (0, n - w.shape[1])))


def _head_cat_cols(w_a, w_b, da, db):
    k = w_a.shape[0]
    parts = [w_a.reshape(k, N_HEADS, da)]
    if w_b is not None:
        parts.append(w_b.reshape(k, N_HEADS, db))
    used = da + (db if w_b is not None else 0)
    parts.append(jnp.zeros((k, N_HEADS, HEAD_PAD - used), w_a.dtype))
    return jnp.concatenate(parts, axis=2).reshape(k, N_HEADS * HEAD_PAD)


def kernel(x, c, ctx, c_ctx, w_mod, b_mod, w_in, b_gate, q_norm, w_uq, kv_norm, w_ukv, w_branch_mla,
           s5_a_re, s5_a_im, s5_log_dt, s5_b_re, s5_b_im, s5_c_re, s5_c_im, s5_d, w_glu, b_glu,
           w_branch_s5, w_out, ln_mix_g, ln_mix_b, ln_ffn_g, ln_ffn_b, ffn_w1, ffn_w3, ffn_w2,
           moe_w_router, moe_b_router, moe_w1, moe_w3, moe_w2):
    B, N, D = x.shape
    C = ctx.shape[1]
    depth = w_mod.shape[0]
    QL, KL = q_norm.shape[1], kv_norm.shape[1]
    SW = s5_d.shape[1]
    H = N_HEADS
    NX, NC_ROWS = B * N, B * C
    T = NX + NC_ROWS
    alpha = (2 * depth) ** 0.25
    q_scale = (QK_NOPE + QK_ROPE) ** -0.5 * math.log2(math.e)
    o_ckv, o_kr, o_u, o_g = QL, QL + KL, QL + KL + QK_ROPE, QL + KL + QK_ROPE + SW
    assert N % C == 0 and N % 256 == 0 and NC_ROWS % 256 == 0

    tm_all = _pick(T, (1536, 1024, 768, 512, 384, 256, 128))
    tm_x = _pick(NX, (1024, 512, 256, 128))
    tile_n = lambda n: _pick(n, (512, 256, 128))

    n_cond = B + 1
    cond = jnp.concatenate([c, c_ctx[None], jnp.zeros((-n_cond % 8, D), F32)], axis=0)
    mods = mod_vectors(cond, w_mod, b_mod)
    mods = mods.reshape(depth, cond.shape[0], 6, D)

    def mvec(l, k):
        return mods[l, :n_cond, k][:, None, :]

    rope_cos, rope_up, rope_dn = _rope_tables(B, N, NC_ROWS)
    rope_cos_t, rope_up_t, rope_dn_t = (t[:, QK_NOPE:QK_NOPE + QK_ROPE].T for t in (rope_cos, rope_up, rope_dn))
    xt =jnp.concatenate([x.reshape(NX, D), ctx.reshape(NC_ROWS, D)], axis=0)
    h = modulate_ln(xt, mvec(0, 0), mvec(0, 1), rows_per_group=N)

    for l in range(depth):
        need_ctx = l < depth - 1
        rows = T if need_ctx else NX
        tm_r = tm_all if need_ctx else tm_x

        wi = w_in[l]
        w_cq = wi[:, :o_ckv].astype(BF16)
        w_ckv = wi[:, o_ckv:o_kr].astype(BF16)
        w_kr = jnp.concatenate(
            [jnp.zeros((D, QK_NOPE), F32), wi[:, o_kr:o_u],
             jnp.zeros((D, HEAD_PAD - QK_NOPE - QK_ROPE), F32)], axis=1).astype(BF16)
        w_u = wi[:, o_u:o_g].astype(BF16)
        w_gm = wi[:, o_g:o_g + D].astype(BF16)
        w_gs = wi[:, o_g + D:].astype(BF16)
        wq = w_uq[l].reshape(QL, H, QK_NOPE + QK_ROPE)
        w_q = _head_cat_cols(wq[:, :, :QK_NOPE].reshape(QL, -1), wq[:, :, QK_NOPE:].reshape(QL, -1),
                             QK_NOPE, QK_ROPE).astype(BF16)
        wkv = w_ukv[l].reshape(KL, H, QK_NOPE + V_HEAD)
        w_k = _head_cat_cols(wkv[:, :, :QK_NOPE].reshape(KL, -1), None, QK_NOPE, 0).astype(BF16)
        w_v = wkv[:, :, QK_NOPE:].reshape(KL, H * V_HEAD).astype(BF16)

        cqn = matmul([h], [w_cq], out_dtype=BF16, tm=tm_r // 2, tn=QL, m_rows=rows,
                     epilogue=_rms_pro, extras=((q_norm[l].reshape(1, QL), "col"),), name="in_cq")
        ckvn = matmul([h], [w_ckv], out_dtype=BF16, tm=tm_all, tn=KL,
                      epilogue=_rms_pro, extras=((kv_norm[l].reshape(1, KL), "col"),), name="in_ckv")
        krp = matmul([h], [w_kr], out_dtype=F32, tm=tm_all, tn=HEAD_PAD, epilogue=_rope_apply,
                     extras=((rope_cos, "rowtab"), (rope_up, "rowtab"), (rope_dn, "rowtab")), name="in_kr")
        u = matmul([h], [w_u], out_dtype=F32, tm=tm_all, tn=tile_n(SW), name="in_s5")

        qt = matmul([w_q.T], [cqn], nt=True, out_dtype=BF16, tm=_pick(H * HEAD_PAD, (1024, 512, 256)), tn=512,
                    n_cols=rows, epilogue=lambda acc, cs, up, dn: _rope_heads_t(acc, cs, up, dn) * q_scale,
                    extras=((rope_cos_t, "coltab"), (rope_up_t, "coltab"), (rope_dn_t, "coltab")),
                    name="mla_qt")
        kh = matmul([ckvn], [w_k], out_dtype=BF16, tm=tm_all, tn=512,
                    epilogue=lambda acc, kr: acc + jnp.tile(kr, (1, acc.shape[1] // HEAD_PAD)),
                    extras=((krp, "rowtab"),), name="mla_k")
        vt = matmul([w_v.T], [ckvn], nt=True, out_dtype=BF16, tm=_pick(H * V_HEAD, (1024, 512, 256, 128)),
                    tn=512, name="mla_vt")
        tq = _pick(N, (512, 256, 128))
        o_x = flash_attention(qt, kh, vt, n_batch=B, q_row0=0, q_len=N, segs=[(0, N), (NX, C)],
                              tq=tq, tk=1024, name="flash_x")
        if need_ctx:
            o_c = flash_attention(qt, kh, vt, n_batch=B, q_row0=NX, q_len=C, segs=[(NX, C)],
                                  tq=_pick(C, (256, 128)), tk=512, name="flash_ctx")
            o_all = jnp.concatenate([o_x, o_c], axis=0)
        else:
            o_all = o_x

        tables = _s5_tables(s5_a_re[l], s5_a_im[l], s5_log_dt[l], s5_b_re[l], s5_b_im[l],
                            s5_c_re[l], s5_c_im[l])
        y = s5_mix(u, tables, n_batch=B, seq=N, ctx_len=C)
        d_row = s5_d[l].reshape(1, SW)
        glu_pro = lambda yv, uv, dv: _gelu_tanh(yv + dv * uv)
        ys = matmul([y, u], [w_glu[l].astype(BF16)], out_dtype=BF16, tm=tm_r // 2, tn=tile_n(SW), m_rows=rows,
                    prologue=glu_pro, pro_consts=(d_row,), keep_pro=True,
                    epilogue=lambda acc, bv, gg: gg * jax.nn.sigmoid(acc + bv),
                    extras=((b_glu[l].reshape(1, SW), "col"),), name="s5_glu")

        bg = b_gate[l]
        m1 = matmul([o_all], [w_branch_mla[l].astype(BF16)], out_dtype=F32, tm=tm_r, tn=512, m_rows=rows,
                    name="branch_mla")
        gm = matmul([h], [w_gm], out_dtype=F32, tm=tm_r, tn=512, m_rows=rows,
                    epilogue=lambda acc, bv, mv: jax.nn.sigmoid(acc + bv) * mv,
                    extras=((bg[:D].reshape(1, D), "col"), (m1, "tile")), name="gate_mla")
        m2 = matmul([ys], [w_branch_s5[l].astype(BF16)], out_dtype=F32, tm=tm_r, tn=512, m_rows=rows,
                    name="branch_s5")
        merged = matmul([h], [w_gs], out_dtype=BF16, tm=tm_r, tn=512, m_rows=rows,
                        epilogue=lambda acc, bv, mv, pv: jax.nn.sigmoid(acc + bv) * mv + pv,
                        extras=((bg[D:].reshape(1, D), "col"), (m2, "tile"), (gm, "tile")), name="gate_s5")
        mix = matmul([merged], [w_out[l].astype(BF16)], out_dtype=F32, tm=tm_r, tn=512, m_rows=rows,
                     name="out_proj")
        xt, h2 = residual_ln(xt, mix, mvec(l, 2), ln_mix_g[l], ln_mix_b[l], mvec(l, 3), mvec(l, 4),
                             alpha=alpha, rows_per_group=N, m_rows=rows, h_dtype=BF16 if l % 2 == 0 else F32)

        if l % 2 == 0:
            fi = l // 2
            dff = ffn_w1.shape[2]
            dff_p = -(-dff // 512) * 512
            w1 = _pad_cols(ffn_w1[fi], dff_p).astype(BF16)
            w3 = _pad_cols(ffn_w3[fi], dff_p).astype(BF16)
            w2 = jnp.pad(ffn_w2[fi], ((0, dff_p - dff), (0, 0))).astype(BF16)
            act = matmul([h2], [w1, w3], out_dtype=BF16, tm=tm_r, tn=256, m_rows=rows,
                         epilogue=lambda a, b: a * jax.nn.sigmoid(a) * b, name="ffn_up")
            tk = _pick(dff_p, (2816, 2048, 1024, 512))
            ff = matmul_ksplit(act, w2, out_dtype=F32, tm=tm_r, tn=512, tk=tk, m_rows=rows, name="ffn_down")
        else:
            mi = l // 2
            n_exp = moe_w_router.shape[2]
            w_r = _pad_cols(moe_w_router[mi], LANE).astype(BF16)
            b_r = jnp.pad(moe_b_router[mi], (0, LANE - n_exp)).reshape(1, LANE)
            logits = matmul([h2], [w_r], out_dtype=F32, tm=tm_r // 2, tn=LANE, m_rows=rows,
                            prologue=lambda a: a, name="router_logits")
            meta, counts = router_top2(logits, b_r, n_exp=n_exp)
            ff = sparse_moe(h2, meta, counts[0, :n_exp], moe_w1[mi], moe_w3[mi], moe_w2[mi])
        if need_ctx:
            xt, h = residual_ln(xt, ff, mvec(l, 5), ln_ffn_g[l], ln_ffn_b[l], mvec(l + 1, 0), mvec(l + 1, 1),
                                alpha=alpha, rows_per_group=N, m_rows=rows)
        else:
            xt, _ = residual_ln(xt, ff, mvec(l, 5), ln_ffn_g[l], ln_ffn_b[l], None, None,
                                alpha=alpha, rows_per_group=N, m_rows=rows)
    return xt[:NX].reshape(B, N, D)
```

```python
import functools
import math

import jax
import jax.numpy as jnp
from jax import lax
from jax.experimental import pallas as pl
from jax.experimental.pallas import tpu as pltpu

N_HEADS = 32
QK_NOPE = 128
QK_ROPE = 64
V_HEAD = 128
ROPE_THETA = 10000.0
GRID_W = 64
S5_GROUP = 16
S5_STATE = 64
TOP_K = 2
LN_EPS = 1e-6
RMS_EPS = 1e-6

HEAD_PAD = 256
S5_L = 16
FLASH_ONES = 16
LANE = 128
S5_SG = LANE // S5_GROUP
VMEM_LIMIT_BYTES = 56 * 2**20

F32 = jnp.float32
BF16 = jnp.bfloat16


def _cparams(sem):
    return pltpu.CompilerParams(dimension_semantics=sem, vmem_limit_bytes=VMEM_LIMIT_BYTES)


def _pick(n, prefs):
    for p in prefs:
        if n % p == 0:
            return p
    raise ValueError(f"no tile in {prefs} divides {n}")


def _mm_body(*refs, n_a, n_pc, n_b, kinds, prologue, epilogue, tn, nt, keep_pro):
    a_refs = refs[:n_a]
    pc_refs = refs[n_a:n_a + n_pc]
    b_refs = refs[n_a + n_pc:n_a + n_pc + n_b]
    ex_refs = refs[n_a + n_pc + n_b:n_a + n_pc + n_b + len(kinds)]
    o_ref = refs[n_a + n_pc + n_b + len(kinds)]
    pro_tile = []
    if prologue is not None:
        a_s = refs[n_a + n_pc + n_b + len(kinds) + 1]
        a_f = refs[n_a + n_pc + n_b + len(kinds) + 2] if keep_pro else None
        j = pl.program_id(1)

        @pl.when(j == 0)
        def _():
            val = prologue(*[r[...] for r in a_refs], *[r[...] for r in pc_refs])
            a_s[...] = val.astype(BF16)
            if keep_pro:
                a_f[...] = val

        a = a_s[...]
        if keep_pro:
            pro_tile = [a_f[:, pl.ds(pl.multiple_of(j * tn, tn), tn)]]
    else:
        a = a_refs[0][...]
    if nt:
        accs = [lax.dot_general(a, b[...], (((1,), (1,)), ((), ())), preferred_element_type=F32)
                for b in b_refs]
    else:
        accs = [jnp.dot(a, b[...].astype(BF16), preferred_element_type=F32) for b in b_refs]
    exs = []
    for r, kind in zip(ex_refs, kinds):
        v = r[...]
        if kind == "rowtab_tiled":
            v = jnp.tile(v, (1, tn // v.shape[1]))
        exs.append(v)
    o_ref[...] = epilogue(*accs, *exs, *pro_tile).astype(o_ref.dtype)


def matmul(a_list, b_list, *, out_dtype, tm, tn, m_rows=None, n_cols=None, nt=False, epilogue=None,
           extras=(), prologue=None, pro_consts=(), keep_pro=False, name="mm"):
    K = a_list[0].shape[1]
    N = b_list[0].shape[0 if nt else 1] if n_cols is None else n_cols
    M = a_list[0].shape[0] if m_rows is None else m_rows
    tn = _pick(N, tuple(t for t in (tn, 512, 256, 128) if t <= tn))
    assert M % tm == 0 and N % tn == 0, (M, tm, N, tn)
    if epilogue is None:
        epilogue = lambda acc: acc
    if prologue is None:
        assert len(a_list) == 1 and a_list[0].dtype == BF16
    in_specs = [pl.BlockSpec((tm, K), lambda i, j: (i, 0)) for _ in a_list]
    in_specs += [pl.BlockSpec(c.shape, lambda i, j: (0, 0)) for c in pro_consts]
    if nt:
        in_specs += [pl.BlockSpec((tn, K), lambda i, j: (j, 0)) for _ in b_list]
    else:
        in_specs += [pl.BlockSpec((K, tn), lambda i, j: (0, j)) for _ in b_list]
    kinds = []
    ex_arrays = []
    for arr, kind in extras:
        kinds.append(kind)
        ex_arrays.append(arr)
        if kind == "tile":
            in_specs.append(pl.BlockSpec((tm, tn), lambda i, j: (i, j)))
        elif kind == "col":
            in_specs.append(pl.BlockSpec((1, tn), lambda i, j: (0, j)))
        elif kind in ("rowtab", "rowtab_tiled"):
            in_specs.append(pl.BlockSpec((tm, arr.shape[1]), lambda i, j: (i, 0)))
        elif kind == "coltab":
            in_specs.append(pl.BlockSpec((arr.shape[0], tn), lambda i, j: (0, j)))
        else:
            raise ValueError(kind)
    scratch = [pltpu.VMEM((tm, K), BF16)] if prologue is not None else []
    if keep_pro:
        assert prologue is not None and N == K
        scratch.append(pltpu.VMEM((tm, K), F32))
    body = functools.partial(_mm_body, n_a=len(a_list), n_pc=len(pro_consts), n_b=len(b_list), kinds=tuple(kinds),
                             prologue=prologue, epilogue=epilogue, tn=tn, nt=nt, keep_pro=keep_pro)
    return pl.pallas_call(
        body,
        grid=(M // tm, N // tn),
        in_specs=in_specs,
        out_specs=pl.BlockSpec((tm, tn), lambda i, j: (i, j)),
        out_shape=jax.ShapeDtypeStruct((M, N), out_dtype),
        scratch_shapes=scratch,
        compiler_params=_cparams(("parallel", "arbitrary")),
        name=name,
    )(*a_list, *pro_consts, *b_list, *ex_arrays)


def _mmk_body(a_ref, b_ref, *rest, n_ex, epilogue, nk):
    ex_refs = rest[:n_ex]
    o_ref = rest[n_ex]
    acc_ref = rest[n_ex + 1]
    k = pl.program_id(2)

    @pl.when(k == 0)
    def _():
        acc_ref[...] = jnp.zeros_like(acc_ref)

    acc_ref[...] += jnp.dot(a_ref[...], b_ref[...], preferred_element_type=F32)

    @pl.when(k == nk - 1)
    def _():
        o_ref[...] = epilogue(acc_ref[...], *[e[...] for e in ex_refs]).astype(o_ref.dtype)


def matmul_ksplit(a, b, *, out_dtype, tm, tn, tk, m_rows=None, epilogue=None, extras=(), name="mmk"):
    K = a.shape[1]
    N = b.shape[1]
    M = a.shape[0] if m_rows is None else m_rows
    tn = _pick(N, tuple(t for t in (tn, 512, 256, 128) if t <= tn))
    assert M % tm == 0 and N % tn == 0 and K % tk == 0, (M, tm, N, tn, K, tk)
    if epilogue is None:
        epilogue = lambda acc: acc
    in_specs = [pl.BlockSpec((tm, tk), lambda i, j, k: (i, k)),
                pl.BlockSpec((tk, tn), lambda i, j, k: (k, j))]
    ex_arrays = []
    for arr, kind in extras:
        ex_arrays.append(arr)
        if kind == "tile":
            in_specs.append(pl.BlockSpec((tm, tn), lambda i, j, k: (i, j)))
        elif kind == "rowtab":
            in_specs.append(pl.BlockSpec((tm, arr.shape[1]), lambda i, j, k: (i, 0)))
        else:
            raise ValueError(kind)
    nk = K // tk
    body = functools.partial(_mmk_body, n_ex=len(ex_arrays), epilogue=epilogue, nk=nk)
    return pl.pallas_call(
        body,
        grid=(M // tm, N // tn, nk),
        in_specs=in_specs,
        out_specs=pl.BlockSpec((tm, tn), lambda i, j, k: (i, j)),
        out_shape=jax.ShapeDtypeStruct((M, N), out_dtype),
        scratch_shapes=[pltpu.VMEM((tm, tn), F32)],
        compiler_params=_cparams(("parallel", "parallel", "arbitrary")),
        name=name,
    )(a, b, *ex_arrays)


def _mod_body(c_ref, w_ref, b_ref, o_ref):
    c = c_ref[...]
    act = (c * jax.nn.sigmoid(c)).astype(BF16)
    o_ref[...] = jnp.dot(act, w_ref[...].astype(BF16), preferred_element_type=F32) + b_ref[...]


def mod_vectors(cond, w_mod, b_mod):
    depth, d, n = w_mod.shape
    r = cond.shape[0]
    tn = _pick(n, (1024, 512, 256, 128))
    return pl.pallas_call(
        _mod_body,
        grid=(depth, n // tn),
        in_specs=[pl.BlockSpec((r, d), lambda l, j: (0, 0)),
                  pl.BlockSpec((None, d, tn), lambda l, j: (l, 0, j)),
                  pl.BlockSpec((None, 1, tn), lambda l, j: (l, 0, j))],
        out_specs=pl.BlockSpec((None, r, tn), lambda l, j: (l, 0, j)),
        out_shape=jax.ShapeDtypeStruct((depth, r, n), F32),
        compiler_params=_cparams(("parallel", "parallel")),
        name="mod_vectors",
    )(cond, w_mod, b_mod.reshape(depth, 1, n))


def _ln_rows(x):
    mu = jnp.mean(x, axis=-1, keepdims=True)
    xc = x - mu
    var = jnp.mean(xc * xc, axis=-1, keepdims=True)
    return xc * lax.rsqrt(var + LN_EPS)


def _modln_body(x_ref, sh_ref, sc_ref, h_ref):
    h_ref[...] = (_ln_rows(x_ref[...]) * (1.0 + sc_ref[...]) + sh_ref[...]).astype(h_ref.dtype)


def modulate_ln(x, shift, scale, *, rows_per_group, m_rows=None, tr=256):
    M = x.shape[0] if m_rows is None else m_rows
    d = x.shape[1]
    assert M % tr == 0 and rows_per_group % tr == 0
    gmap = lambda i: ((i * tr) // rows_per_group, 0, 0)
    return pl.pallas_call(
        _modln_body,
        grid=(M // tr,),
        in_specs=[pl.BlockSpec((tr, d), lambda i: (i, 0)),
                  pl.BlockSpec((None, 1, d), gmap),
                  pl.BlockSpec((None, 1, d), gmap)],
        out_specs=pl.BlockSpec((tr, d), lambda i: (i, 0)),
        out_shape=jax.ShapeDtypeStruct((M, d), BF16),
        compiler_params=_cparams(("parallel",)),
        name="modulate_ln",
    )(x, shift, scale)


def _resln_body(x_ref, y_ref, gate_ref, g_ref, b_ref, sh_ref, sc_ref, xo_ref, h_ref, *, alpha):
    xn = _ln_rows(alpha * x_ref[...] + gate_ref[...] * y_ref[...]) * g_ref[...] + b_ref[...]
    xo_ref[...] = xn
    h_ref[...] = (_ln_rows(xn) * (1.0 + sc_ref[...]) + sh_ref[...]).astype(h_ref.dtype)


def _resln_last_body(x_ref, y_ref, gate_ref, g_ref, b_ref, xo_ref, *, alpha):
    xo_ref[...] = _ln_rows(alpha * x_ref[...] + gate_ref[...] * y_ref[...]) * g_ref[...] + b_ref[...]


def residual_ln(x, y, gate, ln_g, ln_b, shift, scale, *, alpha, rows_per_group, m_rows=None, tr=256,
                h_dtype=None):
    M = x.shape[0] if m_rows is None else m_rows
    d = x.shape[1]
    assert M % tr == 0 and rows_per_group % tr == 0
    gmap = lambda i: ((i * tr) // rows_per_group, 0, 0)
    row = pl.BlockSpec((tr, d), lambda i: (i, 0))
    vec = pl.BlockSpec((1, d), lambda i: (0, 0))
    gvec = pl.BlockSpec((None, 1, d), gmap)
    if shift is None:
        return pl.pallas_call(
            functools.partial(_resln_last_body, alpha=alpha),
            grid=(M // tr,),
            in_specs=[row, row, gvec, vec, vec],
            out_specs=row,
            out_shape=jax.ShapeDtypeStruct((M, d), F32),
            compiler_params=_cparams(("parallel",)),
            name="residual_ln_last",
        )(x, y, gate, ln_g.reshape(1, d), ln_b.reshape(1, d)), None
    return pl.pallas_call(
        functools.partial(_resln_body, alpha=alpha),
        grid=(M // tr,),
        in_specs=[row, row, gvec, vec, vec, gvec, gvec],
        out_specs=[row, row],
        out_shape=[jax.ShapeDtypeStruct((M, d), F32), jax.ShapeDtypeStruct((M, d), h_dtype or BF16)],
        compiler_params=_cparams(("parallel",)),
        name="residual_ln",
    )(x, y, gate, ln_g.reshape(1, d), ln_b.reshape(1, d), shift, scale)


def _flash_body(qt_ref, *refs, seg_lens, tk):
    n_seg = len(seg_lens)
    kv_refs = refs[:2 * n_seg]
    o_ref = refs[2 * n_seg]
    acc_ref, st_a, st_b = refs[2 * n_seg + 1:]
    bufs = (st_a, st_b)
    qt = qt_ref[...]
    tq = qt.shape[1]
    acc_ref[...] = jnp.zeros_like(acc_ref)
    m = jnp.full((1, tq), -jnp.inf, F32)

    def scores(buf, seg, c, tks):
        k = kv_refs[2 * seg][pl.ds(pl.multiple_of(c * tks, tks), tks), :]
        st = jnp.dot(k, qt, preferred_element_type=F32)
        buf[:tks] = st
        return jnp.max(st, axis=0, keepdims=True)

    def absorb(buf, cmax, seg, c, tks, m_old):
        vt = kv_refs[2 * seg + 1][:, pl.ds(pl.multiple_of(c * tks, tks), tks)]
        vt1 = jnp.concatenate([vt, jnp.ones((FLASH_ONES, tks), BF16)], axis=0)
        m_new = jnp.maximum(m_old, cmax)
        p = jnp.exp2(buf[:tks] - m_new).astype(BF16)
        corr = jnp.exp2(m_old - m_new)
        acc_ref[...] = corr * acc_ref[...] + jnp.dot(vt1, p, preferred_element_type=F32)
        return m_new

    def run_static(chunks, cur, cmax, m):
        for i, (seg, c, tks) in enumerate(chunks):
            nxt = scores(bufs[1 - cur], *chunks[i + 1]) if i + 1 < len(chunks) else None
            m = absorb(bufs[cur], cmax, seg, c, tks, m)
            cur, cmax = 1 - cur, nxt
        return m

    chunk_counts = [(s, min(tk, ln), ln // min(tk, ln)) for s, ln in enumerate(seg_lens)]
    s0, tk0, n0 = chunk_counts[0]
    rest = [(s, c, tks) for s, tks, n in chunk_counts[1:] for c in range(n)]
    cmax_a = scores(st_a, s0, 0, tk0)
    if n0 >= 4 and n0 % 2 == 0:
        def pair(j, carry):
            m, cmax_a = carry
            c0 = 2 * j
            cmax_b = scores(st_b, s0, c0 + 1, tk0)
            m = absorb(st_a, cmax_a, s0, c0, tk0, m)
            cmax_a = scores(st_a, s0, c0 + 2, tk0)
            return absorb(st_b, cmax_b, s0, c0 + 1, tk0, m), cmax_a

        m, cmax_a = lax.fori_loop(0, n0 // 2 - 1, pair, (m, cmax_a))
        m = run_static([(s0, n0 - 2, tk0), (s0, n0 - 1, tk0)] + rest, 0, cmax_a, m)
    else:
        m = run_static([(s0, c, tk0) for c in range(n0)] + rest, 0, cmax_a, m)
    acc = acc_ref[...]
    o_ref[...] = (acc[:V_HEAD] / acc[V_HEAD:V_HEAD + 1]).T.astype(o_ref.dtype)


def flash_attention(qt, k, vt, *, n_batch, q_row0, q_len, segs, tq, tk, name="flash"):
    h = N_HEADS
    nq = q_len // tq
    assert q_len % tq == 0 and q_row0 % tq == 0
    in_specs = [pl.BlockSpec((HEAD_PAD, tq), lambda b, hh, i: (hh, q_row0 // tq + b * nq + i))]
    args = [qt]
    for row0, ln in segs:
        assert row0 % ln == 0
        in_specs.append(pl.BlockSpec((ln, HEAD_PAD), lambda b, hh, i, r=row0 // ln: (r + b, hh)))
        in_specs.append(pl.BlockSpec((V_HEAD, ln), lambda b, hh, i, r=row0 // ln: (hh, r + b)))
        args += [k, vt]
    body = functools.partial(_flash_body, seg_lens=tuple(ln for _, ln in segs), tk=tk)
    return pl.pallas_call(
        body,
        grid=(n_batch, h, nq),
        in_specs=in_specs,
        out_specs=pl.BlockSpec((tq, V_HEAD), lambda b, hh, i: (b * nq + i, hh)),
        out_shape=jax.ShapeDtypeStruct((n_batch * q_len, h * V_HEAD), BF16),
        scratch_shapes=[pltpu.VMEM((V_HEAD + FLASH_ONES, tq), F32),
                        pltpu.VMEM((tk, tq), F32), pltpu.VMEM((tk, tq), F32)],
        compiler_params=_cparams(("parallel", "parallel", "arbitrary")),
        name=name,
    )(*args)


def _chunk_rows(x_ref):
    return jnp.concatenate([x_ref[:, t, :] for t in range(S5_L)], axis=1).astype(BF16)


def _s5_drive_body(x_ref, w_ref, o_ref):
    res = jnp.dot(_chunk_rows(x_ref), w_ref[...], preferred_element_type=F32)
    for gl in range(S5_SG):
        o_ref[:, gl, :] = res[:, gl * 4 * S5_STATE:(gl + 1) * 4 * S5_STATE]


def _s5_out_body(x_ref, sf_ref, sb_ref, t_ref, q_ref, o_ref):
    s = jnp.concatenate([sf_ref[:, gl, :] for gl in range(S5_SG)]
                        + [sb_ref[:, gl, :] for gl in range(S5_SG)], axis=1).astype(BF16)
    res = (jnp.dot(_chunk_rows(x_ref), t_ref[...], preferred_element_type=F32)
           + jnp.dot(s, q_ref[...], preferred_element_type=F32))
    for t in range(S5_L):
        o_ref[:, t, :] = res[:, t * LANE:(t + 1) * LANE]


def _s5_scan_body(wf_ref, wb_ref, af_ref, bf_ref, ab_ref, bb_ref, sf_ref, sb_ref, st_f, st_b):
    @pl.when(pl.program_id(1) == 0)
    def _():
        st_f[...] = jnp.zeros_like(st_f)
        st_b[...] = jnp.zeros_like(st_b)

    a_f, b_f, a_b, b_b = af_ref[...], bf_ref[...], ab_ref[...], bb_ref[...]
    cb = wf_ref.shape[0]

    def step(c, carry):
        s_f, s_b = carry
        cr = cb - 1 - c
        sf_ref[c] = s_f
        sb_ref[cr] = s_b
        n_f = a_f * s_f + b_f * pltpu.roll(s_f, S5_STATE, 1) + wf_ref[c]
        n_b = a_b * s_b + b_b * pltpu.roll(s_b, S5_STATE, 1) + wb_ref[cr]
        return n_f, n_b

    s_f, s_b = lax.fori_loop(0, cb, step, (st_f[...], st_b[...]))
    st_f[...] = s_f
    st_b[...] = s_b


def s5_scan(w3, lam, *, n_batch, n_xc, n_cc, cb):
    nch, g, _ = w3.shape
    p2 = 2 * S5_STATE
    n_xb, n_cb = n_xc // cb, n_cc // cb
    ctx0 = n_batch * n_xb

    def fwd_blk(b, j):
        return jnp.where(j < n_cb, ctx0 + b * n_cb + j, b * n_xb + j - n_cb)

    def bwd_blk(b, j):
        return jnp.where(j < n_cb, ctx0 + b * n_cb + (n_cb - 1 - j), b * n_xb + (n_xb - 1 - (j - n_cb)))

    coef = pl.BlockSpec((g, p2), lambda b, j: (0, 0))
    return pl.pallas_call(
        _s5_scan_body,
        grid=(n_batch, n_xb + n_cb),
        in_specs=[pl.BlockSpec((cb, g, p2), lambda b, j: (fwd_blk(b, j), 0, 0)),
                  pl.BlockSpec((cb, g, p2), lambda b, j: (bwd_blk(b, j), 0, 1)),
                  coef, coef, coef, coef],
        out_specs=[pl.BlockSpec((cb, g, p2), lambda b, j: (fwd_blk(b, j), 0, 0)),
                   pl.BlockSpec((cb, g, p2), lambda b, j: (bwd_blk(b, j), 0, 0))],
        out_shape=[jax.ShapeDtypeStruct((nch, g, p2), F32)] * 2,
        scratch_shapes=[pltpu.VMEM((g, p2), F32), pltpu.VMEM((g, p2), F32)],
        compiler_params=_cparams(("arbitrary", "arbitrary")),
        name="s5_scan",
    )(w3, w3, *lam)


def _s5_tables(a_re, a_im, log_dt, b_re, b_im, c_re, c_im):
    L, P, Hh = S5_L, S5_STATE, S5_GROUP
    hp = lax.Precision.HIGHEST
    dt = jnp.exp(log_dt.astype(F32))[..., None]
    ar, ai = a_re.astype(F32), a_im.astype(F32)
    j = jnp.arange(L + 1, dtype=F32)[:, None, None, None]
    mag = jnp.exp(j * ar * dt)
    pr, pi = mag * jnp.cos(j * ai * dt), mag * jnp.sin(j * ai * dt)
    lr, li = pr[1], pi[1]
    nr = lr - 1.0
    den = ar * ar + ai * ai
    f_re = ((nr * ar + li * ai) / den)[..., None]
    f_im = ((li * ar - nr * ai) / den)[..., None]
    br, bi = b_re.astype(F32), b_im.astype(F32)
    bb_re = f_re * br - f_im * bi
    bb_im = f_re * bi + f_im * br
    cr, ci = c_re.astype(F32), c_im.astype(F32)

    zr = pr[:L, ..., None] * bb_re - pi[:L, ..., None] * bb_im
    zi = pr[:L, ..., None] * bb_im + pi[:L, ..., None] * bb_re
    kj = (jnp.einsum('dghp,jdgpk->jdghk', cr, zr, precision=hp)
          - jnp.einsum('dghp,jdgpk->jdghk', ci, zi, precision=hp))
    t_idx = jnp.arange(L)
    lag = t_idx[None, :] - t_idx[:, None]
    kf = kj[:, 0][jnp.clip(lag, 0, L - 1)]
    kb = kj[:, 1][jnp.clip(-lag, 0, L - 1)]
    tm4 = (jnp.where((lag >= 0)[:, :, None, None, None], kf, 0.0)
           + jnp.where((lag <= 0)[:, :, None, None, None], kb, 0.0))
    g = tm4.shape[2]
    tmat = jnp.transpose(tm4, (2, 0, 4, 1, 3)).reshape(g, L * Hh, L * Hh)

    kk = jnp.arange(L)
    pf_r, pf_i = pr[L - 1 - kk, 0], pi[L - 1 - kk, 0]
    pb_r, pb_i = pr[kk, 1], pi[kk, 1]
    wf_re = pf_r[..., None] * bb_re[0] - pf_i[..., None] * bb_im[0]
    wf_im = pf_r[..., None] * bb_im[0] + pf_i[..., None] * bb_re[0]
    wb_re = pb_r[..., None] * bb_re[1] - pb_i[..., None] * bb_im[1]
    wb_im = pb_r[..., None] * bb_im[1] + pb_i[..., None] * bb_re[1]
    wcat = jnp.concatenate([wf_re, wf_im, wb_re, wb_im], axis=2)
    wmat = jnp.transpose(wcat, (1, 0, 3, 2)).reshape(g, L * Hh, 4 * P)

    qf_r, qf_i = pr[kk + 1, 0], pi[kk + 1, 0]
    qb_r, qb_i = pr[L - kk, 1], pi[L - kk, 1]

    def qpair(c_r, c_i, q_r, q_i):
        return (c_r[None] * q_r[:, :, None, :] - c_i[None] * q_i[:, :, None, :],
                -c_r[None] * q_i[:, :, None, :] - c_i[None] * q_r[:, :, None, :])

    qf_re, qf_im = qpair(cr[0], ci[0], qf_r, qf_i)
    qb_re, qb_im = qpair(cr[1], ci[1], qb_r, qb_i)
    qf = jnp.transpose(jnp.concatenate([qf_re, qf_im], axis=3), (1, 3, 0, 2))
    qb = jnp.transpose(jnp.concatenate([qb_re, qb_im], axis=3), (1, 3, 0, 2))

    n_in = L * S5_SG * Hh
    lane = jnp.arange(n_in)
    rep_t = (jnp.arange(L * Hh)[:, None] == ((lane // LANE) * Hh + lane % Hh)[None, :]).astype(BF16)
    rep_w = (jnp.arange(4 * P)[:, None] == (lane % (4 * P))[None, :]).astype(BF16)
    col_grp_t = (lane // Hh) % S5_SG
    col_grp_w = lane // (4 * P)
    wfull = s5_spread([wmat.astype(BF16)], rep_w, col_grp_w, chunk_rows=True)
    tfull = s5_spread([tmat.astype(BF16)], rep_t, col_grp_t, chunk_rows=True)
    qfull = s5_spread([qf.reshape(g, 2 * P, L * Hh).astype(BF16), qb.reshape(g, 2 * P, L * Hh).astype(BF16)],
                      rep_t, col_grp_t, chunk_rows=False)

    def lam_tiles(d):
        return (jnp.concatenate([pr[L, d], pr[L, d]], axis=-1), jnp.concatenate([-pi[L, d], pi[L, d]], axis=-1))

    lam = lam_tiles(0) + lam_tiles(1)
    return wfull, tfull, qfull, lam


def _s5_spread_body(*refs, n_src, chunk_rows):
    src_refs = refs[:n_src]
    rep_ref, cgrp_ref, o_ref = refs[n_src:]
    if chunk_rows:
        pieces = [src_refs[0][a, k * S5_GROUP:(k + 1) * S5_GROUP, :] for k in range(S5_L) for a in range(S5_SG)]
        per_grp = S5_GROUP
    else:
        pieces = [r[a] for r in src_refs for a in range(S5_SG)]
        per_grp = src_refs[0].shape[1]
    rows = jnp.concatenate(pieces, axis=0)
    full = jnp.dot(rows, rep_ref[...], preferred_element_type=F32)
    assert per_grp & (per_grp - 1) == 0 and S5_SG & (S5_SG - 1) == 0
    row_grp = (lax.broadcasted_iota(jnp.int32, full.shape, 0) >> (per_grp.bit_length() - 1)) & (S5_SG - 1)
    o_ref[...] = jnp.where(row_grp == cgrp_ref[...], full, 0.0).astype(o_ref.dtype)


def s5_spread(srcs, rep, col_grp, *, chunk_rows):
    g, r, c = srcs[0].shape
    n = rep.shape[1]
    n_rows = len(srcs) * S5_SG * r
    return pl.pallas_call(
        functools.partial(_s5_spread_body, n_src=len(srcs), chunk_rows=chunk_rows),
        grid=(g // S5_SG,),
        in_specs=[pl.BlockSpec((S5_SG, r, c), lambda s: (s, 0, 0)) for _ in srcs]
        + [pl.BlockSpec((c, n), lambda s: (0, 0)), pl.BlockSpec((1, n), lambda s: (0, 0))],
        out_specs=pl.BlockSpec((None, n_rows, n), lambda s: (s, 0, 0)),
        out_shape=jax.ShapeDtypeStruct((g // S5_SG, n_rows, n), BF16),
        compiler_params=_cparams(("parallel",)),
        name="s5_spread",
    )(*srcs, rep, col_grp.astype(jnp.int32).reshape(1, n))


def s5_mix(u, tables, *, n_batch, seq, ctx_len):
    wfull, tfull, qfull, lam = tables
    L, P = S5_L, S5_STATE
    t_rows, w_tot = u.shape
    g = w_tot // S5_GROUP
    nsg = g // S5_SG
    nch = t_rows // L
    n_in = L * LANE
    rb = max(r for r in range(8, 265, 8) if nch % r == 0)
    u3 = u.reshape(nch, L, w_tot)
    xspec = pl.BlockSpec((rb, L, LANE), lambda s, i: (i, 0, s))
    wspec = lambda k, n: pl.BlockSpec((None, k, n), lambda s, i: (s, 0, 0))

    w3 = pl.pallas_call(
        _s5_drive_body,
        grid=(nsg, nch // rb),
        in_specs=[xspec, wspec(n_in, S5_SG * 4 * P)],
        out_specs=pl.BlockSpec((rb, S5_SG, 4 * P), lambda s, i: (i, s, 0)),
        out_shape=jax.ShapeDtypeStruct((nch, g, 4 * P), F32),
        compiler_params=_cparams(("parallel", "parallel")),
        name="s5_drive",
    )(u3, wfull)

    n_xc, n_cc = seq // L, ctx_len // L
    cb = _pick(math.gcd(n_xc, n_cc), (16, 8, 4, 2, 1))
    sf, sb = s5_scan(w3, lam, n_batch=n_batch, n_xc=n_xc, n_cc=n_cc, cb=cb)

    sspec = pl.BlockSpec((rb, S5_SG, 2 * P), lambda s, i: (i, s, 0))
    y3 = pl.pallas_call(
        _s5_out_body,
        grid=(nsg, nch // rb),
        in_specs=[xspec, sspec, sspec, wspec(n_in, n_in), wspec(S5_SG * 4 * P, n_in)],
        out_specs=xspec,
        out_shape=jax.ShapeDtypeStruct((nch, L, w_tot), F32),
        compiler_params=_cparams(("parallel", "parallel")),
        name="s5_out",
    )(u3, sf, sb, tfull, qfull)
    return y3.reshape(t_rows, w_tot)


R_E1, R_E2, R_W1, R_W2, R_RANK1, R_RANK2 = range(6)


def _router_body(lg_ref, b_ref, meta_ref, cnt_ref, carry_ref, *, n_exp):
    @pl.when(pl.program_id(0) == 0)
    def _():
        carry_ref[...] = jnp.zeros_like(carry_ref)

    lg = lg_ref[...] + b_ref[...]
    tr = lg.shape[0]
    lane = lax.broadcasted_iota(jnp.int32, lg.shape, 1).astype(F32)
    neg = jnp.float32(-jnp.inf)
    lg = jnp.where(lane < n_exp, lg, neg)
    m1 = jnp.max(lg, axis=-1, keepdims=True)
    i1 = jnp.min(jnp.where(lg == m1, lane, float(LANE)), axis=-1, keepdims=True)
    lg2 = jnp.where(lane == i1, neg, lg)
    m2 = jnp.max(lg2, axis=-1, keepdims=True)
    i2 = jnp.min(jnp.where(lg2 == m2, lane, float(LANE)), axis=-1, keepdims=True)
    e2 = jnp.exp(m2 - m1)
    den = 1.0 + e2
    sel = jnp.where((lane == i1) | (lane == i2), 1.0, 0.0)
    r_i = lax.broadcasted_iota(jnp.int32, (tr, tr), 0)
    c_i = lax.broadcasted_iota(jnp.int32, (tr, tr), 1)
    tri = jnp.where(r_i > c_i, 1.0, 0.0).astype(BF16)
    before = jnp.dot(tri, sel.astype(BF16), preferred_element_type=F32) + carry_ref[...]
    rank1 = jnp.sum(jnp.where(lane == i1, before, 0.0), axis=-1, keepdims=True)
    rank2 = jnp.sum(jnp.where(lane == i2, before, 0.0), axis=-1, keepdims=True)
    total = carry_ref[...] + jnp.sum(sel, axis=0, keepdims=True)
    carry_ref[...] = total
    cnt_ref[...] = jnp.broadcast_to(total, cnt_ref.shape)
    meta = jnp.zeros_like(lg)
    for idx, val in ((R_E1, i1), (R_E2, i2), (R_W1, 1.0 / den), (R_W2, e2 / den),
                     (R_RANK1, rank1), (R_RANK2, rank2)):
        meta = jnp.where(lane == idx, val, meta)
    meta_ref[...] = meta


def router_top2(logits, b_router_pad, *, n_exp, tr=512):
    m = logits.shape[0]
    tr = _pick(m, (tr, 256, 128, 64, 32, 16, 8))
    return pl.pallas_call(
        functools.partial(_router_body, n_exp=n_exp),
        grid=(m // tr,),
        in_specs=[pl.BlockSpec((tr, LANE), lambda i: (i, 0)), pl.BlockSpec((1, LANE), lambda i: (0, 0))],
        out_specs=[pl.BlockSpec((tr, LANE), lambda i: (i, 0)), pl.BlockSpec((8, LANE), lambda i: (0, 0))],
        out_shape=[jax.ShapeDtypeStruct((m, LANE), F32), jax.ShapeDtypeStruct((8, LANE), F32)],
        scratch_shapes=[pltpu.VMEM((1, LANE), F32)],
        compiler_params=_cparams(("arbitrary",)),
        name="router_top2",
    )(logits, b_router_pad)


DMA_ISSUE_UNROLL = 8


def _rows_wait(src_ref, dst_ref, sem):
    pltpu.make_async_copy(src_ref.at[pl.ds(0, dst_ref.shape[0]), :], dst_ref, sem).wait()


def _fetch_rows(idx_ref, idx_next_ref, src_ref, buf, sem, *, tb, n_per):
    i = pl.program_id(0)
    slot = lax.rem(i, 2)

    def start_block(ref, s):
        def issue(t, carry):
            for k in range(n_per):
                pltpu.make_async_copy(src_ref.at[pl.ds(ref[0, n_per * t + k], 1), :],
                                      buf.at[s, k, pl.ds(t, 1), :], sem.at[s]).start()
            return carry

        lax.fori_loop(0, tb, issue, 0, unroll=DMA_ISSUE_UNROLL // n_per)

    @pl.when(i == 0)
    def _():
        start_block(idx_ref, slot)

    @pl.when(i + 1 < pl.num_programs(0))
    def _():
        start_block(idx_next_ref, 1 - slot)

    for k in range(n_per):
        _rows_wait(src_ref, buf.at[slot, k], sem.at[slot])
    return slot


def _idx_specs(n_blocks, width):
    return [pl.BlockSpec((None, 1, width), lambda i: (i, 0, 0), memory_space=pltpu.SMEM),
            pl.BlockSpec((None, 1, width), lambda i: (jnp.minimum(i + 1, n_blocks - 1), 0, 0),
                         memory_space=pltpu.SMEM)]


def _gather_rows_body(idx_ref, idx_next_ref, src_ref, o_ref, buf, sem, *, tb):
    slot = _fetch_rows(idx_ref, idx_next_ref, src_ref, buf, sem, tb=tb, n_per=1)
    o_ref[...] = buf[slot, 0].astype(o_ref.dtype)


def gather_rows(src, idx, *, out_dtype, tb=256):
    r, d = idx.shape[0], src.shape[1]
    tb = _pick(r, (tb, 128, 64, 32, 16, 8))
    idx3 = idx.reshape(r // tb, 1, tb)
    return pl.pallas_call(
        functools.partial(_gather_rows_body, tb=tb),
        grid=(r // tb,),
        in_specs=_idx_specs(r // tb, tb) + [pl.BlockSpec(memory_space=pl.ANY)],
        out_specs=pl.BlockSpec((tb, d), lambda i: (i, 0)),
        out_shape=jax.ShapeDtypeStruct((r, d), out_dtype),
        scratch_shapes=[pltpu.VMEM((2, 1, tb, d), src.dtype), pltpu.SemaphoreType.DMA((2,))],
        compiler_params=_cparams(("arbitrary",)),
        name="moe_gather_rows",
    )(idx3, idx3, src)


def _gather_combine_body(pos_ref, pos_next_ref, meta_ref, y_ref, x_ref, gate_ref, g_ref, b_ref, *rest,
                         tb, alpha, with_h):
    if with_h:
        sh_ref, sc_ref, xo_ref, h_ref, buf, sem = rest
    else:
        xo_ref, buf, sem = rest
    slot = _fetch_rows(pos_ref, pos_next_ref, y_ref, buf, sem, tb=tb, n_per=TOP_K)
    meta = meta_ref[...]
    ff = meta[:, R_W1:R_W1 + 1] * buf[slot, 0] + meta[:, R_W2:R_W2 + 1] * buf[slot, 1]
    xn = _ln_rows(alpha * x_ref[...] + gate_ref[...] * ff) * g_ref[...] + b_ref[...]
    xo_ref[...] = xn
    if with_h:
        h_ref[...] = (_ln_rows(xn) * (1.0 + sc_ref[...]) + sh_ref[...]).astype(h_ref.dtype)


def gather_combine_ln(y, pos, meta, x, gate, ln_g, ln_b, shift, scale, *, alpha, rows_per_group, tb=256):
    m, d = pos.shape[0], y.shape[1]
    tb = _pick(m, (tb, 128, 64, 32, 16, 8))
    assert rows_per_group % tb == 0
    pos3 = pos.reshape(m // tb, 1, TOP_K * tb)
    with_h = shift is not None
    row = pl.BlockSpec((tb, d), lambda i: (i, 0))
    vec = pl.BlockSpec((1, d), lambda i: (0, 0))
    gvec = pl.BlockSpec((None, 1, d), lambda i: ((i * tb) // rows_per_group, 0, 0))
    in_specs = _idx_specs(m // tb, TOP_K * tb) + [pl.BlockSpec((tb, LANE), lambda i: (i, 0)),
                                                  pl.BlockSpec(memory_space=pl.ANY), row, gvec, vec, vec]
    args = [pos3, pos3, meta, y, x, gate, ln_g.reshape(1, d), ln_b.reshape(1, d)]
    out_specs, out_shape = [row], [jax.ShapeDtypeStruct((m, d), F32)]
    if with_h:
        in_specs += [gvec, gvec]
        args += [shift, scale]
        out_specs.append(row)
        out_shape.append(jax.ShapeDtypeStruct((m, d), BF16))
    outs = pl.pallas_call(
        functools.partial(_gather_combine_body, tb=tb, alpha=alpha, with_h=with_h),
        grid=(m // tb,),
        in_specs=in_specs,
        out_specs=out_specs,
        out_shape=out_shape,
        scratch_shapes=[pltpu.VMEM((2, TOP_K, tb, d), F32), pltpu.SemaphoreType.DMA((2,))],
        compiler_params=_cparams(("arbitrary",)),
        name="moe_gather_combine_ln",
    )(*args)
    return (outs[0], outs[1]) if with_h else (outs[0], None)


def _gmm_up_body(te_ref, nv_ref, a_ref, b1_ref, b3_ref, o_ref):
    del te_ref
    live = pl.program_id(1) < nv_ref[0]

    @pl.when(live)
    def _():
        a = a_ref[...]
        g = jnp.dot(a, b1_ref[...].astype(BF16), preferred_element_type=F32)
        u = jnp.dot(a, b3_ref[...].astype(BF16), preferred_element_type=F32)
        o_ref[...] = (g * jax.nn.sigmoid(g) * u).astype(o_ref.dtype)

    @pl.when(jnp.logical_not(live))
    def _():
        o_ref[...] = jnp.zeros_like(o_ref)


def _gmm_down_body(te_ref, nv_ref, a_ref, b_ref, o_ref):
    del te_ref
    live = pl.program_id(1) < nv_ref[0]

    @pl.when(live)
    def _():
        o_ref[...] = jnp.dot(a_ref[...], b_ref[...].astype(BF16), preferred_element_type=F32)

    @pl.when(jnp.logical_not(live))
    def _():
        o_ref[...] = jnp.zeros_like(o_ref)


def grouped_swiglu(xs, w1, w3, w2, tile_expert, n_valid, *, tm):
    r, d = xs.shape
    f = w1.shape[2]
    n_tiles = r // tm
    tn_up = _pick(f, (512, 256, 128))
    tn_dn = _pick(d, (1024, 512, 256, 128))
    act = pl.pallas_call(
        _gmm_up_body,
        grid_spec=pltpu.PrefetchScalarGridSpec(
            num_scalar_prefetch=2,
            grid=(f // tn_up, n_tiles),
            in_specs=[pl.BlockSpec((tm, d), lambda j, i, te, nv: (i, 0)),
                      pl.BlockSpec((None, d, tn_up), lambda j, i, te, nv: (te[i], 0, j)),
                      pl.BlockSpec((None, d, tn_up), lambda j, i, te, nv: (te[i], 0, j))],
            out_specs=pl.BlockSpec((tm, tn_up), lambda j, i, te, nv: (i, j))),
        out_shape=jax.ShapeDtypeStruct((r, f), BF16),
        compiler_params=_cparams(("arbitrary", "arbitrary")),
        name="moe_up",
    )(tile_expert, n_valid, xs, w1, w3)
    return pl.pallas_call(
        _gmm_down_body,
        grid_spec=pltpu.PrefetchScalarGridSpec(
            num_scalar_prefetch=2,
            grid=(d // tn_dn, n_tiles),
            in_specs=[pl.BlockSpec((tm, f), lambda j, i, te, nv: (i, 0)),
                      pl.BlockSpec((None, f, tn_dn), lambda j, i, te, nv: (te[i], 0, j))],
            out_specs=pl.BlockSpec((tm, tn_dn), lambda j, i, te, nv: (i, j))),
        out_shape=jax.ShapeDtypeStruct((r, d), F32),
        compiler_params=_cparams(("arbitrary", "arbitrary")),
        name="moe_down",
    )(tile_expert, n_valid, act, w2)


def sparse_moe(hf, meta, counts, w1, w3, w2, resid, *, tm=512):
    m = hf.shape[0]
    n_exp = w1.shape[0]
    tm = _pick(m, (tm, 256, 128))
    cnt = counts.astype(jnp.int32)
    padded = (cnt + tm - 1) // tm * tm
    ends = jnp.cumsum(padded)
    offs = ends - padded
    e1 = meta[:, R_E1].astype(jnp.int32)
    e2 = meta[:, R_E2].astype(jnp.int32)
    pos = jnp.stack([offs[e1] + meta[:, R_RANK1].astype(jnp.int32),
                     offs[e2] + meta[:, R_RANK2].astype(jnp.int32)], axis=1)
    n_tiles = TOP_K * m // tm + n_exp
    tile_expert = jnp.minimum(jnp.searchsorted(ends, jnp.arange(n_tiles) * tm, side="right"),
                              n_exp - 1).astype(jnp.int32)
    n_valid = (ends[-1:] // tm).astype(jnp.int32)
    src_tok = jnp.zeros((n_tiles * tm,), jnp.int32).at[pos.reshape(-1)].set(
        jnp.repeat(jnp.arange(m, dtype=jnp.int32), TOP_K))
    xs = gather_rows(hf, src_tok, out_dtype=BF16)
    y = grouped_swiglu(xs, w1, w3, w2, tile_expert, n_valid, tm=tm)
    args, kwargs = resid
    return gather_combine_ln(y, pos, meta, *args, **kwargs)


def _rms_pro(x, gain):
    return x * lax.rsqrt(jnp.mean(x * x, axis=-1, keepdims=True) + RMS_EPS) * gain


def _gelu_tanh(x):
    return 0.5 * x * (1.0 + jnp.tanh(math.sqrt(2.0 / math.pi) * (x + 0.044715 * (x * x * x))))


def _rope_apply(x, cos, sin_up, sin_dn):
    n = x.shape[-1]
    return x * cos + pltpu.roll(x, n - QK_ROPE // 4, 1) * sin_up + pltpu.roll(x, QK_ROPE // 4, 1) * sin_dn


def _rope_heads_t(x, cos, sin_up, sin_dn):
    q4 = QK_ROPE // 4
    parts = []
    for base in range(0, x.shape[0], HEAD_PAD):
        seg = x[base + QK_NOPE:base + QK_NOPE + QK_ROPE]
        rot = seg * cos + pltpu.roll(seg, QK_ROPE - q4, 0) * sin_up + pltpu.roll(seg, q4, 0) * sin_dn
        parts += [x[base:base + QK_NOPE], rot, x[base + QK_NOPE + QK_ROPE:base + HEAD_PAD]]
    return jnp.concatenate(parts, axis=0)


def _rope_tables(n_batch, seq, n_ctx_rows):
    nf = QK_ROPE // 4
    pos = jnp.arange(seq)
    row = (pos // GRID_W).astype(F32)
    col = (pos % GRID_W).astype(F32)
    inv = ROPE_THETA ** (-jnp.arange(nf, dtype=F32) / nf)
    ar, ac = row[:, None] * inv, col[:, None] * inv
    z = jnp.zeros((seq, nf), F32)
    cos64 = jnp.concatenate([jnp.cos(ar), jnp.cos(ar), jnp.cos(ac), jnp.cos(ac)], axis=1)
    up64 = jnp.concatenate([-jnp.sin(ar), z, -jnp.sin(ac), z], axis=1)
    dn64 = jnp.concatenate([z, jnp.sin(ar), z, jnp.sin(ac)], axis=1)

    def place(t64, fill):
        full = jnp.full((seq, HEAD_PAD), fill, F32).at[:, QK_NOPE:QK_NOPE + QK_ROPE].set(t64)
        full = jnp.tile(full, (n_batch, 1))
        return jnp.concatenate([full, jnp.full((n_ctx_rows, HEAD_PAD), fill, F32)], axis=0)

    return place(cos64, 1.0), place(up64, 0.0), place(dn64, 0.0)


def _pad_cols(w, n):
    return jnp.pad(w, ((0, 0), (0, n - w.shape[1])))


def _head_cat_cols(w_a, w_b, da, db):
    k = w_a.shape[0]
    parts = [w_a.reshape(k, N_HEADS, da)]
    if w_b is not None:
        parts.append(w_b.reshape(k, N_HEADS, db))
    used = da + (db if w_b is not None else 0)
    parts.append(jnp.zeros((k, N_HEADS, HEAD_PAD - used), w_a.dtype))
    return jnp.concatenate(parts, axis=2).reshape(k, N_HEADS * HEAD_PAD)


def kernel(x, c, ctx, c_ctx, w_mod, b_mod, w_in, b_gate, q_norm, w_uq, kv_norm, w_ukv, w_branch_mla,
           s5_a_re, s5_a_im, s5_log_dt, s5_b_re, s5_b_im, s5_c_re, s5_c_im, s5_d, w_glu, b_glu,
           w_branch_s5, w_out, ln_mix_g, ln_mix_b, ln_ffn_g, ln_ffn_b, ffn_w1, ffn_w3, ffn_w2,
           moe_w_router, moe_b_router, moe_w1, moe_w3, moe_w2):
    B, N, D = x.shape
    C = ctx.shape[1]
    depth = w_mod.shape[0]
    QL, KL = q_norm.shape[1], kv_norm.shape[1]
    SW = s5_d.shape[1]
    H = N_HEADS
    NX, NC_ROWS = B * N, B * C
    T = NX + NC_ROWS
    alpha = (2 * depth) ** 0.25
    q_scale = (QK_NOPE + QK_ROPE) ** -0.5 * math.log2(math.e)
    o_ckv, o_kr, o_u, o_g = QL, QL + KL, QL + KL + QK_ROPE, QL + KL + QK_ROPE + SW
    assert N % C == 0 and N % 256 == 0 and NC_ROWS % 256 == 0

    tm_all = _pick(T, (1536, 1024, 768, 512, 384, 256, 128))
    tm_x = _pick(NX, (1024, 512, 256, 128))
    tile_n = lambda n: _pick(n, (512, 256, 128))

    n_cond = B + 1
    cond = jnp.concatenate([c, c_ctx[None], jnp.zeros((-n_cond % 8, D), F32)], axis=0)
    mods = mod_vectors(cond, w_mod, b_mod)
    mods = mods.reshape(depth, cond.shape[0], 6, D)

    def mvec(l, k):
        return mods[l, :n_cond, k][:, None, :]

    rope_cos, rope_up, rope_dn = _rope_tables(B, N, NC_ROWS)
    rope_cos_t, rope_up_t, rope_dn_t = (t[:, QK_NOPE:QK_NOPE + QK_ROPE].T for t in (rope_cos, rope_up, rope_dn))
    xt =jnp.concatenate([x.reshape(NX, D), ctx.reshape(NC_ROWS, D)], axis=0)
    h = modulate_ln(xt, mvec(0, 0), mvec(0, 1), rows_per_group=N)

    for l in range(depth):
        need_ctx = l < depth - 1
        rows = T if need_ctx else NX
        tm_r = tm_all if need_ctx else tm_x

        wi = w_in[l]
        w_cq = wi[:, :o_ckv].astype(BF16)
        w_ckv = wi[:, o_ckv:o_kr].astype(BF16)
        w_kr = jnp.concatenate(
            [jnp.zeros((D, QK_NOPE), F32), wi[:, o_kr:o_u],
             jnp.zeros((D, HEAD_PAD - QK_NOPE - QK_ROPE), F32)], axis=1).astype(BF16)
        w_u = wi[:, o_u:o_g].astype(BF16)
        w_gm = wi[:, o_g:o_g + D].astype(BF16)
        w_gs = wi[:, o_g + D:].astype(BF16)
        wq = w_uq[l].reshape(QL, H, QK_NOPE + QK_ROPE)
        w_q = _head_cat_cols(wq[:, :, :QK_NOPE].reshape(QL, -1), wq[:, :, QK_NOPE:].reshape(QL, -1),
                             QK_NOPE, QK_ROPE).astype(BF16)
        wkv = w_ukv[l].reshape(KL, H, QK_NOPE + V_HEAD)
        w_k = _head_cat_cols(wkv[:, :, :QK_NOPE].reshape(KL, -1), None, QK_NOPE, 0).astype(BF16)
        w_v = wkv[:, :, QK_NOPE:].reshape(KL, H * V_HEAD).astype(BF16)

        cqn = matmul([h], [w_cq], out_dtype=BF16, tm=tm_r // 2, tn=QL, m_rows=rows,
                     epilogue=_rms_pro, extras=((q_norm[l].reshape(1, QL), "col"),), name="in_cq")
        ckvn = matmul([h], [w_ckv], out_dtype=BF16, tm=tm_all, tn=KL,
                      epilogue=_rms_pro, extras=((kv_norm[l].reshape(1, KL), "col"),), name="in_ckv")
        krp = matmul([h], [w_kr], out_dtype=F32, tm=tm_all, tn=HEAD_PAD, epilogue=_rope_apply,
                     extras=((rope_cos, "rowtab"), (rope_up, "rowtab"), (rope_dn, "rowtab")), name="in_kr")
        u = matmul([h], [w_u], out_dtype=F32, tm=tm_all, tn=tile_n(SW), name="in_s5")

        qt = matmul([w_q.T], [cqn], nt=True, out_dtype=BF16, tm=_pick(H * HEAD_PAD, (1024, 512, 256)), tn=512,
                    n_cols=rows, epilogue=lambda acc, cs, up, dn: _rope_heads_t(acc, cs, up, dn) * q_scale,
                    extras=((rope_cos_t, "coltab"), (rope_up_t, "coltab"), (rope_dn_t, "coltab")),
                    name="mla_qt")
        kh = matmul([ckvn], [w_k], out_dtype=BF16, tm=tm_all, tn=512,
                    epilogue=lambda acc, kr: acc + jnp.tile(kr, (1, acc.shape[1] // HEAD_PAD)),
                    extras=((krp, "rowtab"),), name="mla_k")
        vt = matmul([w_v.T], [ckvn], nt=True, out_dtype=BF16, tm=_pick(H * V_HEAD, (1024, 512, 256, 128)),
                    tn=512, name="mla_vt")
        tq = _pick(N, (512, 256, 128))
        o_x = flash_attention(qt, kh, vt, n_batch=B, q_row0=0, q_len=N, segs=[(0, N), (NX, C)],
                              tq=tq, tk=1024, name="flash_x")
        if need_ctx:
            o_c = flash_attention(qt, kh, vt, n_batch=B, q_row0=NX, q_len=C, segs=[(NX, C)],
                                  tq=_pick(C, (256, 128)), tk=512, name="flash_ctx")
            o_all = jnp.concatenate([o_x, o_c], axis=0)
        else:
            o_all = o_x

        tables = _s5_tables(s5_a_re[l], s5_a_im[l], s5_log_dt[l], s5_b_re[l], s5_b_im[l],
                            s5_c_re[l], s5_c_im[l])
        y = s5_mix(u, tables, n_batch=B, seq=N, ctx_len=C)
        d_row = s5_d[l].reshape(1, SW)
        glu_pro = lambda yv, uv, dv: _gelu_tanh(yv + dv * uv)
        ys = matmul([y, u], [w_glu[l].astype(BF16)], out_dtype=BF16, tm=tm_r // 2, tn=tile_n(SW), m_rows=rows,
                    prologue=glu_pro, pro_consts=(d_row,), keep_pro=True,
                    epilogue=lambda acc, bv, gg: gg * jax.nn.sigmoid(acc + bv),
                    extras=((b_glu[l].reshape(1, SW), "col"),), name="s5_glu")

        bg = b_gate[l]
        m1 = matmul([o_all], [w_branch_mla[l].astype(BF16)], out_dtype=F32, tm=tm_r, tn=512, m_rows=rows,
                    name="branch_mla")
        gm = matmul([h], [w_gm], out_dtype=F32, tm=tm_r, tn=512, m_rows=rows,
                    epilogue=lambda acc, bv, mv: jax.nn.sigmoid(acc + bv) * mv,
                    extras=((bg[:D].reshape(1, D), "col"), (m1, "tile")), name="gate_mla")
        m2 = matmul([ys], [w_branch_s5[l].astype(BF16)], out_dtype=F32, tm=tm_r, tn=512, m_rows=rows,
                    name="branch_s5")
        merged = matmul([h], [w_gs], out_dtype=BF16, tm=tm_r, tn=512, m_rows=rows,
                        epilogue=lambda acc, bv, mv, pv: jax.nn.sigmoid(acc + bv) * mv + pv,
                        extras=((bg[D:].reshape(1, D), "col"), (m2, "tile"), (gm, "tile")), name="gate_s5")
        mix = matmul([merged], [w_out[l].astype(BF16)], out_dtype=F32, tm=tm_r, tn=512, m_rows=rows,
                     name="out_proj")
        xt, h2 = residual_ln(xt, mix, mvec(l, 2), ln_mix_g[l], ln_mix_b[l], mvec(l, 3), mvec(l, 4),
                             alpha=alpha, rows_per_group=N, m_rows=rows, h_dtype=BF16 if l % 2 == 0 else F32)

        if l % 2 == 0:
            fi = l // 2
            dff = ffn_w1.shape[2]
            act = matmul([h2], [ffn_w1[fi].astype(BF16), ffn_w3[fi].astype(BF16)], out_dtype=BF16, tm=tm_r,
                         tn=256, m_rows=rows, epilogue=lambda a, b: a * jax.nn.sigmoid(a) * b, name="ffn_up")
            tk = max(k for k in range(LANE, 5633, LANE) if dff % k == 0)
            ff = matmul_ksplit(act, ffn_w2[fi].astype(BF16), out_dtype=F32, tm=tm_r // 2, tn=1024, tk=tk,
                               m_rows=rows, name="ffn_down")
        else:
            mi = l // 2
            n_exp = moe_w_router.shape[2]
            w_r = _pad_cols(moe_w_router[mi], LANE).astype(BF16)
            b_r = jnp.pad(moe_b_router[mi], (0, LANE - n_exp)).reshape(1, LANE)
            logits = matmul([h2], [w_r], out_dtype=F32, tm=tm_r // 2, tn=LANE, m_rows=rows,
                            prologue=lambda a: a, name="router_logits")
            meta, counts = router_top2(logits, b_r, n_exp=n_exp)
            ff = None
        nxt = (mvec(l + 1, 0), mvec(l + 1, 1)) if need_ctx else (None, None)
        if ff is None:
            resid = ((xt[:rows], mvec(l, 5), ln_ffn_g[l], ln_ffn_b[l], *nxt), dict(alpha=alpha, rows_per_group=N))
            xt, h = sparse_moe(h2, meta, counts[0, :n_exp], moe_w1[mi], moe_w3[mi], moe_w2[mi], resid)
        else:
            xt, h = residual_ln(xt, ff, mvec(l, 5), ln_ffn_g[l], ln_ffn_b[l], *nxt,
                                alpha=alpha, rows_per_group=N, m_rows=rows)
    return xt[:NX].reshape(B, N, D)
```

```python
import functools
import math

import jax
import jax.numpy as jnp
from jax import lax
from jax.experimental import pallas as pl
from jax.experimental.pallas import tpu as pltpu

N_HEADS = 32
QK_NOPE = 128
QK_ROPE = 64
V_HEAD = 128
ROPE_THETA = 10000.0
GRID_W = 64
S5_GROUP = 16
S5_STATE = 64
TOP_K = 2
LN_EPS = 1e-6
RMS_EPS = 1e-6

HEAD_PAD = 256
S5_L = 16
FLASH_ONES = 16
LANE = 128
S5_SG = LANE // S5_GROUP
VMEM_LIMIT_BYTES = 56 * 2**20

F32 = jnp.float32
BF16 = jnp.bfloat16


def _cparams(sem):
    return pltpu.CompilerParams(dimension_semantics=sem, vmem_limit_bytes=VMEM_LIMIT_BYTES)


def _pick(n, prefs):
    for p in prefs:
        if n % p == 0:
            return p
    raise ValueError(f"no tile in {prefs} divides {n}")


def _mm_body(*refs, n_a, n_pc, n_b, kinds, prologue, epilogue, tn, nt, keep_pro):
    a_refs = refs[:n_a]
    pc_refs = refs[n_a:n_a + n_pc]
    b_refs = refs[n_a + n_pc:n_a + n_pc + n_b]
    ex_refs = refs[n_a + n_pc + n_b:n_a + n_pc + n_b + len(kinds)]
    o_ref = refs[n_a + n_pc + n_b + len(kinds)]
    pro_tile = []
    if prologue is not None:
        a_s = refs[n_a + n_pc + n_b + len(kinds) + 1]
        a_f = refs[n_a + n_pc + n_b + len(kinds) + 2] if keep_pro else None
        j = pl.program_id(1)

        @pl.when(j == 0)
        def _():
            val = prologue(*[r[...] for r in a_refs], *[r[...] for r in pc_refs])
            a_s[...] = val.astype(BF16)
            if keep_pro:
                a_f[...] = val

        a = a_s[...]
        if keep_pro:
            pro_tile = [a_f[:, pl.ds(pl.multiple_of(j * tn, tn), tn)]]
    else:
        a = a_refs[0][...]
    if nt:
        accs = [lax.dot_general(a, b[...], (((1,), (1,)), ((), ())), preferred_element_type=F32)
                for b in b_refs]
    else:
        accs = [jnp.dot(a, b[...].astype(BF16), preferred_element_type=F32) for b in b_refs]
    exs = []
    for r, kind in zip(ex_refs, kinds):
        v = r[...]
        if kind == "rowtab_tiled":
            v = jnp.tile(v, (1, tn // v.shape[1]))
        exs.append(v)
    o_ref[...] = epilogue(*accs, *exs, *pro_tile).astype(o_ref.dtype)


def matmul(a_list, b_list, *, out_dtype, tm, tn, m_rows=None, n_cols=None, nt=False, epilogue=None,
           extras=(), prologue=None, pro_consts=(), keep_pro=False, name="mm"):
    K = a_list[0].shape[1]
    N = b_list[0].shape[0 if nt else 1] if n_cols is None else n_cols
    M = a_list[0].shape[0] if m_rows is None else m_rows
    tn = _pick(N, tuple(t for t in (tn, 512, 256, 128) if t <= tn))
    assert M % tm == 0 and N % tn == 0, (M, tm, N, tn)
    if epilogue is None:
        epilogue = lambda acc: acc
    if prologue is None:
        assert len(a_list) == 1 and a_list[0].dtype == BF16
    in_specs = [pl.BlockSpec((tm, K), lambda i, j: (i, 0)) for _ in a_list]
    in_specs += [pl.BlockSpec(c.shape, lambda i, j: (0, 0)) for c in pro_consts]
    if nt:
        in_specs += [pl.BlockSpec((tn, K), lambda i, j: (j, 0)) for _ in b_list]
    else:
        in_specs += [pl.BlockSpec((K, tn), lambda i, j: (0, j)) for _ in b_list]
    kinds = []
    ex_arrays = []
    for arr, kind in extras:
        kinds.append(kind)
        ex_arrays.append(arr)
        if kind == "tile":
            in_specs.append(pl.BlockSpec((tm, tn), lambda i, j: (i, j)))
        elif kind == "col":
            in_specs.append(pl.BlockSpec((1, tn), lambda i, j: (0, j)))
        elif kind in ("rowtab", "rowtab_tiled"):
            in_specs.append(pl.BlockSpec((tm, arr.shape[1]), lambda i, j: (i, 0)))
        elif kind == "coltab":
            in_specs.append(pl.BlockSpec((arr.shape[0], tn), lambda i, j: (0, j)))
        else:
            raise ValueError(kind)
    scratch = [pltpu.VMEM((tm, K), BF16)] if prologue is not None else []
    if keep_pro:
        assert prologue is not None and N == K
        scratch.append(pltpu.VMEM((tm, K), F32))
    body = functools.partial(_mm_body, n_a=len(a_list), n_pc=len(pro_consts), n_b=len(b_list), kinds=tuple(kinds),
                             prologue=prologue, epilogue=epilogue, tn=tn, nt=nt, keep_pro=keep_pro)
    return pl.pallas_call(
        body,
        grid=(M // tm, N // tn),
        in_specs=in_specs,
        out_specs=pl.BlockSpec((tm, tn), lambda i, j: (i, j)),
        out_shape=jax.ShapeDtypeStruct((M, N), out_dtype),
        scratch_shapes=scratch,
        compiler_params=_cparams(("parallel", "arbitrary")),
        name=name,
    )(*a_list, *pro_consts, *b_list, *ex_arrays)


def _mmk_body(a_ref, b_ref, *rest, n_ex, epilogue, nk):
    ex_refs = rest[:n_ex]
    o_ref = rest[n_ex]
    acc_ref = rest[n_ex + 1]
    k = pl.program_id(2)

    @pl.when(k == 0)
    def _():
        acc_ref[...] = jnp.zeros_like(acc_ref)

    acc_ref[...] += jnp.dot(a_ref[...], b_ref[...], preferred_element_type=F32)

    @pl.when(k == nk - 1)
    def _():
        o_ref[...] = epilogue(acc_ref[...], *[e[...] for e in ex_refs]).astype(o_ref.dtype)


def matmul_ksplit(a, b, *, out_dtype, tm, tn, tk, m_rows=None, epilogue=None, extras=(), name="mmk"):
    K = a.shape[1]
    N = b.shape[1]
    M = a.shape[0] if m_rows is None else m_rows
    tn = _pick(N, tuple(t for t in (tn, 512, 256, 128) if t <= tn))
    assert M % tm == 0 and N % tn == 0 and K % tk == 0, (M, tm, N, tn, K, tk)
    if epilogue is None:
        epilogue = lambda acc: acc
    in_specs = [pl.BlockSpec((tm, tk), lambda i, j, k: (i, k)),
                pl.BlockSpec((tk, tn), lambda i, j, k: (k, j))]
    ex_arrays = []
    for arr, kind in extras:
        ex_arrays.append(arr)
        if kind == "tile":
            in_specs.append(pl.BlockSpec((tm, tn), lambda i, j, k: (i, j)))
        elif kind == "rowtab":
            in_specs.append(pl.BlockSpec((tm, arr.shape[1]), lambda i, j, k: (i, 0)))
        else:
            raise ValueError(kind)
    nk = K // tk
    body = functools.partial(_mmk_body, n_ex=len(ex_arrays), epilogue=epilogue, nk=nk)
    return pl.pallas_call(
        body,
        grid=(M // tm, N // tn, nk),
        in_specs=in_specs,
        out_specs=pl.BlockSpec((tm, tn), lambda i, j, k: (i, j)),
        out_shape=jax.ShapeDtypeStruct((M, N), out_dtype),
        scratch_shapes=[pltpu.VMEM((tm, tn), F32)],
        compiler_params=_cparams(("parallel", "parallel", "arbitrary")),
        name=name,
    )(a, b, *ex_arrays)


def _mod_body(c_ref, w_ref, b_ref, o_ref):
    c = c_ref[...]
    act = (c * jax.nn.sigmoid(c)).astype(BF16)
    o_ref[...] = jnp.dot(act, w_ref[...].astype(BF16), preferred_element_type=F32) + b_ref[...]


def mod_vectors(cond, w_mod, b_mod):
    depth, d, n = w_mod.shape
    r = cond.shape[0]
    tn = _pick(n, (1024, 512, 256, 128))
    return pl.pallas_call(
        _mod_body,
        grid=(depth, n // tn),
        in_specs=[pl.BlockSpec((r, d), lambda l, j: (0, 0)),
                  pl.BlockSpec((None, d, tn), lambda l, j: (l, 0, j)),
                  pl.BlockSpec((None, 1, tn), lambda l, j: (l, 0, j))],
        out_specs=pl.BlockSpec((None, r, tn), lambda l, j: (l, 0, j)),
        out_shape=jax.ShapeDtypeStruct((depth, r, n), F32),
        compiler_params=_cparams(("parallel", "parallel")),
        name="mod_vectors",
    )(cond, w_mod, b_mod.reshape(depth, 1, n))


def _ln_rows(x):
    mu = jnp.mean(x, axis=-1, keepdims=True)
    xc = x - mu
    var = jnp.mean(xc * xc, axis=-1, keepdims=True)
    return xc * lax.rsqrt(var + LN_EPS)


def _modln_body(x_ref, sh_ref, sc_ref, h_ref):
    h_ref[...] = (_ln_rows(x_ref[...]) * (1.0 + sc_ref[...]) + sh_ref[...]).astype(h_ref.dtype)


def modulate_ln(x, shift, scale, *, rows_per_group, m_rows=None, tr=256):
    M = x.shape[0] if m_rows is None else m_rows
    d = x.shape[1]
    assert M % tr == 0 and rows_per_group % tr == 0
    gmap = lambda i: ((i * tr) // rows_per_group, 0, 0)
    return pl.pallas_call(
        _modln_body,
        grid=(M // tr,),
        in_specs=[pl.BlockSpec((tr, d), lambda i: (i, 0)),
                  pl.BlockSpec((None, 1, d), gmap),
                  pl.BlockSpec((None, 1, d), gmap)],
        out_specs=pl.BlockSpec((tr, d), lambda i: (i, 0)),
        out_shape=jax.ShapeDtypeStruct((M, d), BF16),
        compiler_params=_cparams(("parallel",)),
        name="modulate_ln",
    )(x, shift, scale)


def _resln_body(x_ref, y_ref, gate_ref, g_ref, b_ref, sh_ref, sc_ref, xo_ref, h_ref, *, alpha):
    xn = _ln_rows(alpha * x_ref[...] + gate_ref[...] * y_ref[...]) * g_ref[...] + b_ref[...]
    xo_ref[...] = xn
    h_ref[...] = (_ln_rows(xn) * (1.0 + sc_ref[...]) + sh_ref[...]).astype(h_ref.dtype)


def _resln_last_body(x_ref, y_ref, gate_ref, g_ref, b_ref, xo_ref, *, alpha):
    xo_ref[...] = _ln_rows(alpha * x_ref[...] + gate_ref[...] * y_ref[...]) * g_ref[...] + b_ref[...]


def residual_ln(x, y, gate, ln_g, ln_b, shift, scale, *, alpha, rows_per_group, m_rows=None, tr=256,
                h_dtype=None):
    M = x.shape[0] if m_rows is None else m_rows
    d = x.shape[1]
    assert M % tr == 0 and rows_per_group % tr == 0
    gmap = lambda i: ((i * tr) // rows_per_group, 0, 0)
    row = pl.BlockSpec((tr, d), lambda i: (i, 0))
    vec = pl.BlockSpec((1, d), lambda i: (0, 0))
    gvec = pl.BlockSpec((None, 1, d), gmap)
    if shift is None:
        return pl.pallas_call(
            functools.partial(_resln_last_body, alpha=alpha),
            grid=(M // tr,),
            in_specs=[row, row, gvec, vec, vec],
            out_specs=row,
            out_shape=jax.ShapeDtypeStruct((M, d), F32),
            compiler_params=_cparams(("parallel",)),
            name="residual_ln_last",
        )(x, y, gate, ln_g.reshape(1, d), ln_b.reshape(1, d)), None
    return pl.pallas_call(
        functools.partial(_resln_body, alpha=alpha),
        grid=(M // tr,),
        in_specs=[row, row, gvec, vec, vec, gvec, gvec],
        out_specs=[row, row],
        out_shape=[jax.ShapeDtypeStruct((M, d), F32), jax.ShapeDtypeStruct((M, d), h_dtype or BF16)],
        compiler_params=_cparams(("parallel",)),
        name="residual_ln",
    )(x, y, gate, ln_g.reshape(1, d), ln_b.reshape(1, d), shift, scale)


def _flash_body(qt_ref, *refs, seg_lens, tk):
    n_seg = len(seg_lens)
    kv_refs = refs[:2 * n_seg]
    o_ref = refs[2 * n_seg]
    acc_ref, st_a, st_b = refs[2 * n_seg + 1:]
    bufs = (st_a, st_b)
    qt = qt_ref[...]
    tq = qt.shape[1]
    acc_ref[...] = jnp.zeros_like(acc_ref)
    m = jnp.full((1, tq), -jnp.inf, F32)

    def scores(buf, seg, c, tks):
        k = kv_refs[2 * seg][pl.ds(pl.multiple_of(c * tks, tks), tks), :]
        st = jnp.dot(k, qt, preferred_element_type=F32)
        buf[:tks] = st
        return jnp.max(st, axis=0, keepdims=True)

    def absorb(buf, cmax, seg, c, tks, m_old):
        vt = kv_refs[2 * seg + 1][:, pl.ds(pl.multiple_of(c * tks, tks), tks)]
        vt1 = jnp.concatenate([vt, jnp.ones((FLASH_ONES, tks), BF16)], axis=0)
        m_new = jnp.maximum(m_old, cmax)
        p = jnp.exp2(buf[:tks] - m_new).astype(BF16)
        corr = jnp.exp2(m_old - m_new)
        acc_ref[...] = corr * acc_ref[...] + jnp.dot(vt1, p, preferred_element_type=F32)
        return m_new

    def run_static(chunks, cur, cmax, m):
        for i, (seg, c, tks) in enumerate(chunks):
            nxt = scores(bufs[1 - cur], *chunks[i + 1]) if i + 1 < len(chunks) else None
            m = absorb(bufs[cur], cmax, seg, c, tks, m)
            cur, cmax = 1 - cur, nxt
        return m

    chunk_counts = [(s, min(tk, ln), ln // min(tk, ln)) for s, ln in enumerate(seg_lens)]
    s0, tk0, n0 = chunk_counts[0]
    rest = [(s, c, tks) for s, tks, n in chunk_counts[1:] for c in range(n)]
    cmax_a = scores(st_a, s0, 0, tk0)
    if n0 >= 4 and n0 % 2 == 0:
        def pair(j, carry):
            m, cmax_a = carry
            c0 = 2 * j
            cmax_b = scores(st_b, s0, c0 + 1, tk0)
            m = absorb(st_a, cmax_a, s0, c0, tk0, m)
            cmax_a = scores(st_a, s0, c0 + 2, tk0)
            return absorb(st_b, cmax_b, s0, c0 + 1, tk0, m), cmax_a

        m, cmax_a = lax.fori_loop(0, n0 // 2 - 1, pair, (m, cmax_a))
        m = run_static([(s0, n0 - 2, tk0), (s0, n0 - 1, tk0)] + rest, 0, cmax_a, m)
    else:
        m = run_static([(s0, c, tk0) for c in range(n0)] + rest, 0, cmax_a, m)
    acc = acc_ref[...]
    o_ref[...] = (acc[:V_HEAD] / acc[V_HEAD:V_HEAD + 1]).T.astype(o_ref.dtype)


def flash_attention(qt, k, vt, *, n_batch, q_row0, q_len, segs, tq, tk, name="flash"):
    h = N_HEADS
    nq = q_len // tq
    assert q_len % tq == 0 and q_row0 % tq == 0
    in_specs = [pl.BlockSpec((HEAD_PAD, tq), lambda b, hh, i: (hh, q_row0 // tq + b * nq + i))]
    args = [qt]
    for row0, ln in segs:
        assert row0 % ln == 0
        in_specs.append(pl.BlockSpec((ln, HEAD_PAD), lambda b, hh, i, r=row0 // ln: (r + b, hh)))
        in_specs.append(pl.BlockSpec((V_HEAD, ln), lambda b, hh, i, r=row0 // ln: (hh, r + b)))
        args += [k, vt]
    body = functools.partial(_flash_body, seg_lens=tuple(ln for _, ln in segs), tk=tk)
    return pl.pallas_call(
        body,
        grid=(n_batch, h, nq),
        in_specs=in_specs,
        out_specs=pl.BlockSpec((tq, V_HEAD), lambda b, hh, i: (b * nq + i, hh)),
        out_shape=jax.ShapeDtypeStruct((n_batch * q_len, h * V_HEAD), BF16),
        scratch_shapes=[pltpu.VMEM((V_HEAD + FLASH_ONES, tq), F32),
                        pltpu.VMEM((tk, tq), F32), pltpu.VMEM((tk, tq), F32)],
        compiler_params=_cparams(("parallel", "parallel", "arbitrary")),
        name=name,
    )(*args)


def _chunk_rows(x_ref):
    return jnp.concatenate([x_ref[:, t, :] for t in range(S5_L)], axis=1).astype(BF16)


def _s5_drive_body(x_ref, w_ref, o_ref):
    res = jnp.dot(_chunk_rows(x_ref), w_ref[...], preferred_element_type=F32)
    for gl in range(S5_SG):
        o_ref[:, gl, :] = res[:, gl * 4 * S5_STATE:(gl + 1) * 4 * S5_STATE]


def _s5_out_body(x_ref, sf_ref, sb_ref, t_ref, q_ref, o_ref):
    s = jnp.concatenate([sf_ref[:, gl, :] for gl in range(S5_SG)]
                        + [sb_ref[:, gl, :] for gl in range(S5_SG)], axis=1).astype(BF16)
    res = (jnp.dot(_chunk_rows(x_ref), t_ref[...], preferred_element_type=F32)
           + jnp.dot(s, q_ref[...], preferred_element_type=F32))
    for t in range(S5_L):
        o_ref[:, t, :] = res[:, t * LANE:(t + 1) * LANE]


def _s5_scan_body(wf_ref, wb_ref, af_ref, bf_ref, ab_ref, bb_ref, sf_ref, sb_ref, st_f, st_b):
    @pl.when(pl.program_id(1) == 0)
    def _():
        st_f[...] = jnp.zeros_like(st_f)
        st_b[...] = jnp.zeros_like(st_b)

    a_f, b_f, a_b, b_b = af_ref[...], bf_ref[...], ab_ref[...], bb_ref[...]
    cb = wf_ref.shape[0]

    def step(c, carry):
        s_f, s_b = carry
        cr = cb - 1 - c
        sf_ref[c] = s_f
        sb_ref[cr] = s_b
        n_f = a_f * s_f + b_f * pltpu.roll(s_f, S5_STATE, 1) + wf_ref[c]
        n_b = a_b * s_b + b_b * pltpu.roll(s_b, S5_STATE, 1) + wb_ref[cr]
        return n_f, n_b

    s_f, s_b = lax.fori_loop(0, cb, step, (st_f[...], st_b[...]))
    st_f[...] = s_f
    st_b[...] = s_b


def s5_scan(w3, lam, *, n_batch, n_xc, n_cc, cb):
    nch, g, _ = w3.shape
    p2 = 2 * S5_STATE
    n_xb, n_cb = n_xc // cb, n_cc // cb
    ctx0 = n_batch * n_xb

    def fwd_blk(b, j):
        return jnp.where(j < n_cb, ctx0 + b * n_cb + j, b * n_xb + j - n_cb)

    def bwd_blk(b, j):
        return jnp.where(j < n_cb, ctx0 + b * n_cb + (n_cb - 1 - j), b * n_xb + (n_xb - 1 - (j - n_cb)))

    coef = pl.BlockSpec((g, p2), lambda b, j: (0, 0))
    return pl.pallas_call(
        _s5_scan_body,
        grid=(n_batch, n_xb + n_cb),
        in_specs=[pl.BlockSpec((cb, g, p2), lambda b, j: (fwd_blk(b, j), 0, 0)),
                  pl.BlockSpec((cb, g, p2), lambda b, j: (bwd_blk(b, j), 0, 1)),
                  coef, coef, coef, coef],
        out_specs=[pl.BlockSpec((cb, g, p2), lambda b, j: (fwd_blk(b, j), 0, 0)),
                   pl.BlockSpec((cb, g, p2), lambda b, j: (bwd_blk(b, j), 0, 0))],
        out_shape=[jax.ShapeDtypeStruct((nch, g, p2), F32)] * 2,
        scratch_shapes=[pltpu.VMEM((g, p2), F32), pltpu.VMEM((g, p2), F32)],
        compiler_params=_cparams(("arbitrary", "arbitrary")),
        name="s5_scan",
    )(w3, w3, *lam)


def _s5_tables(a_re, a_im, log_dt, b_re, b_im, c_re, c_im):
    L, P, Hh = S5_L, S5_STATE, S5_GROUP
    hp = lax.Precision.HIGHEST
    dt = jnp.exp(log_dt.astype(F32))[..., None]
    ar, ai = a_re.astype(F32), a_im.astype(F32)
    j = jnp.arange(L + 1, dtype=F32)[:, None, None, None]
    mag = jnp.exp(j * ar * dt)
    pr, pi = mag * jnp.cos(j * ai * dt), mag * jnp.sin(j * ai * dt)
    lr, li = pr[1], pi[1]
    nr = lr - 1.0
    den = ar * ar + ai * ai
    f_re = ((nr * ar + li * ai) / den)[..., None]
    f_im = ((li * ar - nr * ai) / den)[..., None]
    br, bi = b_re.astype(F32), b_im.astype(F32)
    bb_re = f_re * br - f_im * bi
    bb_im = f_re * bi + f_im * br
    cr, ci = c_re.astype(F32), c_im.astype(F32)

    zr = pr[:L, ..., None] * bb_re - pi[:L, ..., None] * bb_im
    zi = pr[:L, ..., None] * bb_im + pi[:L, ..., None] * bb_re
    kj = (jnp.einsum('dghp,jdgpk->dgkjh', cr, zr, precision=hp)
          - jnp.einsum('dghp,jdgpk->dgkjh', ci, zi, precision=hp))
    g = kj.shape[1]
    kall = jnp.concatenate([jnp.flip(kj[1][:, :, 1:], axis=2), kj[0][:, :, :1] + kj[1][:, :, :1],
                            kj[0][:, :, 1:]], axis=2)
    tmat = jnp.stack([kall[:, :, L - 1 - k:2 * L - 1 - k] for k in range(L)], axis=1)
    tmat = tmat.reshape(g, L * Hh, L * Hh)

    kk = jnp.arange(L)
    pf_r, pf_i = pr[L - 1 - kk, 0], pi[L - 1 - kk, 0]
    pb_r, pb_i = pr[kk, 1], pi[kk, 1]
    wf_re = pf_r[..., None] * bb_re[0] - pf_i[..., None] * bb_im[0]
    wf_im = pf_r[..., None] * bb_im[0] + pf_i[..., None] * bb_re[0]
    wb_re = pb_r[..., None] * bb_re[1] - pb_i[..., None] * bb_im[1]
    wb_im = pb_r[..., None] * bb_im[1] + pb_i[..., None] * bb_re[1]
    wcat = jnp.concatenate([wf_re, wf_im, wb_re, wb_im], axis=2)
    wmat = jnp.transpose(wcat, (1, 0, 3, 2)).reshape(g, L * Hh, 4 * P)

    qf_r, qf_i = pr[kk + 1, 0], pi[kk + 1, 0]
    qb_r, qb_i = pr[L - kk, 1], pi[L - kk, 1]

    def qpair(c_r, c_i, q_r, q_i):
        return (c_r[None] * q_r[:, :, None, :] - c_i[None] * q_i[:, :, None, :],
                -c_r[None] * q_i[:, :, None, :] - c_i[None] * q_r[:, :, None, :])

    qf_re, qf_im = qpair(cr[0], ci[0], qf_r, qf_i)
    qb_re, qb_im = qpair(cr[1], ci[1], qb_r, qb_i)
    qf = jnp.transpose(jnp.concatenate([qf_re, qf_im], axis=3), (1, 3, 0, 2))
    qb = jnp.transpose(jnp.concatenate([qb_re, qb_im], axis=3), (1, 3, 0, 2))

    n_in = L * S5_SG * Hh
    lane = jnp.arange(n_in)
    rep_t = (jnp.arange(L * Hh)[:, None] == ((lane // LANE) * Hh + lane % Hh)[None, :]).astype(BF16)
    rep_w = (jnp.arange(4 * P)[:, None] == (lane % (4 * P))[None, :]).astype(BF16)
    col_grp_t = (lane // Hh) % S5_SG
    col_grp_w = lane // (4 * P)
    wfull = s5_spread([wmat.astype(BF16)], rep_w, col_grp_w, chunk_rows=True)
    tfull = s5_spread([tmat.astype(BF16)], rep_t, col_grp_t, chunk_rows=True)
    qfull = s5_spread([qf.reshape(g, 2 * P, L * Hh).astype(BF16), qb.reshape(g, 2 * P, L * Hh).astype(BF16)],
                      rep_t, col_grp_t, chunk_rows=False)

    def lam_tiles(d):
        return (jnp.concatenate([pr[L, d], pr[L, d]], axis=-1), jnp.concatenate([-pi[L, d], pi[L, d]], axis=-1))

    lam = lam_tiles(0) + lam_tiles(1)
    return wfull, tfull, qfull, lam


def _s5_spread_body(*refs, n_src, chunk_rows):
    src_refs = refs[:n_src]
    rep_ref, cgrp_ref, o_ref = refs[n_src:]
    if chunk_rows:
        pieces = [src_refs[0][a, k * S5_GROUP:(k + 1) * S5_GROUP, :] for k in range(S5_L) for a in range(S5_SG)]
        per_grp = S5_GROUP
    else:
        pieces = [r[a] for r in src_refs for a in range(S5_SG)]
        per_grp = src_refs[0].shape[1]
    rows = jnp.concatenate(pieces, axis=0)
    full = jnp.dot(rows, rep_ref[...], preferred_element_type=F32)
    assert per_grp & (per_grp - 1) == 0 and S5_SG & (S5_SG - 1) == 0
    row_grp = (lax.broadcasted_iota(jnp.int32, full.shape, 0) >> (per_grp.bit_length() - 1)) & (S5_SG - 1)
    o_ref[...] = jnp.where(row_grp == cgrp_ref[...], full, 0.0).astype(o_ref.dtype)


def s5_spread(srcs, rep, col_grp, *, chunk_rows):
    g, r, c = srcs[0].shape
    n = rep.shape[1]
    n_rows = len(srcs) * S5_SG * r
    return pl.pallas_call(
        functools.partial(_s5_spread_body, n_src=len(srcs), chunk_rows=chunk_rows),
        grid=(g // S5_SG,),
        in_specs=[pl.BlockSpec((S5_SG, r, c), lambda s: (s, 0, 0)) for _ in srcs]
        + [pl.BlockSpec((c, n), lambda s: (0, 0)), pl.BlockSpec((1, n), lambda s: (0, 0))],
        out_specs=pl.BlockSpec((None, n_rows, n), lambda s: (s, 0, 0)),
        out_shape=jax.ShapeDtypeStruct((g // S5_SG, n_rows, n), BF16),
        compiler_params=_cparams(("parallel",)),
        name="s5_spread",
    )(*srcs, rep, col_grp.astype(jnp.int32).reshape(1, n))


def s5_mix(u, tables, *, n_batch, seq, ctx_len):
    wfull, tfull, qfull, lam = tables
    L, P = S5_L, S5_STATE
    t_rows, w_tot = u.shape
    g = w_tot // S5_GROUP
    nsg = g // S5_SG
    nch = t_rows // L
    n_in = L * LANE
    rb = max(r for r in range(8, 265, 8) if nch % r == 0)
    u3 = u.reshape(nch, L, w_tot)
    xspec = pl.BlockSpec((rb, L, LANE), lambda s, i: (i, 0, s))
    wspec = lambda k, n: pl.BlockSpec((None, k, n), lambda s, i: (s, 0, 0))

    w3 = pl.pallas_call(
        _s5_drive_body,
        grid=(nsg, nch // rb),
        in_specs=[xspec, wspec(n_in, S5_SG * 4 * P)],
        out_specs=pl.BlockSpec((rb, S5_SG, 4 * P), lambda s, i: (i, s, 0)),
        out_shape=jax.ShapeDtypeStruct((nch, g, 4 * P), F32),
        compiler_params=_cparams(("parallel", "parallel")),
        name="s5_drive",
    )(u3, wfull)

    n_xc, n_cc = seq // L, ctx_len // L
    cb = _pick(math.gcd(n_xc, n_cc), (16, 8, 4, 2, 1))
    sf, sb = s5_scan(w3, lam, n_batch=n_batch, n_xc=n_xc, n_cc=n_cc, cb=cb)

    sspec = pl.BlockSpec((rb, S5_SG, 2 * P), lambda s, i: (i, s, 0))
    y3 = pl.pallas_call(
        _s5_out_body,
        grid=(nsg, nch // rb),
        in_specs=[xspec, sspec, sspec, wspec(n_in, n_in), wspec(S5_SG * 4 * P, n_in)],
        out_specs=xspec,
        out_shape=jax.ShapeDtypeStruct((nch, L, w_tot), F32),
        compiler_params=_cparams(("parallel", "parallel")),
        name="s5_out",
    )(u3, sf, sb, tfull, qfull)
    return y3.reshape(t_rows, w_tot)


R_E1, R_E2, R_W1, R_W2, R_RANK1, R_RANK2 = range(6)


def _router_body(lg_ref, b_ref, meta_ref, cnt_ref, carry_ref, *, n_exp):
    @pl.when(pl.program_id(0) == 0)
    def _():
        carry_ref[...] = jnp.zeros_like(carry_ref)

    lg = lg_ref[...] + b_ref[...]
    tr = lg.shape[0]
    lane = lax.broadcasted_iota(jnp.int32, lg.shape, 1).astype(F32)
    neg = jnp.float32(-jnp.inf)
    lg = jnp.where(lane < n_exp, lg, neg)
    m1 = jnp.max(lg, axis=-1, keepdims=True)
    i1 = jnp.min(jnp.where(lg == m1, lane, float(LANE)), axis=-1, keepdims=True)
    lg2 = jnp.where(lane == i1, neg, lg)
    m2 = jnp.max(lg2, axis=-1, keepdims=True)
    i2 = jnp.min(jnp.where(lg2 == m2, lane, float(LANE)), axis=-1, keepdims=True)
    e2 = jnp.exp(m2 - m1)
    den = 1.0 + e2
    sel = jnp.where((lane == i1) | (lane == i2), 1.0, 0.0)
    r_i = lax.broadcasted_iota(jnp.int32, (tr, tr), 0)
    c_i = lax.broadcasted_iota(jnp.int32, (tr, tr), 1)
    tri = jnp.where(r_i > c_i, 1.0, 0.0).astype(BF16)
    before = jnp.dot(tri, sel.astype(BF16), preferred_element_type=F32) + carry_ref[...]
    rank1 = jnp.sum(jnp.where(lane == i1, before, 0.0), axis=-1, keepdims=True)
    rank2 = jnp.sum(jnp.where(lane == i2, before, 0.0), axis=-1, keepdims=True)
    total = carry_ref[...] + jnp.sum(sel, axis=0, keepdims=True)
    carry_ref[...] = total
    cnt_ref[...] = jnp.broadcast_to(total, cnt_ref.shape)
    meta = jnp.zeros_like(lg)
    for idx, val in ((R_E1, i1), (R_E2, i2), (R_W1, 1.0 / den), (R_W2, e2 / den),
                     (R_RANK1, rank1), (R_RANK2, rank2)):
        meta = jnp.where(lane == idx, val, meta)
    meta_ref[...] = meta


def router_top2(logits, b_router_pad, *, n_exp, tr=512):
    m = logits.shape[0]
    tr = _pick(m, (tr, 256, 128, 64, 32, 16, 8))
    return pl.pallas_call(
        functools.partial(_router_body, n_exp=n_exp),
        grid=(m // tr,),
        in_specs=[pl.BlockSpec((tr, LANE), lambda i: (i, 0)), pl.BlockSpec((1, LANE), lambda i: (0, 0))],
        out_specs=[pl.BlockSpec((tr, LANE), lambda i: (i, 0)), pl.BlockSpec((8, LANE), lambda i: (0, 0))],
        out_shape=[jax.ShapeDtypeStruct((m, LANE), F32), jax.ShapeDtypeStruct((8, LANE), F32)],
        scratch_shapes=[pltpu.VMEM((1, LANE), F32)],
        compiler_params=_cparams(("arbitrary",)),
        name="router_top2",
    )(logits, b_router_pad)


DMA_ISSUE_UNROLL = 8


def _rows_wait(src_ref, dst_ref, sem):
    pltpu.make_async_copy(src_ref.at[pl.ds(0, dst_ref.shape[0]), :], dst_ref, sem).wait()


def _fetch_rows(idx_ref, idx_next_ref, src_ref, buf, sem, *, tb, n_per):
    i = pl.program_id(0)
    slot = lax.rem(i, 2)

    def start_block(ref, s):
        def issue(t, carry):
            for k in range(n_per):
                pltpu.make_async_copy(src_ref.at[pl.ds(ref[0, n_per * t + k], 1), :],
                                      buf.at[s, k, pl.ds(t, 1), :], sem.at[s]).start()
            return carry

        lax.fori_loop(0, tb, issue, 0, unroll=DMA_ISSUE_UNROLL // n_per)

    @pl.when(i == 0)
    def _():
        start_block(idx_ref, slot)

    @pl.when(i + 1 < pl.num_programs(0))
    def _():
        start_block(idx_next_ref, 1 - slot)

    for k in range(n_per):
        _rows_wait(src_ref, buf.at[slot, k], sem.at[slot])
    return slot


def _idx_specs(n_blocks, width):
    return [pl.BlockSpec((None, 1, width), lambda i: (i, 0, 0), memory_space=pltpu.SMEM),
            pl.BlockSpec((None, 1, width), lambda i: (jnp.minimum(i + 1, n_blocks - 1), 0, 0),
                         memory_space=pltpu.SMEM)]


def _gather_rows_body(idx_ref, idx_next_ref, src_ref, o_ref, buf, sem, *, tb):
    slot = _fetch_rows(idx_ref, idx_next_ref, src_ref, buf, sem, tb=tb, n_per=1)
    o_ref[...] = buf[slot, 0].astype(o_ref.dtype)


def gather_rows(src, idx, *, out_dtype, tb=256):
    r, d = idx.shape[0], src.shape[1]
    tb = _pick(r, (tb, 128, 64, 32, 16, 8))
    idx3 = idx.reshape(r // tb, 1, tb)
    return pl.pallas_call(
        functools.partial(_gather_rows_body, tb=tb),
        grid=(r // tb,),
        in_specs=_idx_specs(r // tb, tb) + [pl.BlockSpec(memory_space=pl.ANY)],
        out_specs=pl.BlockSpec((tb, d), lambda i: (i, 0)),
        out_shape=jax.ShapeDtypeStruct((r, d), out_dtype),
        scratch_shapes=[pltpu.VMEM((2, 1, tb, d), src.dtype), pltpu.SemaphoreType.DMA((2,))],
        compiler_params=_cparams(("arbitrary",)),
        name="moe_gather_rows",
    )(idx3, idx3, src)


def _gather_combine_body(pos_ref, pos_next_ref, meta_ref, y_ref, x_ref, gate_ref, g_ref, b_ref, *rest,
                         tb, alpha, with_h):
    if with_h:
        sh_ref, sc_ref, xo_ref, h_ref, buf, sem = rest
    else:
        xo_ref, buf, sem = rest
    slot = _fetch_rows(pos_ref, pos_next_ref, y_ref, buf, sem, tb=tb, n_per=TOP_K)
    meta = meta_ref[...]
    ff = meta[:, R_W1:R_W1 + 1] * buf[slot, 0] + meta[:, R_W2:R_W2 + 1] * buf[slot, 1]
    xn = _ln_rows(alpha * x_ref[...] + gate_ref[...] * ff) * g_ref[...] + b_ref[...]
    xo_ref[...] = xn
    if with_h:
        h_ref[...] = (_ln_rows(xn) * (1.0 + sc_ref[...]) + sh_ref[...]).astype(h_ref.dtype)


def gather_combine_ln(y, pos, meta, x, gate, ln_g, ln_b, shift, scale, *, alpha, rows_per_group, tb=256):
    m, d = pos.shape[0], y.shape[1]
    tb = _pick(m, (tb, 128, 64, 32, 16, 8))
    assert rows_per_group % tb == 0
    pos3 = pos.reshape(m // tb, 1, TOP_K * tb)
    with_h = shift is not None
    row = pl.BlockSpec((tb, d), lambda i: (i, 0))
    vec = pl.BlockSpec((1, d), lambda i: (0, 0))
    gvec = pl.BlockSpec((None, 1, d), lambda i: ((i * tb) // rows_per_group, 0, 0))
    in_specs = _idx_specs(m // tb, TOP_K * tb) + [pl.BlockSpec((tb, LANE), lambda i: (i, 0)),
                                                  pl.BlockSpec(memory_space=pl.ANY), row, gvec, vec, vec]
    args = [pos3, pos3, meta, y, x, gate, ln_g.reshape(1, d), ln_b.reshape(1, d)]
    out_specs, out_shape = [row], [jax.ShapeDtypeStruct((m, d), F32)]
    if with_h:
        in_specs += [gvec, gvec]
        args += [shift, scale]
        out_specs.append(row)
        out_shape.append(jax.ShapeDtypeStruct((m, d), BF16))
    outs = pl.pallas_call(
        functools.partial(_gather_combine_body, tb=tb, alpha=alpha, with_h=with_h),
        grid=(m // tb,),
        in_specs=in_specs,
        out_specs=out_specs,
        out_shape=out_shape,
        scratch_shapes=[pltpu.VMEM((2, TOP_K, tb, d), F32), pltpu.SemaphoreType.DMA((2,))],
        compiler_params=_cparams(("arbitrary",)),
        name="moe_gather_combine_ln",
    )(*args)
    return (outs[0], outs[1]) if with_h else (outs[0], None)


def _gmm_up_body(te_ref, nv_ref, a_ref, b1_ref, b3_ref, o_ref):
    del te_ref
    live = pl.program_id(1) < nv_ref[0]

    @pl.when(live)
    def _():
        a = a_ref[...]
        g = jnp.dot(a, b1_ref[...].astype(BF16), preferred_element_type=F32)
        u = jnp.dot(a, b3_ref[...].astype(BF16), preferred_element_type=F32)
        o_ref[...] = (g * jax.nn.sigmoid(g) * u).astype(o_ref.dtype)

    @pl.when(jnp.logical_not(live))
    def _():
        o_ref[...] = jnp.zeros_like(o_ref)


def _gmm_down_body(te_ref, nv_ref, a_ref, b_ref, o_ref):
    del te_ref
    live = pl.program_id(1) < nv_ref[0]

    @pl.when(live)
    def _():
        o_ref[...] = jnp.dot(a_ref[...], b_ref[...].astype(BF16), preferred_element_type=F32)

    @pl.when(jnp.logical_not(live))
    def _():
        o_ref[...] = jnp.zeros_like(o_ref)


def grouped_swiglu(xs, w1, w3, w2, tile_expert, n_valid, *, tm):
    r, d = xs.shape
    f = w1.shape[2]
    n_tiles = r // tm
    tn_up = _pick(f, (512, 256, 128))
    tn_dn = _pick(d, (1024, 512, 256, 128))
    act = pl.pallas_call(
        _gmm_up_body,
        grid_spec=pltpu.PrefetchScalarGridSpec(
            num_scalar_prefetch=2,
            grid=(f // tn_up, n_tiles),
            in_specs=[pl.BlockSpec((tm, d), lambda j, i, te, nv: (i, 0)),
                      pl.BlockSpec((None, d, tn_up), lambda j, i, te, nv: (te[i], 0, j)),
                      pl.BlockSpec((None, d, tn_up), lambda j, i, te, nv: (te[i], 0, j))],
            out_specs=pl.BlockSpec((tm, tn_up), lambda j, i, te, nv: (i, j))),
        out_shape=jax.ShapeDtypeStruct((r, f), BF16),
        compiler_params=_cparams(("arbitrary", "arbitrary")),
        name="moe_up",
    )(tile_expert, n_valid, xs, w1, w3)
    return pl.pallas_call(
        _gmm_down_body,
        grid_spec=pltpu.PrefetchScalarGridSpec(
            num_scalar_prefetch=2,
            grid=(d // tn_dn, n_tiles),
            in_specs=[pl.BlockSpec((tm, f), lambda j, i, te, nv: (i, 0)),
                      pl.BlockSpec((None, f, tn_dn), lambda j, i, te, nv: (te[i], 0, j))],
            out_specs=pl.BlockSpec((tm, tn_dn), lambda j, i, te, nv: (i, j))),
        out_shape=jax.ShapeDtypeStruct((r, d), F32),
        compiler_params=_cparams(("arbitrary", "arbitrary")),
        name="moe_down",
    )(tile_expert, n_valid, act, w2)


def sparse_moe(hf, meta, counts, w1, w3, w2, resid, *, tm=512):
    m = hf.shape[0]
    n_exp = w1.shape[0]
    tm = _pick(m, (tm, 256, 128))
    cnt = counts.astype(jnp.int32)
    padded = (cnt + tm - 1) // tm * tm
    ends = jnp.cumsum(padded)
    offs = ends - padded
    e1 = meta[:, R_E1].astype(jnp.int32)
    e2 = meta[:, R_E2].astype(jnp.int32)
    pos = jnp.stack([offs[e1] + meta[:, R_RANK1].astype(jnp.int32),
                     offs[e2] + meta[:, R_RANK2].astype(jnp.int32)], axis=1)
    n_tiles = TOP_K * m // tm + n_exp
    tile_expert = jnp.minimum(jnp.searchsorted(ends, jnp.arange(n_tiles) * tm, side="right"),
                              n_exp - 1).astype(jnp.int32)
    n_valid = (ends[-1:] // tm).astype(jnp.int32)
    src_tok = jnp.zeros((n_tiles * tm,), jnp.int32).at[pos.reshape(-1)].set(
        jnp.repeat(jnp.arange(m, dtype=jnp.int32), TOP_K))
    xs = gather_rows(hf, src_tok, out_dtype=BF16)
    y = grouped_swiglu(xs, w1, w3, w2, tile_expert, n_valid, tm=tm)
    args, kwargs = resid
    return gather_combine_ln(y, pos, meta, *args, **kwargs)


def _rms_pro(x, gain):
    return x * lax.rsqrt(jnp.mean(x * x, axis=-1, keepdims=True) + RMS_EPS) * gain


def _gelu_tanh(x):
    return 0.5 * x * (1.0 + jnp.tanh(math.sqrt(2.0 / math.pi) * (x + 0.044715 * (x * x * x))))


def _rope_apply(x, cos, sin_up, sin_dn):
    n = x.shape[-1]
    return x * cos + pltpu.roll(x, n - QK_ROPE // 4, 1) * sin_up + pltpu.roll(x, QK_ROPE // 4, 1) * sin_dn


def _rope_heads_t(x, cos, sin_up, sin_dn):
    q4 = QK_ROPE // 4
    parts = []
    for base in range(0, x.shape[0], HEAD_PAD):
        seg = x[base + QK_NOPE:base + QK_NOPE + QK_ROPE]
        rot = seg * cos + pltpu.roll(seg, QK_ROPE - q4, 0) * sin_up + pltpu.roll(seg, q4, 0) * sin_dn
        parts += [x[base:base + QK_NOPE], rot, x[base + QK_NOPE + QK_ROPE:base + HEAD_PAD]]
    return jnp.concatenate(parts, axis=0)


def _rope_tables(n_batch, seq, n_ctx_rows):
    nf = QK_ROPE // 4
    pos = jnp.arange(seq)
    row = (pos // GRID_W).astype(F32)
    col = (pos % GRID_W).astype(F32)
    inv = ROPE_THETA ** (-jnp.arange(nf, dtype=F32) / nf)
    ar, ac = row[:, None] * inv, col[:, None] * inv
    z = jnp.zeros((seq, nf), F32)
    cos64 = jnp.concatenate([jnp.cos(ar), jnp.cos(ar), jnp.cos(ac), jnp.cos(ac)], axis=1)
    up64 = jnp.concatenate([-jnp.sin(ar), z, -jnp.sin(ac), z], axis=1)
    dn64 = jnp.concatenate([z, jnp.sin(ar), z, jnp.sin(ac)], axis=1)

    def place(t64, fill):
        full = jnp.full((seq, HEAD_PAD), fill, F32).at[:, QK_NOPE:QK_NOPE + QK_ROPE].set(t64)
        full = jnp.tile(full, (n_batch, 1))
        return jnp.concatenate([full, jnp.full((n_ctx_rows, HEAD_PAD), fill, F32)], axis=0)

    return place(cos64, 1.0), place(up64, 0.0), place(dn64, 0.0)


def _pad_cols(w, n):
    return jnp.pad(w, ((0, 0), (0, n - w.shape[1])))


def _head_cat_cols(w_a, w_b, da, db):
    k = w_a.shape[0]
    parts = [w_a.reshape(k, N_HEADS, da)]
    if w_b is not None:
        parts.append(w_b.reshape(k, N_HEADS, db))
    used = da + (db if w_b is not None else 0)
    parts.append(jnp.zeros((k, N_HEADS, HEAD_PAD - used), w_a.dtype))
    return jnp.concatenate(parts, axis=2).reshape(k, N_HEADS * HEAD_PAD)


def kernel(x, c, ctx, c_ctx, w_mod, b_mod, w_in, b_gate, q_norm, w_uq, kv_norm, w_ukv, w_branch_mla,
           s5_a_re, s5_a_im, s5_log_dt, s5_b_re, s5_b_im, s5_c_re, s5_c_im, s5_d, w_glu, b_glu,
           w_branch_s5, w_out, ln_mix_g, ln_mix_b, ln_ffn_g, ln_ffn_b, ffn_w1, ffn_w3, ffn_w2,
           moe_w_router, moe_b_router, moe_w1, moe_w3, moe_w2):
    B, N, D = x.shape
    C = ctx.shape[1]
    depth = w_mod.shape[0]
    QL, KL = q_norm.shape[1], kv_norm.shape[1]
    SW = s5_d.shape[1]
    H = N_HEADS
    NX, NC_ROWS = B * N, B * C
    T = NX + NC_ROWS
    alpha = (2 * depth) ** 0.25
    q_scale = (QK_NOPE + QK_ROPE) ** -0.5 * math.log2(math.e)
    o_ckv, o_kr, o_u, o_g = QL, QL + KL, QL + KL + QK_ROPE, QL + KL + QK_ROPE + SW
    assert N % C == 0 and N % 256 == 0 and NC_ROWS % 256 == 0

    tm_all = _pick(T, (1536, 1024, 768, 512, 384, 256, 128))
    tm_x = _pick(NX, (1024, 512, 256, 128))
    tile_n = lambda n: _pick(n, (512, 256, 128))

    n_cond = B + 1
    cond = jnp.concatenate([c, c_ctx[None], jnp.zeros((-n_cond % 8, D), F32)], axis=0)
    mods = mod_vectors(cond, w_mod, b_mod)
    mods = mods.reshape(depth, cond.shape[0], 6, D)

    def mvec(l, k):
        return mods[l, :n_cond, k][:, None, :]

    rope_cos, rope_up, rope_dn = _rope_tables(B, N, NC_ROWS)
    rope_cos_t, rope_up_t, rope_dn_t = (t[:, QK_NOPE:QK_NOPE + QK_ROPE].T for t in (rope_cos, rope_up, rope_dn))
    xt =jnp.concatenate([x.reshape(NX, D), ctx.reshape(NC_ROWS, D)], axis=0)
    h = modulate_ln(xt, mvec(0, 0), mvec(0, 1), rows_per_group=N)

    for l in range(depth):
        need_ctx = l < depth - 1
        rows = T if need_ctx else NX
        tm_r = tm_all if need_ctx else tm_x

        wi = w_in[l]
        w_cq = wi[:, :o_ckv].astype(BF16)
        w_ckv = wi[:, o_ckv:o_kr].astype(BF16)
        w_kr = jnp.concatenate(
            [jnp.zeros((D, QK_NOPE), F32), wi[:, o_kr:o_u],
             jnp.zeros((D, HEAD_PAD - QK_NOPE - QK_ROPE), F32)], axis=1).astype(BF16)
        w_u = wi[:, o_u:o_g].astype(BF16)
        w_gm = wi[:, o_g:o_g + D].astype(BF16)
        w_gs = wi[:, o_g + D:].astype(BF16)
        wq = w_uq[l].reshape(QL, H, QK_NOPE + QK_ROPE)
        w_q = _head_cat_cols(wq[:, :, :QK_NOPE].reshape(QL, -1), wq[:, :, QK_NOPE:].reshape(QL, -1),
                             QK_NOPE, QK_ROPE).astype(BF16)
        wkv = w_ukv[l].reshape(KL, H, QK_NOPE + V_HEAD)
        w_k = _head_cat_cols(wkv[:, :, :QK_NOPE].reshape(KL, -1), None, QK_NOPE, 0).astype(BF16)
        w_v = wkv[:, :, QK_NOPE:].reshape(KL, H * V_HEAD).astype(BF16)

        cqn = matmul([h], [w_cq], out_dtype=BF16, tm=tm_r // 2, tn=QL, m_rows=rows,
                     epilogue=_rms_pro, extras=((q_norm[l].reshape(1, QL), "col"),), name="in_cq")
        ckvn = matmul([h], [w_ckv], out_dtype=BF16, tm=tm_all, tn=KL,
                      epilogue=_rms_pro, extras=((kv_norm[l].reshape(1, KL), "col"),), name="in_ckv")
        krp = matmul([h], [w_kr], out_dtype=F32, tm=tm_all, tn=HEAD_PAD, epilogue=_rope_apply,
                     extras=((rope_cos, "rowtab"), (rope_up, "rowtab"), (rope_dn, "rowtab")), name="in_kr")
        u = matmul([h], [w_u], out_dtype=F32, tm=tm_all, tn=tile_n(SW), name="in_s5")

        qt = matmul([w_q.T], [cqn], nt=True, out_dtype=BF16, tm=_pick(H * HEAD_PAD, (1024, 512, 256)), tn=512,
                    n_cols=rows, epilogue=lambda acc, cs, up, dn: _rope_heads_t(acc, cs, up, dn) * q_scale,
                    extras=((rope_cos_t, "coltab"), (rope_up_t, "coltab"), (rope_dn_t, "coltab")),
                    name="mla_qt")
        kh = matmul([ckvn], [w_k], out_dtype=BF16, tm=tm_all, tn=512,
                    epilogue=lambda acc, kr: acc + jnp.tile(kr, (1, acc.shape[1] // HEAD_PAD)),
                    extras=((krp, "rowtab"),), name="mla_k")
        vt = matmul([w_v.T], [ckvn], nt=True, out_dtype=BF16, tm=_pick(H * V_HEAD, (1024, 512, 256, 128)),
                    tn=512, name="mla_vt")
        tq = _pick(N, (512, 256, 128))
        o_x = flash_attention(qt, kh, vt, n_batch=B, q_row0=0, q_len=N, segs=[(0, N), (NX, C)],
                              tq=tq, tk=1024, name="flash_x")
        if need_ctx:
            o_c = flash_attention(qt, kh, vt, n_batch=B, q_row0=NX, q_len=C, segs=[(NX, C)],
                                  tq=_pick(C, (256, 128)), tk=512, name="flash_ctx")
            o_all = jnp.concatenate([o_x, o_c], axis=0)
        else:
            o_all = o_x

        tables = _s5_tables(s5_a_re[l], s5_a_im[l], s5_log_dt[l], s5_b_re[l], s5_b_im[l],
                            s5_c_re[l], s5_c_im[l])
        y = s5_mix(u, tables, n_batch=B, seq=N, ctx_len=C)
        d_row = s5_d[l].reshape(1, SW)
        glu_pro = lambda yv, uv, dv: _gelu_tanh(yv + dv * uv)
        ys = matmul([y, u], [w_glu[l].astype(BF16)], out_dtype=BF16, tm=tm_r // 2, tn=tile_n(SW), m_rows=rows,
                    prologue=glu_pro, pro_consts=(d_row,), keep_pro=True,
                    epilogue=lambda acc, bv, gg: gg * jax.nn.sigmoid(acc + bv),
                    extras=((b_glu[l].reshape(1, SW), "col"),), name="s5_glu")

        bg = b_gate[l]
        m1 = matmul([o_all], [w_branch_mla[l].astype(BF16)], out_dtype=F32, tm=tm_r, tn=512, m_rows=rows,
                    name="branch_mla")
        gm = matmul([h], [w_gm], out_dtype=F32, tm=tm_r, tn=512, m_rows=rows,
                    epilogue=lambda acc, bv, mv: jax.nn.sigmoid(acc + bv) * mv,
                    extras=((bg[:D].reshape(1, D), "col"), (m1, "tile")), name="gate_mla")
        m2 = matmul([ys], [w_branch_s5[l].astype(BF16)], out_dtype=F32, tm=tm_r, tn=512, m_rows=rows,
                    name="branch_s5")
        merged = matmul([h], [w_gs], out_dtype=BF16, tm=tm_r, tn=512, m_rows=rows,
                        epilogue=lambda acc, bv, mv, pv: jax.nn.sigmoid(acc + bv) * mv + pv,
                        extras=((bg[D:].reshape(1, D), "col"), (m2, "tile"), (gm, "tile")), name="gate_s5")
        mix = matmul([merged], [w_out[l].astype(BF16)], out_dtype=F32, tm=tm_r, tn=512, m_rows=rows,
                     name="out_proj")
        xt, h2 = residual_ln(xt, mix, mvec(l, 2), ln_mix_g[l], ln_mix_b[l], mvec(l, 3), mvec(l, 4),
                             alpha=alpha, rows_per_group=N, m_rows=rows, h_dtype=BF16 if l % 2 == 0 else F32)

        if l % 2 == 0:
            fi = l // 2
            dff = ffn_w1.shape[2]
            act = matmul([h2], [ffn_w1[fi].astype(BF16), ffn_w3[fi].astype(BF16)], out_dtype=BF16, tm=tm_r,
                         tn=256, m_rows=rows, epilogue=lambda a, b: a * jax.nn.sigmoid(a) * b, name="ffn_up")
            tk = max(k for k in range(LANE, 5633, LANE) if dff % k == 0)
            ff = matmul_ksplit(act, ffn_w2[fi].astype(BF16), out_dtype=F32, tm=tm_r // 2, tn=1024, tk=tk,
                               m_rows=rows, name="ffn_down")
        else:
            mi = l // 2
            n_exp = moe_w_router.shape[2]
            w_r = _pad_cols(moe_w_router[mi], LANE).astype(BF16)
            b_r = jnp.pad(moe_b_router[mi], (0, LANE - n_exp)).reshape(1, LANE)
            logits = matmul([h2], [w_r], out_dtype=F32, tm=tm_r // 2, tn=LANE, m_rows=rows,
                            prologue=lambda a: a, name="router_logits")
            meta, counts = router_top2(logits, b_r, n_exp=n_exp)
            ff = None
        nxt = (mvec(l + 1, 0), mvec(l + 1, 1)) if need_ctx else (None, None)
        if ff is None:
            resid = ((xt[:rows], mvec(l, 5), ln_ffn_g[l], ln_ffn_b[l], *nxt), dict(alpha=alpha, rows_per_group=N))
            xt, h = sparse_moe(h2, meta, counts[0, :n_exp], moe_w1[mi], moe_w3[mi], moe_w2[mi], resid)
        else:
            xt, h = residual_ln(xt, ff, mvec(l, 5), ln_ffn_g[l], ln_ffn_b[l], *nxt,
                                alpha=alpha, rows_per_group=N, m_rows=rows)
    return xt[:NX].reshape(B, N, D)
```

```python
import functools
import math

import jax
import jax.numpy as jnp
from jax import lax
from jax.experimental import pallas as pl
from jax.experimental.pallas import tpu as pltpu

N_HEADS = 32
QK_NOPE = 128
QK_ROPE = 64
V_HEAD = 128
ROPE_THETA = 10000.0
GRID_W = 64
S5_GROUP = 16
S5_STATE = 64
TOP_K = 2
LN_EPS = 1e-6
RMS_EPS = 1e-6

HEAD_PAD = 256
S5_L = 16
FLASH_ONES = 16
LANE = 128
S5_SG = LANE // S5_GROUP
VMEM_LIMIT_BYTES = 56 * 2**20
FFN_TK_MAX = 5632

F32 = jnp.float32
BF16 = jnp.bfloat16


def _cparams(sem):
    return pltpu.CompilerParams(dimension_semantics=sem, vmem_limit_bytes=VMEM_LIMIT_BYTES)


def _pick(n, prefs):
    for p in prefs:
        if n % p == 0:
            return p
    raise ValueError(f"no tile in {prefs} divides {n}")


def _mm_body(*refs, n_a, n_pc, n_b, kinds, prologue, epilogue, tn, nt, keep_pro):
    a_refs = refs[:n_a]
    pc_refs = refs[n_a:n_a + n_pc]
    b_refs = refs[n_a + n_pc:n_a + n_pc + n_b]
    ex_refs = refs[n_a + n_pc + n_b:n_a + n_pc + n_b + len(kinds)]
    o_ref = refs[n_a + n_pc + n_b + len(kinds)]
    pro_tile = []
    if prologue is not None:
        a_s = refs[n_a + n_pc + n_b + len(kinds) + 1]
        a_f = refs[n_a + n_pc + n_b + len(kinds) + 2] if keep_pro else None
        j = pl.program_id(1)

        @pl.when(j == 0)
        def _():
            val = prologue(*[r[...] for r in a_refs], *[r[...] for r in pc_refs])
            a_s[...] = val.astype(BF16)
            if keep_pro:
                a_f[...] = val

        a = a_s[...]
        if keep_pro:
            pro_tile = [a_f[:, pl.ds(pl.multiple_of(j * tn, tn), tn)]]
    else:
        a = a_refs[0][...]
    if nt:
        accs = [lax.dot_general(a, b[...], (((1,), (1,)), ((), ())), preferred_element_type=F32)
                for b in b_refs]
    else:
        accs = [jnp.dot(a, b[...].astype(BF16), preferred_element_type=F32) for b in b_refs]
    exs = []
    for r, kind in zip(ex_refs, kinds):
        v = r[...]
        if kind == "rowtab_tiled":
            v = jnp.tile(v, (1, tn // v.shape[1]))
        exs.append(v)
    o_ref[...] = epilogue(*accs, *exs, *pro_tile).astype(o_ref.dtype)


def matmul(a_list, b_list, *, out_dtype, tm, tn, m_rows=None, n_cols=None, nt=False, epilogue=None,
           extras=(), prologue=None, pro_consts=(), keep_pro=False, name="mm"):
    K = a_list[0].shape[1]
    N = b_list[0].shape[0 if nt else 1] if n_cols is None else n_cols
    M = a_list[0].shape[0] if m_rows is None else m_rows
    tn = _pick(N, tuple(t for t in (tn, 512, 256, 128) if t <= tn))
    assert M % tm == 0 and N % tn == 0, (M, tm, N, tn)
    if epilogue is None:
        epilogue = lambda acc: acc
    if prologue is None:
        assert len(a_list) == 1 and a_list[0].dtype == BF16
    in_specs = [pl.BlockSpec((tm, K), lambda i, j: (i, 0)) for _ in a_list]
    in_specs += [pl.BlockSpec(c.shape, lambda i, j: (0, 0)) for c in pro_consts]
    if nt:
        in_specs += [pl.BlockSpec((tn, K), lambda i, j: (j, 0)) for _ in b_list]
    else:
        in_specs += [pl.BlockSpec((K, tn), lambda i, j: (0, j)) for _ in b_list]
    kinds = []
    ex_arrays = []
    for arr, kind in extras:
        kinds.append(kind)
        ex_arrays.append(arr)
        if kind == "tile":
            in_specs.append(pl.BlockSpec((tm, tn), lambda i, j: (i, j)))
        elif kind == "col":
            in_specs.append(pl.BlockSpec((1, tn), lambda i, j: (0, j)))
        elif kind in ("rowtab", "rowtab_tiled"):
            in_specs.append(pl.BlockSpec((tm, arr.shape[1]), lambda i, j: (i, 0)))
        elif kind == "coltab":
            in_specs.append(pl.BlockSpec((arr.shape[0], tn), lambda i, j: (0, j)))
        else:
            raise ValueError(kind)
    scratch = [pltpu.VMEM((tm, K), BF16)] if prologue is not None else []
    if keep_pro:
        assert prologue is not None and N == K
        scratch.append(pltpu.VMEM((tm, K), F32))
    body = functools.partial(_mm_body, n_a=len(a_list), n_pc=len(pro_consts), n_b=len(b_list), kinds=tuple(kinds),
                             prologue=prologue, epilogue=epilogue, tn=tn, nt=nt, keep_pro=keep_pro)
    return pl.pallas_call(
        body,
        grid=(M // tm, N // tn),
        in_specs=in_specs,
        out_specs=pl.BlockSpec((tm, tn), lambda i, j: (i, j)),
        out_shape=jax.ShapeDtypeStruct((M, N), out_dtype),
        scratch_shapes=scratch,
        compiler_params=_cparams(("parallel", "arbitrary")),
        name=name,
    )(*a_list, *pro_consts, *b_list, *ex_arrays)


def _mmk_body(a_ref, b_ref, *rest, n_ex, epilogue, nk):
    ex_refs = rest[:n_ex]
    o_ref = rest[n_ex]
    acc_ref = rest[n_ex + 1]
    k = pl.program_id(2)

    @pl.when(k == 0)
    def _():
        acc_ref[...] = jnp.zeros_like(acc_ref)

    acc_ref[...] += jnp.dot(a_ref[...], b_ref[...], preferred_element_type=F32)

    @pl.when(k == nk - 1)
    def _():
        o_ref[...] = epilogue(acc_ref[...], *[e[...] for e in ex_refs]).astype(o_ref.dtype)


def matmul_ksplit(a, b, *, out_dtype, tm, tn, tk, m_rows=None, epilogue=None, extras=(), name="mmk"):
    K = a.shape[1]
    N = b.shape[1]
    M = a.shape[0] if m_rows is None else m_rows
    tn = _pick(N, tuple(t for t in (tn, 512, 256, 128) if t <= tn))
    assert M % tm == 0 and N % tn == 0 and K % tk == 0, (M, tm, N, tn, K, tk)
    if epilogue is None:
        epilogue = lambda acc: acc
    in_specs = [pl.BlockSpec((tm, tk), lambda i, j, k: (i, k)),
                pl.BlockSpec((tk, tn), lambda i, j, k: (k, j))]
    ex_arrays = []
    for arr, kind in extras:
        ex_arrays.append(arr)
        if kind == "tile":
            in_specs.append(pl.BlockSpec((tm, tn), lambda i, j, k: (i, j)))
        elif kind == "rowtab":
            in_specs.append(pl.BlockSpec((tm, arr.shape[1]), lambda i, j, k: (i, 0)))
        else:
            raise ValueError(kind)
    nk = K // tk
    body = functools.partial(_mmk_body, n_ex=len(ex_arrays), epilogue=epilogue, nk=nk)
    return pl.pallas_call(
        body,
        grid=(M // tm, N // tn, nk),
        in_specs=in_specs,
        out_specs=pl.BlockSpec((tm, tn), lambda i, j, k: (i, j)),
        out_shape=jax.ShapeDtypeStruct((M, N), out_dtype),
        scratch_shapes=[pltpu.VMEM((tm, tn), F32)],
        compiler_params=_cparams(("parallel", "parallel", "arbitrary")),
        name=name,
    )(a, b, *ex_arrays)


def _mod_body(c_ref, w_ref, b_ref, o_ref):
    c = c_ref[...]
    act = (c * jax.nn.sigmoid(c)).astype(BF16)
    o_ref[...] = jnp.dot(act, w_ref[...].astype(BF16), preferred_element_type=F32) + b_ref[...]


def mod_vectors(cond, w_mod, b_mod):
    depth, d, n = w_mod.shape
    r = cond.shape[0]
    tn = _pick(n, (1024, 512, 256, 128))
    return pl.pallas_call(
        _mod_body,
        grid=(depth, n // tn),
        in_specs=[pl.BlockSpec((r, d), lambda l, j: (0, 0)),
                  pl.BlockSpec((None, d, tn), lambda l, j: (l, 0, j)),
                  pl.BlockSpec((None, 1, tn), lambda l, j: (l, 0, j))],
        out_specs=pl.BlockSpec((None, r, tn), lambda l, j: (l, 0, j)),
        out_shape=jax.ShapeDtypeStruct((depth, r, n), F32),
        compiler_params=_cparams(("parallel", "parallel")),
        name="mod_vectors",
    )(cond, w_mod, b_mod.reshape(depth, 1, n))


def _ln_rows(x):
    mu = jnp.mean(x, axis=-1, keepdims=True)
    xc = x - mu
    var = jnp.mean(xc * xc, axis=-1, keepdims=True)
    return xc * lax.rsqrt(var + LN_EPS)


def _modln_body(x_ref, sh_ref, sc_ref, h_ref):
    h_ref[...] = (_ln_rows(x_ref[...]) * (1.0 + sc_ref[...]) + sh_ref[...]).astype(h_ref.dtype)


def modulate_ln(x, shift, scale, *, rows_per_group, m_rows=None, tr=256):
    M = x.shape[0] if m_rows is None else m_rows
    d = x.shape[1]
    assert M % tr == 0 and rows_per_group % tr == 0
    gmap = lambda i: ((i * tr) // rows_per_group, 0, 0)
    return pl.pallas_call(
        _modln_body,
        grid=(M // tr,),
        in_specs=[pl.BlockSpec((tr, d), lambda i: (i, 0)),
                  pl.BlockSpec((None, 1, d), gmap),
                  pl.BlockSpec((None, 1, d), gmap)],
        out_specs=pl.BlockSpec((tr, d), lambda i: (i, 0)),
        out_shape=jax.ShapeDtypeStruct((M, d), BF16),
        compiler_params=_cparams(("parallel",)),
        name="modulate_ln",
    )(x, shift, scale)


def _resln_body(x_ref, y_ref, gate_ref, g_ref, b_ref, sh_ref, sc_ref, xo_ref, h_ref, *, alpha):
    xn = _ln_rows(alpha * x_ref[...] + gate_ref[...] * y_ref[...]) * g_ref[...] + b_ref[...]
    xo_ref[...] = xn
    h_ref[...] = (_ln_rows(xn) * (1.0 + sc_ref[...]) + sh_ref[...]).astype(h_ref.dtype)


def _resln_last_body(x_ref, y_ref, gate_ref, g_ref, b_ref, xo_ref, *, alpha):
    xo_ref[...] = _ln_rows(alpha * x_ref[...] + gate_ref[...] * y_ref[...]) * g_ref[...] + b_ref[...]


def residual_ln(x, y, gate, ln_g, ln_b, shift, scale, *, alpha, rows_per_group, m_rows=None, tr=256,
                h_dtype=None):
    M = x.shape[0] if m_rows is None else m_rows
    d = x.shape[1]
    assert M % tr == 0 and rows_per_group % tr == 0
    gmap = lambda i: ((i * tr) // rows_per_group, 0, 0)
    row = pl.BlockSpec((tr, d), lambda i: (i, 0))
    vec = pl.BlockSpec((1, d), lambda i: (0, 0))
    gvec = pl.BlockSpec((None, 1, d), gmap)
    if shift is None:
        return pl.pallas_call(
            functools.partial(_resln_last_body, alpha=alpha),
            grid=(M // tr,),
            in_specs=[row, row, gvec, vec, vec],
            out_specs=row,
            out_shape=jax.ShapeDtypeStruct((M, d), F32),
            compiler_params=_cparams(("parallel",)),
            name="residual_ln_last",
        )(x, y, gate, ln_g.reshape(1, d), ln_b.reshape(1, d)), None
    return pl.pallas_call(
        functools.partial(_resln_body, alpha=alpha),
        grid=(M // tr,),
        in_specs=[row, row, gvec, vec, vec, gvec, gvec],
        out_specs=[row, row],
        out_shape=[jax.ShapeDtypeStruct((M, d), F32), jax.ShapeDtypeStruct((M, d), h_dtype or BF16)],
        compiler_params=_cparams(("parallel",)),
        name="residual_ln",
    )(x, y, gate, ln_g.reshape(1, d), ln_b.reshape(1, d), shift, scale)


def _flash_body(qt_ref, *refs, seg_lens, tk):
    n_seg = len(seg_lens)
    kv_refs = refs[:2 * n_seg]
    o_ref = refs[2 * n_seg]
    acc_ref, st_a, st_b = refs[2 * n_seg + 1:]
    bufs = (st_a, st_b)
    qt = qt_ref[...]
    tq = qt.shape[1]
    acc_ref[...] = jnp.zeros_like(acc_ref)
    m = jnp.full((1, tq), -jnp.inf, F32)

    def scores(buf, seg, c, tks):
        k = kv_refs[2 * seg][pl.ds(pl.multiple_of(c * tks, tks), tks), :]
        st = jnp.dot(k, qt, preferred_element_type=F32)
        buf[:tks] = st
        return jnp.max(st, axis=0, keepdims=True)

    def absorb(buf, cmax, seg, c, tks, m_old):
        vt = kv_refs[2 * seg + 1][:, pl.ds(pl.multiple_of(c * tks, tks), tks)]
        vt1 = jnp.concatenate([vt, jnp.ones((FLASH_ONES, tks), BF16)], axis=0)
        m_new = jnp.maximum(m_old, cmax)
        p = jnp.exp2(buf[:tks] - m_new).astype(BF16)
        corr = jnp.exp2(m_old - m_new)
        acc_ref[...] = corr * acc_ref[...] + jnp.dot(vt1, p, preferred_element_type=F32)
        return m_new

    def run_static(chunks, cur, cmax, m):
        for i, (seg, c, tks) in enumerate(chunks):
            nxt = scores(bufs[1 - cur], *chunks[i + 1]) if i + 1 < len(chunks) else None
            m = absorb(bufs[cur], cmax, seg, c, tks, m)
            cur, cmax = 1 - cur, nxt
        return m

    chunk_counts = [(s, min(tk, ln), ln // min(tk, ln)) for s, ln in enumerate(seg_lens)]
    s0, tk0, n0 = chunk_counts[0]
    rest = [(s, c, tks) for s, tks, n in chunk_counts[1:] for c in range(n)]
    cmax_a = scores(st_a, s0, 0, tk0)
    if n0 >= 4 and n0 % 2 == 0:
        def pair(j, carry):
            m, cmax_a = carry
            c0 = 2 * j
            cmax_b = scores(st_b, s0, c0 + 1, tk0)
            m = absorb(st_a, cmax_a, s0, c0, tk0, m)
            cmax_a = scores(st_a, s0, c0 + 2, tk0)
            return absorb(st_b, cmax_b, s0, c0 + 1, tk0, m), cmax_a

        m, cmax_a = lax.fori_loop(0, n0 // 2 - 1, pair, (m, cmax_a))
        m = run_static([(s0, n0 - 2, tk0), (s0, n0 - 1, tk0)] + rest, 0, cmax_a, m)
    else:
        m = run_static([(s0, c, tk0) for c in range(n0)] + rest, 0, cmax_a, m)
    acc = acc_ref[...]
    o_ref[...] = (acc[:V_HEAD] / acc[V_HEAD:V_HEAD + 1]).T.astype(o_ref.dtype)


def flash_attention(qt, k, vt, *, n_batch, q_row0, q_len, segs, tq, tk, name="flash"):
    h = N_HEADS
    nq = q_len // tq
    assert q_len % tq == 0 and q_row0 % tq == 0
    in_specs = [pl.BlockSpec((HEAD_PAD, tq), lambda b, hh, i: (hh, q_row0 // tq + b * nq + i))]
    args = [qt]
    for row0, ln in segs:
        assert row0 % ln == 0
        in_specs.append(pl.BlockSpec((ln, HEAD_PAD), lambda b, hh, i, r=row0 // ln: (r + b, hh)))
        in_specs.append(pl.BlockSpec((V_HEAD, ln), lambda b, hh, i, r=row0 // ln: (hh, r + b)))
        args += [k, vt]
    body = functools.partial(_flash_body, seg_lens=tuple(ln for _, ln in segs), tk=tk)
    return pl.pallas_call(
        body,
        grid=(n_batch, h, nq),
        in_specs=in_specs,
        out_specs=pl.BlockSpec((tq, V_HEAD), lambda b, hh, i: (b * nq + i, hh)),
        out_shape=jax.ShapeDtypeStruct((n_batch * q_len, h * V_HEAD), BF16),
        scratch_shapes=[pltpu.VMEM((V_HEAD + FLASH_ONES, tq), F32),
                        pltpu.VMEM((tk, tq), F32), pltpu.VMEM((tk, tq), F32)],
        compiler_params=_cparams(("parallel", "parallel", "arbitrary")),
        name=name,
    )(*args)


def _chunk_rows(x_ref):
    return jnp.concatenate([x_ref[:, t, :] for t in range(S5_L)], axis=1).astype(BF16)


def _s5_drive_body(x_ref, w_ref, o_ref):
    res = jnp.dot(_chunk_rows(x_ref), w_ref[...], preferred_element_type=F32)
    for gl in range(S5_SG):
        o_ref[:, gl, :] = res[:, gl * 4 * S5_STATE:(gl + 1) * 4 * S5_STATE]


def _s5_out_body(x_ref, sf_ref, sb_ref, t_ref, q_ref, o_ref):
    s = jnp.concatenate([sf_ref[:, gl, :] for gl in range(S5_SG)]
                        + [sb_ref[:, gl, :] for gl in range(S5_SG)], axis=1).astype(BF16)
    res = (jnp.dot(_chunk_rows(x_ref), t_ref[...], preferred_element_type=F32)
           + jnp.dot(s, q_ref[...], preferred_element_type=F32))
    for t in range(S5_L):
        o_ref[:, t, :] = res[:, t * LANE:(t + 1) * LANE]


def _s5_scan_body(wf_ref, wb_ref, af_ref, bf_ref, ab_ref, bb_ref, sf_ref, sb_ref, st_f, st_b):
    @pl.when(pl.program_id(1) == 0)
    def _():
        st_f[...] = jnp.zeros_like(st_f)
        st_b[...] = jnp.zeros_like(st_b)

    a_f, b_f, a_b, b_b = af_ref[...], bf_ref[...], ab_ref[...], bb_ref[...]
    cb = wf_ref.shape[0]

    def step(c, carry):
        s_f, s_b = carry
        cr = cb - 1 - c
        sf_ref[c] = s_f
        sb_ref[cr] = s_b
        n_f = a_f * s_f + b_f * pltpu.roll(s_f, S5_STATE, 1) + wf_ref[c]
        n_b = a_b * s_b + b_b * pltpu.roll(s_b, S5_STATE, 1) + wb_ref[cr]
        return n_f, n_b

    s_f, s_b = lax.fori_loop(0, cb, step, (st_f[...], st_b[...]))
    st_f[...] = s_f
    st_b[...] = s_b


def s5_scan(w3, lam, *, n_batch, n_xc, n_cc, cb):
    nch, g, _ = w3.shape
    p2 = 2 * S5_STATE
    n_xb, n_cb = n_xc // cb, n_cc // cb
    ctx0 = n_batch * n_xb

    def fwd_blk(b, j):
        return jnp.where(j < n_cb, ctx0 + b * n_cb + j, b * n_xb + j - n_cb)

    def bwd_blk(b, j):
        return jnp.where(j < n_cb, ctx0 + b * n_cb + (n_cb - 1 - j), b * n_xb + (n_xb - 1 - (j - n_cb)))

    coef = pl.BlockSpec((g, p2), lambda b, j: (0, 0))
    return pl.pallas_call(
        _s5_scan_body,
        grid=(n_batch, n_xb + n_cb),
        in_specs=[pl.BlockSpec((cb, g, p2), lambda b, j: (fwd_blk(b, j), 0, 0)),
                  pl.BlockSpec((cb, g, p2), lambda b, j: (bwd_blk(b, j), 0, 1)),
                  coef, coef, coef, coef],
        out_specs=[pl.BlockSpec((cb, g, p2), lambda b, j: (fwd_blk(b, j), 0, 0)),
                   pl.BlockSpec((cb, g, p2), lambda b, j: (bwd_blk(b, j), 0, 0))],
        out_shape=[jax.ShapeDtypeStruct((nch, g, p2), F32)] * 2,
        scratch_shapes=[pltpu.VMEM((g, p2), F32), pltpu.VMEM((g, p2), F32)],
        compiler_params=_cparams(("arbitrary", "arbitrary")),
        name="s5_scan",
    )(w3, w3, *lam)


def _s5_tables(a_re, a_im, log_dt, b_re, b_im, c_re, c_im):
    L, P, Hh = S5_L, S5_STATE, S5_GROUP
    hp = lax.Precision.HIGHEST
    dt = jnp.exp(log_dt.astype(F32))[..., None]
    ar, ai = a_re.astype(F32), a_im.astype(F32)
    j = jnp.arange(L + 1, dtype=F32)[:, None, None, None]
    mag = jnp.exp(j * ar * dt)
    pr, pi = mag * jnp.cos(j * ai * dt), mag * jnp.sin(j * ai * dt)
    lr, li = pr[1], pi[1]
    nr = lr - 1.0
    den = ar * ar + ai * ai
    f_re = ((nr * ar + li * ai) / den)[..., None]
    f_im = ((li * ar - nr * ai) / den)[..., None]
    br, bi = b_re.astype(F32), b_im.astype(F32)
    bb_re = f_re * br - f_im * bi
    bb_im = f_re * bi + f_im * br
    cr, ci = c_re.astype(F32), c_im.astype(F32)

    zr = pr[:L, ..., None] * bb_re - pi[:L, ..., None] * bb_im
    zi = pr[:L, ..., None] * bb_im + pi[:L, ..., None] * bb_re
    kj = (jnp.einsum('dghp,jdgpk->dgkjh', cr, zr, precision=hp)
          - jnp.einsum('dghp,jdgpk->dgkjh', ci, zi, precision=hp))
    g = kj.shape[1]
    kall = jnp.concatenate([jnp.flip(kj[1][:, :, 1:], axis=2), kj[0][:, :, :1] + kj[1][:, :, :1],
                            kj[0][:, :, 1:]], axis=2)
    tmat = jnp.stack([kall[:, :, L - 1 - k:2 * L - 1 - k] for k in range(L)], axis=1)
    tmat = tmat.reshape(g, L * Hh, L * Hh)

    kk = jnp.arange(L)
    pf_r, pf_i = pr[L - 1 - kk, 0], pi[L - 1 - kk, 0]
    pb_r, pb_i = pr[kk, 1], pi[kk, 1]
    wf_re = pf_r[..., None] * bb_re[0] - pf_i[..., None] * bb_im[0]
    wf_im = pf_r[..., None] * bb_im[0] + pf_i[..., None] * bb_re[0]
    wb_re = pb_r[..., None] * bb_re[1] - pb_i[..., None] * bb_im[1]
    wb_im = pb_r[..., None] * bb_im[1] + pb_i[..., None] * bb_re[1]
    wcat = jnp.concatenate([wf_re, wf_im, wb_re, wb_im], axis=2)
    wmat = jnp.transpose(wcat, (1, 0, 3, 2)).reshape(g, L * Hh, 4 * P)

    qf_r, qf_i = pr[kk + 1, 0], pi[kk + 1, 0]
    qb_r, qb_i = pr[L - kk, 1], pi[L - kk, 1]

    def qpair(c_r, c_i, q_r, q_i):
        return (c_r[None] * q_r[:, :, None, :] - c_i[None] * q_i[:, :, None, :],
                -c_r[None] * q_i[:, :, None, :] - c_i[None] * q_r[:, :, None, :])

    qf_re, qf_im = qpair(cr[0], ci[0], qf_r, qf_i)
    qb_re, qb_im = qpair(cr[1], ci[1], qb_r, qb_i)
    qf = jnp.transpose(jnp.concatenate([qf_re, qf_im], axis=3), (1, 3, 0, 2))
    qb = jnp.transpose(jnp.concatenate([qb_re, qb_im], axis=3), (1, 3, 0, 2))

    n_in = L * S5_SG * Hh
    lane = jnp.arange(n_in)
    rep_t = (jnp.arange(L * Hh)[:, None] == ((lane // LANE) * Hh + lane % Hh)[None, :]).astype(BF16)
    rep_w = (jnp.arange(4 * P)[:, None] == (lane % (4 * P))[None, :]).astype(BF16)
    col_grp_t = (lane // Hh) % S5_SG
    col_grp_w = lane // (4 * P)
    wfull = s5_spread([wmat.astype(BF16)], rep_w, col_grp_w, chunk_rows=True)
    tfull = s5_spread([tmat.astype(BF16)], rep_t, col_grp_t, chunk_rows=True)
    qfull = s5_spread([qf.reshape(g, 2 * P, L * Hh).astype(BF16), qb.reshape(g, 2 * P, L * Hh).astype(BF16)],
                      rep_t, col_grp_t, chunk_rows=False)

    def lam_tiles(d):
        return (jnp.concatenate([pr[L, d], pr[L, d]], axis=-1), jnp.concatenate([-pi[L, d], pi[L, d]], axis=-1))

    lam = lam_tiles(0) + lam_tiles(1)
    return wfull, tfull, qfull, lam


def _s5_spread_body(*refs, n_src, chunk_rows):
    src_refs = refs[:n_src]
    rep_ref, cgrp_ref, o_ref = refs[n_src:]
    if chunk_rows:
        pieces = [src_refs[0][a, k * S5_GROUP:(k + 1) * S5_GROUP, :] for k in range(S5_L) for a in range(S5_SG)]
        per_grp = S5_GROUP
    else:
        pieces = [r[a] for r in src_refs for a in range(S5_SG)]
        per_grp = src_refs[0].shape[1]
    rows = jnp.concatenate(pieces, axis=0)
    full = jnp.dot(rows, rep_ref[...], preferred_element_type=F32)
    assert per_grp & (per_grp - 1) == 0 and S5_SG & (S5_SG - 1) == 0
    row_grp = (lax.broadcasted_iota(jnp.int32, full.shape, 0) >> (per_grp.bit_length() - 1)) & (S5_SG - 1)
    o_ref[...] = jnp.where(row_grp == cgrp_ref[...], full, 0.0).astype(o_ref.dtype)


def s5_spread(srcs, rep, col_grp, *, chunk_rows):
    g, r, c = srcs[0].shape
    n = rep.shape[1]
    n_rows = len(srcs) * S5_SG * r
    return pl.pallas_call(
        functools.partial(_s5_spread_body, n_src=len(srcs), chunk_rows=chunk_rows),
        grid=(g // S5_SG,),
        in_specs=[pl.BlockSpec((S5_SG, r, c), lambda s: (s, 0, 0)) for _ in srcs]
        + [pl.BlockSpec((c, n), lambda s: (0, 0)), pl.BlockSpec((1, n), lambda s: (0, 0))],
        out_specs=pl.BlockSpec((None, n_rows, n), lambda s: (s, 0, 0)),
        out_shape=jax.ShapeDtypeStruct((g // S5_SG, n_rows, n), BF16),
        compiler_params=_cparams(("parallel",)),
        name="s5_spread",
    )(*srcs, rep, col_grp.astype(jnp.int32).reshape(1, n))


def s5_mix(u, tables, *, n_batch, seq, ctx_len):
    wfull, tfull, qfull, lam = tables
    L, P = S5_L, S5_STATE
    t_rows, w_tot = u.shape
    g = w_tot // S5_GROUP
    nsg = g // S5_SG
    nch = t_rows // L
    n_in = L * LANE
    rb = max(r for r in range(8, 265, 8) if nch % r == 0)
    u3 = u.reshape(nch, L, w_tot)
    xspec = pl.BlockSpec((rb, L, LANE), lambda s, i: (i, 0, s))
    wspec = lambda k, n: pl.BlockSpec((None, k, n), lambda s, i: (s, 0, 0))

    w3 = pl.pallas_call(
        _s5_drive_body,
        grid=(nsg, nch // rb),
        in_specs=[xspec, wspec(n_in, S5_SG * 4 * P)],
        out_specs=pl.BlockSpec((rb, S5_SG, 4 * P), lambda s, i: (i, s, 0)),
        out_shape=jax.ShapeDtypeStruct((nch, g, 4 * P), F32),
        compiler_params=_cparams(("parallel", "parallel")),
        name="s5_drive",
    )(u3, wfull)

    n_xc, n_cc = seq // L, ctx_len // L
    cb = _pick(math.gcd(n_xc, n_cc), (16, 8, 4, 2, 1))
    sf, sb = s5_scan(w3, lam, n_batch=n_batch, n_xc=n_xc, n_cc=n_cc, cb=cb)

    sspec = pl.BlockSpec((rb, S5_SG, 2 * P), lambda s, i: (i, s, 0))
    y3 = pl.pallas_call(
        _s5_out_body,
        grid=(nsg, nch // rb),
        in_specs=[xspec, sspec, sspec, wspec(n_in, n_in), wspec(S5_SG * 4 * P, n_in)],
        out_specs=xspec,
        out_shape=jax.ShapeDtypeStruct((nch, L, w_tot), F32),
        compiler_params=_cparams(("parallel", "parallel")),
        name="s5_out",
    )(u3, sf, sb, tfull, qfull)
    return y3.reshape(t_rows, w_tot)


R_E1, R_E2, R_W1, R_W2, R_RANK1, R_RANK2 = range(6)


def _router_body(lg_ref, b_ref, meta_ref, cnt_ref, carry_ref, *, n_exp):
    @pl.when(pl.program_id(0) == 0)
    def _():
        carry_ref[...] = jnp.zeros_like(carry_ref)

    lg = lg_ref[...] + b_ref[...]
    tr = lg.shape[0]
    lane = lax.broadcasted_iota(jnp.int32, lg.shape, 1).astype(F32)
    neg = jnp.float32(-jnp.inf)
    lg = jnp.where(lane < n_exp, lg, neg)
    m1 = jnp.max(lg, axis=-1, keepdims=True)
    i1 = jnp.min(jnp.where(lg == m1, lane, float(LANE)), axis=-1, keepdims=True)
    lg2 = jnp.where(lane == i1, neg, lg)
    m2 = jnp.max(lg2, axis=-1, keepdims=True)
    i2 = jnp.min(jnp.where(lg2 == m2, lane, float(LANE)), axis=-1, keepdims=True)
    e2 = jnp.exp(m2 - m1)
    den = 1.0 + e2
    sel = jnp.where((lane == i1) | (lane == i2), 1.0, 0.0)
    r_i = lax.broadcasted_iota(jnp.int32, (tr, tr), 0)
    c_i = lax.broadcasted_iota(jnp.int32, (tr, tr), 1)
    tri = jnp.where(r_i > c_i, 1.0, 0.0).astype(BF16)
    before = jnp.dot(tri, sel.astype(BF16), preferred_element_type=F32) + carry_ref[...]
    rank1 = jnp.sum(jnp.where(lane == i1, before, 0.0), axis=-1, keepdims=True)
    rank2 = jnp.sum(jnp.where(lane == i2, before, 0.0), axis=-1, keepdims=True)
    total = carry_ref[...] + jnp.sum(sel, axis=0, keepdims=True)
    carry_ref[...] = total
    cnt_ref[...] = jnp.broadcast_to(total, cnt_ref.shape)
    meta = jnp.zeros_like(lg)
    for idx, val in ((R_E1, i1), (R_E2, i2), (R_W1, 1.0 / den), (R_W2, e2 / den),
                     (R_RANK1, rank1), (R_RANK2, rank2)):
        meta = jnp.where(lane == idx, val, meta)
    meta_ref[...] = meta


def router_top2(logits, b_router_pad, *, n_exp, tr=512):
    m = logits.shape[0]
    tr = _pick(m, (tr, 256, 128, 64, 32, 16, 8))
    return pl.pallas_call(
        functools.partial(_router_body, n_exp=n_exp),
        grid=(m // tr,),
        in_specs=[pl.BlockSpec((tr, LANE), lambda i: (i, 0)), pl.BlockSpec((1, LANE), lambda i: (0, 0))],
        out_specs=[pl.BlockSpec((tr, LANE), lambda i: (i, 0)), pl.BlockSpec((8, LANE), lambda i: (0, 0))],
        out_shape=[jax.ShapeDtypeStruct((m, LANE), F32), jax.ShapeDtypeStruct((8, LANE), F32)],
        scratch_shapes=[pltpu.VMEM((1, LANE), F32)],
        compiler_params=_cparams(("arbitrary",)),
        name="router_top2",
    )(logits, b_router_pad)


DMA_ISSUE_UNROLL = 8


def _rows_wait(src_ref, dst_ref, sem):
    pltpu.make_async_copy(src_ref.at[pl.ds(0, dst_ref.shape[0]), :], dst_ref, sem).wait()


def _fetch_rows(idx_ref, idx_next_ref, src_ref, buf, sem, *, tb, n_per):
    i = pl.program_id(0)
    slot = lax.rem(i, 2)

    def start_block(ref, s):
        def issue(t, carry):
            for k in range(n_per):
                pltpu.make_async_copy(src_ref.at[pl.ds(ref[0, n_per * t + k], 1), :],
                                      buf.at[s, k, pl.ds(t, 1), :], sem.at[s]).start()
            return carry

        lax.fori_loop(0, tb, issue, 0, unroll=DMA_ISSUE_UNROLL // n_per)

    @pl.when(i == 0)
    def _():
        start_block(idx_ref, slot)

    @pl.when(i + 1 < pl.num_programs(0))
    def _():
        start_block(idx_next_ref, 1 - slot)

    for k in range(n_per):
        _rows_wait(src_ref, buf.at[slot, k], sem.at[slot])
    return slot


def _idx_specs(n_blocks, width):
    return [pl.BlockSpec((None, 1, width), lambda i: (i, 0, 0), memory_space=pltpu.SMEM),
            pl.BlockSpec((None, 1, width), lambda i: (jnp.minimum(i + 1, n_blocks - 1), 0, 0),
                         memory_space=pltpu.SMEM)]


def _gather_rows_body(idx_ref, idx_next_ref, src_ref, o_ref, buf, sem, *, tb):
    slot = _fetch_rows(idx_ref, idx_next_ref, src_ref, buf, sem, tb=tb, n_per=1)
    o_ref[...] = buf[slot, 0].astype(o_ref.dtype)


def gather_rows(src, idx, *, out_dtype, tb=256):
    r, d = idx.shape[0], src.shape[1]
    tb = _pick(r, (tb, 128, 64, 32, 16, 8))
    idx3 = idx.reshape(r // tb, 1, tb)
    return pl.pallas_call(
        functools.partial(_gather_rows_body, tb=tb),
        grid=(r // tb,),
        in_specs=_idx_specs(r // tb, tb) + [pl.BlockSpec(memory_space=pl.ANY)],
        out_specs=pl.BlockSpec((tb, d), lambda i: (i, 0)),
        out_shape=jax.ShapeDtypeStruct((r, d), out_dtype),
        scratch_shapes=[pltpu.VMEM((2, 1, tb, d), src.dtype), pltpu.SemaphoreType.DMA((2,))],
        compiler_params=_cparams(("arbitrary",)),
        name="moe_gather_rows",
    )(idx3, idx3, src)


def _gather_combine_body(pos_ref, pos_next_ref, meta_ref, y_ref, x_ref, gate_ref, g_ref, b_ref, *rest,
                         tb, alpha, with_h):
    if with_h:
        sh_ref, sc_ref, xo_ref, h_ref, buf, sem = rest
    else:
        xo_ref, buf, sem = rest
    slot = _fetch_rows(pos_ref, pos_next_ref, y_ref, buf, sem, tb=tb, n_per=TOP_K)
    meta = meta_ref[...]
    ff = meta[:, R_W1:R_W1 + 1] * buf[slot, 0] + meta[:, R_W2:R_W2 + 1] * buf[slot, 1]
    xn = _ln_rows(alpha * x_ref[...] + gate_ref[...] * ff) * g_ref[...] + b_ref[...]
    xo_ref[...] = xn
    if with_h:
        h_ref[...] = (_ln_rows(xn) * (1.0 + sc_ref[...]) + sh_ref[...]).astype(h_ref.dtype)


def gather_combine_ln(y, pos, meta, x, gate, ln_g, ln_b, shift, scale, *, alpha, rows_per_group, tb=256):
    m, d = pos.shape[0], y.shape[1]
    tb = _pick(m, (tb, 128, 64, 32, 16, 8))
    assert rows_per_group % tb == 0
    pos3 = pos.reshape(m // tb, 1, TOP_K * tb)
    with_h = shift is not None
    row = pl.BlockSpec((tb, d), lambda i: (i, 0))
    vec = pl.BlockSpec((1, d), lambda i: (0, 0))
    gvec = pl.BlockSpec((None, 1, d), lambda i: ((i * tb) // rows_per_group, 0, 0))
    in_specs = _idx_specs(m // tb, TOP_K * tb) + [pl.BlockSpec((tb, LANE), lambda i: (i, 0)),
                                                  pl.BlockSpec(memory_space=pl.ANY), row, gvec, vec, vec]
    args = [pos3, pos3, meta, y, x, gate, ln_g.reshape(1, d), ln_b.reshape(1, d)]
    out_specs, out_shape = [row], [jax.ShapeDtypeStruct((m, d), F32)]
    if with_h:
        in_specs += [gvec, gvec]
        args += [shift, scale]
        out_specs.append(row)
        out_shape.append(jax.ShapeDtypeStruct((m, d), BF16))
    outs = pl.pallas_call(
        functools.partial(_gather_combine_body, tb=tb, alpha=alpha, with_h=with_h),
        grid=(m // tb,),
        in_specs=in_specs,
        out_specs=out_specs,
        out_shape=out_shape,
        scratch_shapes=[pltpu.VMEM((2, TOP_K, tb, d), F32), pltpu.SemaphoreType.DMA((2,))],
        compiler_params=_cparams(("arbitrary",)),
        name="moe_gather_combine_ln",
    )(*args)
    return (outs[0], outs[1]) if with_h else (outs[0], None)


def _gmm_up_body(te_ref, nv_ref, a_ref, b1_ref, b3_ref, o_ref):
    del te_ref
    live = pl.program_id(1) < nv_ref[0]

    @pl.when(live)
    def _():
        a = a_ref[...]
        g = jnp.dot(a, b1_ref[...].astype(BF16), preferred_element_type=F32)
        u = jnp.dot(a, b3_ref[...].astype(BF16), preferred_element_type=F32)
        o_ref[...] = (g * jax.nn.sigmoid(g) * u).astype(o_ref.dtype)

    @pl.when(jnp.logical_not(live))
    def _():
        o_ref[...] = jnp.zeros_like(o_ref)


def _gmm_down_body(te_ref, nv_ref, a_ref, b_ref, o_ref):
    del te_ref
    live = pl.program_id(1) < nv_ref[0]

    @pl.when(live)
    def _():
        o_ref[...] = jnp.dot(a_ref[...], b_ref[...].astype(BF16), preferred_element_type=F32)

    @pl.when(jnp.logical_not(live))
    def _():
        o_ref[...] = jnp.zeros_like(o_ref)


def grouped_swiglu(xs, w1, w3, w2, tile_expert, n_valid, *, tm):
    r, d = xs.shape
    f = w1.shape[2]
    n_tiles = r // tm
    tn_up = _pick(f, (512, 256, 128))
    tn_dn = _pick(d, (1024, 512, 256, 128))
    act = pl.pallas_call(
        _gmm_up_body,
        grid_spec=pltpu.PrefetchScalarGridSpec(
            num_scalar_prefetch=2,
            grid=(f // tn_up, n_tiles),
            in_specs=[pl.BlockSpec((tm, d), lambda j, i, te, nv: (i, 0)),
                      pl.BlockSpec((None, d, tn_up), lambda j, i, te, nv: (te[i], 0, j)),
                      pl.BlockSpec((None, d, tn_up), lambda j, i, te, nv: (te[i], 0, j))],
            out_specs=pl.BlockSpec((tm, tn_up), lambda j, i, te, nv: (i, j))),
        out_shape=jax.ShapeDtypeStruct((r, f), BF16),
        compiler_params=_cparams(("arbitrary", "arbitrary")),
        name="moe_up",
    )(tile_expert, n_valid, xs, w1, w3)
    return pl.pallas_call(
        _gmm_down_body,
        grid_spec=pltpu.PrefetchScalarGridSpec(
            num_scalar_prefetch=2,
            grid=(d // tn_dn, n_tiles),
            in_specs=[pl.BlockSpec((tm, f), lambda j, i, te, nv: (i, 0)),
                      pl.BlockSpec((None, f, tn_dn), lambda j, i, te, nv: (te[i], 0, j))],
            out_specs=pl.BlockSpec((tm, tn_dn), lambda j, i, te, nv: (i, j))),
        out_shape=jax.ShapeDtypeStruct((r, d), F32),
        compiler_params=_cparams(("arbitrary", "arbitrary")),
        name="moe_down",
    )(tile_expert, n_valid, act, w2)


def sparse_moe(hf, meta, counts, w1, w3, w2, resid, *, tm=512):
    m = hf.shape[0]
    n_exp = w1.shape[0]
    tm = _pick(m, (tm, 256, 128))
    cnt = counts.astype(jnp.int32)
    padded = (cnt + tm - 1) // tm * tm
    ends = jnp.cumsum(padded)
    offs = ends - padded
    e1 = meta[:, R_E1].astype(jnp.int32)
    e2 = meta[:, R_E2].astype(jnp.int32)
    pos = jnp.stack([offs[e1] + meta[:, R_RANK1].astype(jnp.int32),
                     offs[e2] + meta[:, R_RANK2].astype(jnp.int32)], axis=1)
    n_tiles = TOP_K * m // tm + n_exp
    tile_expert = jnp.minimum(jnp.searchsorted(ends, jnp.arange(n_tiles) * tm, side="right"),
                              n_exp - 1).astype(jnp.int32)
    n_valid = (ends[-1:] // tm).astype(jnp.int32)
    src_tok = jnp.zeros((n_tiles * tm,), jnp.int32).at[pos.reshape(-1)].set(
        jnp.repeat(jnp.arange(m, dtype=jnp.int32), TOP_K))
    xs = gather_rows(hf, src_tok, out_dtype=BF16)
    y = grouped_swiglu(xs, w1, w3, w2, tile_expert, n_valid, tm=tm)
    args, kwargs = resid
    return gather_combine_ln(y, pos, meta, *args, **kwargs)


def _rms_pro(x, gain):
    return x * lax.rsqrt(jnp.mean(x * x, axis=-1, keepdims=True) + RMS_EPS) * gain


def _gelu_tanh(x):
    return 0.5 * x * (1.0 + jnp.tanh(math.sqrt(2.0 / math.pi) * (x + 0.044715 * (x * x * x))))


def _rope_apply(x, cos, sin_up, sin_dn):
    n = x.shape[-1]
    return x * cos + pltpu.roll(x, n - QK_ROPE // 4, 1) * sin_up + pltpu.roll(x, QK_ROPE // 4, 1) * sin_dn


def _rope_heads_t(x, cos, sin_up, sin_dn):
    q4 = QK_ROPE // 4
    parts = []
    for base in range(0, x.shape[0], HEAD_PAD):
        seg = x[base + QK_NOPE:base + QK_NOPE + QK_ROPE]
        rot = seg * cos + pltpu.roll(seg, QK_ROPE - q4, 0) * sin_up + pltpu.roll(seg, q4, 0) * sin_dn
        parts += [x[base:base + QK_NOPE], rot, x[base + QK_NOPE + QK_ROPE:base + HEAD_PAD]]
    return jnp.concatenate(parts, axis=0)


def _rope_tables(n_batch, seq, n_ctx_rows):
    nf = QK_ROPE // 4
    pos = jnp.arange(seq)
    row = (pos // GRID_W).astype(F32)
    col = (pos % GRID_W).astype(F32)
    inv = ROPE_THETA ** (-jnp.arange(nf, dtype=F32) / nf)
    ar, ac = row[:, None] * inv, col[:, None] * inv
    z = jnp.zeros((seq, nf), F32)
    cos64 = jnp.concatenate([jnp.cos(ar), jnp.cos(ar), jnp.cos(ac), jnp.cos(ac)], axis=1)
    up64 = jnp.concatenate([-jnp.sin(ar), z, -jnp.sin(ac), z], axis=1)
    dn64 = jnp.concatenate([z, jnp.sin(ar), z, jnp.sin(ac)], axis=1)

    def place(t64, fill):
        full = jnp.full((seq, HEAD_PAD), fill, F32).at[:, QK_NOPE:QK_NOPE + QK_ROPE].set(t64)
        full = jnp.tile(full, (n_batch, 1))
        return jnp.concatenate([full, jnp.full((n_ctx_rows, HEAD_PAD), fill, F32)], axis=0)

    return place(cos64, 1.0), place(up64, 0.0), place(dn64, 0.0)


def _pad_cols(w, n):
    return jnp.pad(w, ((0, 0), (0, n - w.shape[1])))


def _head_cat_cols(w_a, w_b, da, db):
    k = w_a.shape[0]
    parts = [w_a.reshape(k, N_HEADS, da)]
    if w_b is not None:
        parts.append(w_b.reshape(k, N_HEADS, db))
    used = da + (db if w_b is not None else 0)
    parts.append(jnp.zeros((k, N_HEADS, HEAD_PAD - used), w_a.dtype))
    return jnp.concatenate(parts, axis=2).reshape(k, N_HEADS * HEAD_PAD)


def kernel(x, c, ctx, c_ctx, w_mod, b_mod, w_in, b_gate, q_norm, w_uq, kv_norm, w_ukv, w_branch_mla,
           s5_a_re, s5_a_im, s5_log_dt, s5_b_re, s5_b_im, s5_c_re, s5_c_im, s5_d, w_glu, b_glu,
           w_branch_s5, w_out, ln_mix_g, ln_mix_b, ln_ffn_g, ln_ffn_b, ffn_w1, ffn_w3, ffn_w2,
           moe_w_router, moe_b_router, moe_w1, moe_w3, moe_w2):
    B, N, D = x.shape
    C = ctx.shape[1]
    depth = w_mod.shape[0]
    QL, KL = q_norm.shape[1], kv_norm.shape[1]
    SW = s5_d.shape[1]
    H = N_HEADS
    NX, NC_ROWS = B * N, B * C
    T = NX + NC_ROWS
    alpha = (2 * depth) ** 0.25
    q_scale = (QK_NOPE + QK_ROPE) ** -0.5 * math.log2(math.e)
    o_ckv, o_kr, o_u, o_g = QL, QL + KL, QL + KL + QK_ROPE, QL + KL + QK_ROPE + SW
    assert N % C == 0 and N % 256 == 0 and NC_ROWS % 256 == 0

    tm_all = _pick(T, (1536, 1024, 768, 512, 384, 256, 128))
    tm_x = _pick(NX, (1024, 512, 256, 128))
    tile_n = lambda n: _pick(n, (512, 256, 128))

    n_cond = B + 1
    cond = jnp.concatenate([c, c_ctx[None], jnp.zeros((-n_cond % 8, D), F32)], axis=0)
    mods = mod_vectors(cond, w_mod, b_mod)
    mods = mods.reshape(depth, cond.shape[0], 6, D)

    def mvec(l, k):
        return mods[l, :n_cond, k][:, None, :]

    rope_cos, rope_up, rope_dn = _rope_tables(B, N, NC_ROWS)
    rope_cos_t, rope_up_t, rope_dn_t = (t[:, QK_NOPE:QK_NOPE + QK_ROPE].T for t in (rope_cos, rope_up, rope_dn))
    xt =jnp.concatenate([x.reshape(NX, D), ctx.reshape(NC_ROWS, D)], axis=0)
    h = modulate_ln(xt, mvec(0, 0), mvec(0, 1), rows_per_group=N)

    for l in range(depth):
        need_ctx = l < depth - 1
        rows = T if need_ctx else NX
        tm_r = tm_all if need_ctx else tm_x

        wi = w_in[l]
        w_cq = wi[:, :o_ckv].astype(BF16)
        w_ckv = wi[:, o_ckv:o_kr].astype(BF16)
        w_kr = jnp.concatenate(
            [jnp.zeros((D, QK_NOPE), F32), wi[:, o_kr:o_u],
             jnp.zeros((D, HEAD_PAD - QK_NOPE - QK_ROPE), F32)], axis=1).astype(BF16)
        w_u = wi[:, o_u:o_g].astype(BF16)
        w_gm = wi[:, o_g:o_g + D].astype(BF16)
        w_gs = wi[:, o_g + D:].astype(BF16)
        wq = w_uq[l].reshape(QL, H, QK_NOPE + QK_ROPE)
        w_q = _head_cat_cols(wq[:, :, :QK_NOPE].reshape(QL, -1), wq[:, :, QK_NOPE:].reshape(QL, -1),
                             QK_NOPE, QK_ROPE).astype(BF16)
        wkv = w_ukv[l].reshape(KL, H, QK_NOPE + V_HEAD)
        w_k = _head_cat_cols(wkv[:, :, :QK_NOPE].reshape(KL, -1), None, QK_NOPE, 0).astype(BF16)
        w_v = wkv[:, :, QK_NOPE:].reshape(KL, H * V_HEAD).astype(BF16)

        cqn = matmul([h], [w_cq], out_dtype=BF16, tm=tm_r // 2, tn=QL, m_rows=rows,
                     epilogue=_rms_pro, extras=((q_norm[l].reshape(1, QL), "col"),), name="in_cq")
        ckvn = matmul([h], [w_ckv], out_dtype=BF16, tm=tm_all, tn=KL,
                      epilogue=_rms_pro, extras=((kv_norm[l].reshape(1, KL), "col"),), name="in_ckv")
        krp = matmul([h], [w_kr], out_dtype=F32, tm=tm_all, tn=HEAD_PAD, epilogue=_rope_apply,
                     extras=((rope_cos, "rowtab"), (rope_up, "rowtab"), (rope_dn, "rowtab")), name="in_kr")
        u = matmul([h], [w_u], out_dtype=F32, tm=tm_all, tn=tile_n(SW), name="in_s5")

        qt = matmul([w_q.T], [cqn], nt=True, out_dtype=BF16, tm=_pick(H * HEAD_PAD, (1024, 512, 256)), tn=1536,
                    n_cols=rows, epilogue=lambda acc, cs, up, dn: _rope_heads_t(acc, cs, up, dn) * q_scale,
                    extras=((rope_cos_t, "coltab"), (rope_up_t, "coltab"), (rope_dn_t, "coltab")),
                    name="mla_qt")
        kh = matmul([ckvn], [w_k], out_dtype=BF16, tm=tm_all, tn=1024,
                    epilogue=lambda acc, kr: acc + jnp.tile(kr, (1, acc.shape[1] // HEAD_PAD)),
                    extras=((krp, "rowtab"),), name="mla_k")
        vt = matmul([w_v.T], [ckvn], nt=True, out_dtype=BF16, tm=_pick(H * V_HEAD, (1024, 512, 256, 128)),
                    tn=1536, name="mla_vt")
        tq = _pick(N, (512, 256, 128))
        o_x = flash_attention(qt, kh, vt, n_batch=B, q_row0=0, q_len=N, segs=[(0, N), (NX, C)],
                              tq=tq, tk=1024, name="flash_x")
        if need_ctx:
            o_c = flash_attention(qt, kh, vt, n_batch=B, q_row0=NX, q_len=C, segs=[(NX, C)],
                                  tq=_pick(C, (256, 128)), tk=512, name="flash_ctx")
            o_all = jnp.concatenate([o_x, o_c], axis=0)
        else:
            o_all = o_x

        tables = _s5_tables(s5_a_re[l], s5_a_im[l], s5_log_dt[l], s5_b_re[l], s5_b_im[l],
                            s5_c_re[l], s5_c_im[l])
        y = s5_mix(u, tables, n_batch=B, seq=N, ctx_len=C)
        d_row = s5_d[l].reshape(1, SW)
        glu_pro = lambda yv, uv, dv: _gelu_tanh(yv + dv * uv)
        ys = matmul([y, u], [w_glu[l].astype(BF16)], out_dtype=BF16, tm=tm_r // 2, tn=tile_n(SW), m_rows=rows,
                    prologue=glu_pro, pro_consts=(d_row,), keep_pro=True,
                    epilogue=lambda acc, bv, gg: gg * jax.nn.sigmoid(acc + bv),
                    extras=((b_glu[l].reshape(1, SW), "col"),), name="s5_glu")

        bg = b_gate[l]
        m1 = matmul([o_all], [w_branch_mla[l].astype(BF16)], out_dtype=F32, tm=tm_r, tn=512, m_rows=rows,
                    name="branch_mla")
        gm = matmul([h], [w_gm], out_dtype=F32, tm=tm_r, tn=512, m_rows=rows,
                    epilogue=lambda acc, bv, mv: jax.nn.sigmoid(acc + bv) * mv,
                    extras=((bg[:D].reshape(1, D), "col"), (m1, "tile")), name="gate_mla")
        m2 = matmul([ys], [w_branch_s5[l].astype(BF16)], out_dtype=F32, tm=tm_r, tn=512, m_rows=rows,
                    name="branch_s5")
        merged = matmul([h], [w_gs], out_dtype=BF16, tm=tm_r, tn=512, m_rows=rows,
                        epilogue=lambda acc, bv, mv, pv: jax.nn.sigmoid(acc + bv) * mv + pv,
                        extras=((bg[D:].reshape(1, D), "col"), (m2, "tile"), (gm, "tile")), name="gate_s5")
        mix = matmul([merged], [w_out[l].astype(BF16)], out_dtype=F32, tm=tm_r, tn=512, m_rows=rows,
                     name="out_proj")
        xt, h2 = residual_ln(xt, mix, mvec(l, 2), ln_mix_g[l], ln_mix_b[l], mvec(l, 3), mvec(l, 4),
                             alpha=alpha, rows_per_group=N, m_rows=rows, h_dtype=BF16 if l % 2 == 0 else F32)

        if l % 2 == 0:
            fi = l // 2
            dff = ffn_w1.shape[2]
            act = matmul([h2], [ffn_w1[fi].astype(BF16), ffn_w3[fi].astype(BF16)], out_dtype=BF16, tm=tm_r,
                         tn=256, m_rows=rows, epilogue=lambda a, b: a * jax.nn.sigmoid(a) * b, name="ffn_up")
            tk = max(k for k in range(LANE, FFN_TK_MAX + 1, LANE) if dff % k == 0)
            ff = matmul_ksplit(act, ffn_w2[fi].astype(BF16), out_dtype=F32, tm=tm_r // 2, tn=1024, tk=tk,
                               m_rows=rows, name="ffn_down")
        else:
            mi = l // 2
            n_exp = moe_w_router.shape[2]
            w_r = _pad_cols(moe_w_router[mi], LANE).astype(BF16)
            b_r = jnp.pad(moe_b_router[mi], (0, LANE - n_exp)).reshape(1, LANE)
            logits = matmul([h2], [w_r], out_dtype=F32, tm=tm_r // 2, tn=LANE, m_rows=rows,
                            prologue=lambda a: a, name="router_logits")
            meta, counts = router_top2(logits, b_r, n_exp=n_exp)
            ff = None
        nxt = (mvec(l + 1, 0), mvec(l + 1, 1)) if need_ctx else (None, None)
        if ff is None:
            resid = ((xt[:rows], mvec(l, 5), ln_ffn_g[l], ln_ffn_b[l], *nxt), dict(alpha=alpha, rows_per_group=N))
            xt, h = sparse_moe(h2, meta, counts[0, :n_exp], moe_w1[mi], moe_w3[mi], moe_w2[mi], resid)
        else:
            xt, h = residual_ln(xt, ff, mvec(l, 5), ln_ffn_g[l], ln_ffn_b[l], *nxt,
                                alpha=alpha, rows_per_group=N, m_rows=rows)
    return xt[:NX].reshape(B, N, D)
```
